```python
import math
import jax, jax.numpy as jnp
from jax import lax
import numpy as np

D_MODEL = 1024
BATCH = 8
SEQ = 2048
DEPTH = 1

CHUNK = 64
D_MIX = D_MODEL
RWKV_HEADS = 8
RWKV_HEAD_DIM = 64
RWKV_WIDTH = RWKV_HEADS * RWKV_HEAD_DIM
DECAY_RANK = 64
A_RANK = 64
GATE_RANK = 128
GN_EPS = 64e-5
ATTN_HEADS = 8
ATTN_HEAD_DIM = 64
ATTN_WIDTH = ATTN_HEADS * ATTN_HEAD_DIM
IDX_HEADS = 8
IDX_DIM = 64
MAX_TOPK = 256
Q_BLOCK = 64
REL_BUCKETS = 32
REL_MAX_DISTANCE = 128
D_FF = 4 * D_MODEL
PLE_DIM = 256
NORM_EPS = 1e-6
LN_EPS = 1e-6
NEG_INF = -1e30

RWKV_SPLITS = (RWKV_WIDTH, DECAY_RANK, RWKV_WIDTH, RWKV_WIDTH, A_RANK, GATE_RANK)
RWKV_COLS = sum(RWKV_SPLITS)
DSA_SPLITS = (ATTN_WIDTH, ATTN_WIDTH, ATTN_WIDTH, IDX_HEADS * IDX_DIM, IDX_DIM, IDX_HEADS)
DSA_COLS = sum(DSA_SPLITS)
IN_COLS = RWKV_COLS + DSA_COLS

kernel_name = 'hybrid_rwkv7_dsa_stream_block'


def _split(z, sizes):
    idx, acc = [], 0
    for s in sizes[:-1]:
        acc += s
        idx.append(acc)
    return jnp.split(z, idx, axis=-1)


def rms_norm(x, g):
    xf = x.astype(jnp.float32)
    y = xf * lax.rsqrt(jnp.mean(xf * xf, axis=-1, keepdims=True) + NORM_EPS) * g.astype(jnp.float32)
    return y.astype(x.dtype)


def t5_bucket(rel):
    half = REL_BUCKETS // 2
    max_exact = half // 2
    ret = jnp.where(rel > 0, half, 0)
    n = jnp.abs(rel)
    nf = jnp.maximum(n, 1).astype(jnp.float32)
    large = max_exact + (jnp.log(nf / max_exact) / math.log(REL_MAX_DISTANCE / max_exact)
                         * (half - max_exact)).astype(jnp.int32)
    large = jnp.minimum(large, half - 1)
    return ret + jnp.where(n < max_exact, n, large)


def rwkv7_time_mix(z, mu, w0, w2, a0, a2, g2, k_k, k_a, r_k, ln_g, ln_b):
    B, S, _ = z.shape
    H, N = RWKV_HEADS, RWKV_HEAD_DIM
    f32 = jnp.float32
    z = z.astype(f32)
    z_prev = jnp.pad(z, ((0, 0), (1, 0), (0, 0)))[:, :-1]
    z = z + (z_prev - z) * mu.astype(f32)
    r, zw, k, v, za, zg = _split(z, RWKV_SPLITS)
    w_log = -jax.nn.softplus(-(w0.astype(f32) + jnp.tanh(zw) @ w2.astype(f32))) - 0.5
    decay = jnp.exp(-jnp.exp(w_log))
    a = jax.nn.sigmoid(a0.astype(f32) + za @ a2.astype(f32))
    g = jax.nn.sigmoid(zg) @ g2.astype(f32)
    hs = lambda t: t.reshape(B, S, H, N)
    r, decay, k, v, a = hs(r), hs(decay), hs(k), hs(v), hs(a)
    kk = k * k_k.astype(f32).reshape(H, N)
    kk = kk / jnp.maximum(jnp.sqrt(jnp.sum(kk * kk, axis=-1, keepdims=True)), 1e-12)
    k = k * (1.0 + (a - 1.0) * k_a.astype(f32).reshape(H, N))

    def step(state, inp):
        r_t, w_t, k_t, v_t, kk_t, a_t = inp
        sa = jnp.einsum('bhvk,bhk->bhv', state, -kk_t)
        state = (state * w_t[:, :, None, :]
                 + sa[..., :, None] * (kk_t * a_t)[..., None, :]
                 + v_t[..., :, None] * k_t[..., None, :])
        return state, jnp.einsum('bhvk,bhk->bhv', state, r_t)

    seq_first = lambda t: jnp.swapaxes(t, 0, 1)
    state0 = jnp.zeros((B, H, N, N), f32)
    _, o = lax.scan(step, state0, (seq_first(r), seq_first(decay), seq_first(k),
                                   seq_first(v), seq_first(kk), seq_first(a)))
    o = seq_first(o)
    mean = jnp.mean(o, axis=-1, keepdims=True)
    var = jnp.mean(jnp.square(o - mean), axis=-1, keepdims=True)
    o = (o - mean) * lax.rsqrt(var + GN_EPS) * ln_g.astype(f32).reshape(H, N) + ln_b.astype(f32).reshape(H, N)
    o = o + jnp.sum(r * k * r_k.astype(f32), axis=-1, keepdims=True) * v
    return o.reshape(B, S, RWKV_WIDTH) * g


def dsa_attention(z, k_ln_g, k_ln_b, rel_bias):
    B, S, _ = z.shape
    f32 = jnp.float32
    q, k, v, qi, ki, wi = _split(z, DSA_SPLITS)
    q = q.reshape(B, S, ATTN_HEADS, ATTN_HEAD_DIM)
    k = k.reshape(B, S, ATTN_HEADS, ATTN_HEAD_DIM)
    v = v.reshape(B, S, ATTN_HEADS, ATTN_HEAD_DIM)
    qi = qi.reshape(B, S, IDX_HEADS, IDX_DIM).astype(f32)
    kf = ki.astype(f32)
    k_mean = jnp.mean(kf, axis=-1, keepdims=True)
    k_var = jnp.mean(jnp.square(kf - k_mean), axis=-1, keepdims=True)
    ki = (kf - k_mean) * lax.rsqrt(k_var + LN_EPS) * k_ln_g.astype(f32) + k_ln_b.astype(f32)
    wi = wi.astype(f32) * IDX_HEADS ** -0.5
    topk = min(MAX_TOPK, S // 4)
    n_blocks = S // Q_BLOCK
    key_pos = jnp.arange(S)
    scale = ATTN_HEAD_DIM ** -0.5
    idx_scale = IDX_DIM ** -0.5
    gather = jax.vmap(lambda t, i: t[i])
    table = rel_bias.astype(f32)

    def block(blk):
        t0 = blk * Q_BLOCK
        q_pos = t0 + jnp.arange(Q_BLOCK)
        chunk_end = (q_pos // CHUNK + 1) * CHUNK
        qb = lax.dynamic_slice_in_dim(q, t0, Q_BLOCK, axis=1).astype(f32)
        qib = lax.dynamic_slice_in_dim(qi, t0, Q_BLOCK, axis=1)
        wib = lax.dynamic_slice_in_dim(wi, t0, Q_BLOCK, axis=1)
        act = jax.nn.relu(jnp.einsum('bqhd,bsd->bqsh', qib, ki) * idx_scale)
        score = jnp.einsum('bqsh,bqh->bqs', act, wib)
        score = jnp.where(key_pos[None, None, :] < chunk_end[None, :, None], score, -jnp.inf)
        _, sel = lax.top_k(score, topk)
        valid = sel < chunk_end[None, :, None]
        k_sel = gather(k, sel).astype(f32)
        v_sel = gather(v, sel).astype(f32)
        bias = table[t5_bucket(sel - q_pos[None, :, None])]
        logits = jnp.einsum('bqhd,bqkhd->bqkh', qb, k_sel) * scale + bias
        logits = jnp.where(valid[..., None], logits, NEG_INF)
        probs = jax.nn.softmax(logits, axis=2)
        return jnp.einsum('bqkh,bqkhd->bqhd', probs, v_sel)

    o = lax.map(block, jnp.arange(n_blocks))
    return jnp.swapaxes(o, 0, 1).reshape(B, S, ATTN_WIDTH)


def setup_inputs(seed: int = 0) -> dict:
    key = jax.random.key(seed)
    ks = iter(jax.random.split(key, 32))
    nrm = lambda shape, s: jax.random.normal(next(ks), shape, jnp.float32) * s
    gain = lambda shape: 1.0 + nrm(shape, 0.05)
    H, N = RWKV_HEADS, RWKV_HEAD_DIM
    return {
        'x': nrm((BATCH, SEQ, D_MODEL), 1.0),
        'p': nrm((DEPTH, BATCH, SEQ, PLE_DIM), 1.0),
        'w_in': nrm((DEPTH, D_MODEL, IN_COLS), D_MODEL ** -0.5),
        'shift_mu': jax.random.uniform(next(ks), (DEPTH, RWKV_COLS), jnp.float32),
        'rwkv_w0': nrm((DEPTH, RWKV_WIDTH), 1.0) - 1.0,
        'rwkv_w2': nrm((DEPTH, DECAY_RANK, RWKV_WIDTH), 0.5 * DECAY_RANK ** -0.5),
        'rwkv_a0': nrm((DEPTH, RWKV_WIDTH), 0.1),
        'rwkv_a2': nrm((DEPTH, A_RANK, RWKV_WIDTH), 0.5 * A_RANK ** -0.5),
        'rwkv_g2': nrm((DEPTH, GATE_RANK, RWKV_WIDTH), GATE_RANK ** -0.5),
        'rwkv_k_k': 0.85 + nrm((DEPTH, RWKV_WIDTH), 0.05),
        'rwkv_k_a': 1.0 + nrm((DEPTH, RWKV_WIDTH), 0.05),
        'rwkv_r_k': nrm((DEPTH, H, N), 0.1),
        'rwkv_ln_g': gain((DEPTH, RWKV_WIDTH)),
        'rwkv_ln_b': nrm((DEPTH, RWKV_WIDTH), 0.02),
        'idx_k_ln_g': gain((DEPTH, IDX_DIM)),
        'idx_k_ln_b': nrm((DEPTH, IDX_DIM), 0.02),
        'rel_bias': nrm((REL_BUCKETS, ATTN_HEADS), 0.5),
        'w_out': nrm((DEPTH, D_MIX, D_MODEL), D_MIX ** -0.5),
        'w_ffn1': nrm((DEPTH, D_MODEL, D_FF), D_MODEL ** -0.5),
        'w_ffn2': nrm((DEPTH, D_FF, D_MODEL), D_FF ** -0.5),
        'w_ple': nrm((DEPTH, PLE_DIM, D_MODEL), PLE_DIM ** -0.5),
        'w_ple_gate': nrm((DEPTH, D_MODEL, D_MODEL), D_MODEL ** -0.5),
        'norm_mix_pre': gain((DEPTH, D_MODEL)),
        'norm_mix_post': gain((DEPTH, D_MODEL)),
        'norm_ffn_pre': gain((DEPTH, D_MODEL)),
        'norm_ffn_post': gain((DEPTH, D_MODEL)),
        'norm_ple_pre': gain((DEPTH, D_MODEL)),
        'norm_ple_post': gain((DEPTH, D_MODEL)),
    }


def reference(x, p, w_in, shift_mu, rwkv_w0, rwkv_w2, rwkv_a0, rwkv_a2, rwkv_g2,
              rwkv_k_k, rwkv_k_a, rwkv_r_k, rwkv_ln_g, rwkv_ln_b, idx_k_ln_g, idx_k_ln_b,
              rel_bias, w_out, w_ffn1, w_ffn2, w_ple, w_ple_gate,
              norm_mix_pre, norm_mix_post, norm_ffn_pre, norm_ffn_post,
              norm_ple_pre, norm_ple_post):
    for i in range(DEPTH):
        h = rms_norm(x, norm_mix_pre[i])
        z = h @ w_in[i]
        z_rwkv, z_dsa = z[..., :RWKV_COLS], z[..., RWKV_COLS:]
        y_rwkv = rwkv7_time_mix(z_rwkv, shift_mu[i], rwkv_w0[i], rwkv_w2[i], rwkv_a0[i],
                                rwkv_a2[i], rwkv_g2[i], rwkv_k_k[i], rwkv_k_a[i],
                                rwkv_r_k[i], rwkv_ln_g[i], rwkv_ln_b[i])
        y_dsa = dsa_attention(z_dsa, idx_k_ln_g[i], idx_k_ln_b[i], rel_bias)
        mix = jnp.concatenate([y_rwkv, y_dsa], axis=-1).astype(x.dtype) @ w_out[i]
        x = x + rms_norm(mix, norm_mix_post[i])
        h = rms_norm(x, norm_ffn_pre[i])
        f = jnp.square(jax.nn.relu(h @ w_ffn1[i])) @ w_ffn2[i]
        x = x + rms_norm(f, norm_ffn_post[i])
        gate = jax.nn.sigmoid(rms_norm(x, norm_ple_pre[i]) @ w_ple_gate[i])
        x = x + rms_norm((p[i] @ w_ple[i]) * gate, norm_ple_post[i])
    return x
```

```python
import functools
import math

import jax
import jax.numpy as jnp
from jax import lax
from jax.experimental import pallas as pl
from jax.experimental.pallas import tpu as pltpu

F32 = jnp.float32
BF16 = jnp.bfloat16
I32 = jnp.int32

CHUNK = 64
HEADS = 8
HEAD_DIM = 64
WIDTH = HEADS * HEAD_DIM
DECAY_RANK = 64
A_RANK = 64
GATE_RANK = 128
GN_EPS = 64e-5
IDX_HEADS = 8
IDX_DIM = 64
MAX_TOPK = 256
REL_BUCKETS = 32
REL_MAX_DISTANCE = 128
NORM_EPS = 1e-6
LN_EPS = 1e-6
NEG_INF = -1e30
RWKV_COLS = 1792
N_MAIN = RWKV_COLS + 3 * WIDTH
N_IDX = 640

LANES = 128
KEY_TILE = 256
FRONT_PAD = 2 * CHUNK
BACK_PAD = CHUNK
NEAR_VALID = 3 * CHUNK
VMEM_LIMIT = 56 * 1024 * 1024

INT_MIN = -(2**31)
INT_MAX = 2**31 - 1


def _nt(a, b):
    return lax.dot_general(a, b, (((1,), (1,)), ((), ())), preferred_element_type=F32)


def _tn(a, b):
    return lax.dot_general(a, b, (((0,), (0,)), ((), ())), preferred_element_type=F32)


def _mm(a, b):
    return jnp.dot(a, b, preferred_element_type=F32)


def _split2(x):
    hi = x.astype(BF16)
    lo = (x - hi.astype(F32)).astype(BF16)
    return hi, lo


def _split3(x):
    hi = x.astype(BF16)
    r = x - hi.astype(F32)
    mid = r.astype(BF16)
    lo = (r - mid.astype(F32)).astype(BF16)
    return hi, mid, lo


def _rms(x, g):
    return x * lax.rsqrt(jnp.mean(x * x, axis=-1, keepdims=True) + NORM_EPS) * g


def _sigmoid(x):
    return 1.0 / (1.0 + jnp.exp(-x))


def _in_proj_kernel(x_ref, g_ref, wm_ref, wih_ref, wil_ref, zr_ref, q_ref, k_ref, v_ref, qi_ref, kiw_ref):
    h = _rms(x_ref[...], g_ref[...])
    hh, hl = _split2(h)
    zm = _mm(hh, wm_ref[...])
    zr_ref[...] = zm[:, :RWKV_COLS]
    q_ref[...] = zm[:, RWKV_COLS:RWKV_COLS + WIDTH].astype(BF16)
    k_ref[...] = zm[:, RWKV_COLS + WIDTH:RWKV_COLS + 2 * WIDTH].astype(BF16)
    v_ref[...] = zm[:, RWKV_COLS + 2 * WIDTH:].astype(BF16)
    zi = _mm(hh, wih_ref[...]) + _mm(hh, wil_ref[...]) + _mm(hl, wih_ref[...])
    qi_ref[...] = zi[:, :WIDTH]
    kiw_ref[...] = zi[:, WIDTH:]


def _in_proj(x2, g, wm, wih, wil, tm=256):
    m, d = x2.shape
    row = lambda n: pl.BlockSpec((tm, n), lambda i: (i, 0))
    full = lambda a: pl.BlockSpec(a.shape, lambda i: (0, 0))
    return pl.pallas_call(
        _in_proj_kernel,
        grid=(m // tm,),
        in_specs=[row(d), full(g), full(wm), full(wih), full(wil)],
        out_specs=[row(RWKV_COLS), row(WIDTH), row(WIDTH), row(WIDTH), row(WIDTH), row(LANES)],
        out_shape=[
            jax.ShapeDtypeStruct((m, RWKV_COLS), F32),
            jax.ShapeDtypeStruct((m, WIDTH), BF16),
            jax.ShapeDtypeStruct((m, WIDTH), BF16),
            jax.ShapeDtypeStruct((m, WIDTH), BF16),
            jax.ShapeDtypeStruct((m, WIDTH), F32),
            jax.ShapeDtypeStruct((m, LANES), F32),
        ],
        compiler_params=pltpu.CompilerParams(dimension_semantics=("arbitrary",), vmem_limit_bytes=VMEM_LIMIT),
        name="in_proj",
    )(x2, g, wm, wih, wil)


def _t5_bucket(rel):
    half = REL_BUCKETS // 2
    max_exact = half // 2
    ret = jnp.where(rel > 0, half, 0)
    n = jnp.abs(rel)
    nf = jnp.maximum(n, 1).astype(F32)
    large = max_exact + (jnp.log(nf / max_exact) / math.log(REL_MAX_DISTANCE / max_exact)
                         * (half - max_exact)).astype(I32)
    large = jnp.minimum(large, half - 1)
    return ret + jnp.where(n < max_exact, n, large)


def _bias_kernel(tab_ref, band_ref, far_ref):
    r = lax.broadcasted_iota(I32, (CHUNK, KEY_TILE), 0)
    c = lax.broadcasted_iota(I32, (CHUNK, KEY_TILE), 1)
    near_bucket = _t5_bucket(c - FRONT_PAD - r)
    far_bucket = _t5_bucket(jnp.full((CHUNK, LANES), -(FRONT_PAD + 1), I32))
    for h in range(HEADS):
        near = jnp.zeros((CHUNK, KEY_TILE), F32)
        far = jnp.zeros((CHUNK, LANES), F32)
        for j in range(REL_BUCKETS):
            t = tab_ref[j, h]
            near = jnp.where(near_bucket == j, t, near)
            far = jnp.where(far_bucket == j, t, far)
        band_ref[h * CHUNK:(h + 1) * CHUNK, :] = near
        far_ref[h * CHUNK:(h + 1) * CHUNK, :] = far


def _bias_tiles(rel_bias):
    return pl.pallas_call(
        _bias_kernel,
        in_specs=[pl.BlockSpec(memory_space=pltpu.SMEM)],
        out_shape=[
            jax.ShapeDtypeStruct((HEADS * CHUNK, KEY_TILE), F32),
            jax.ShapeDtypeStruct((HEADS * CHUNK, LANES), F32),
        ],
        name="bias_tiles",
    )(rel_bias)


def _dot3(a, b):
    ah, al = _split2(a)
    bh, bl = _split2(b)
    lhs = jnp.concatenate([ah, ah, al], axis=1)
    rhs = jnp.concatenate([bh, bl, bh], axis=0)
    return _mm(lhs, rhs)


def _inv_unit_lower(low):
    n = low.shape[0]
    eye = (lax.broadcasted_iota(I32, (n, n), 0) == lax.broadcasted_iota(I32, (n, n), 1)).astype(F32)
    t = eye + low
    p = low
    steps = int(math.log2(n)) - 1
    for _ in range(steps):
        p = _dot3(p, p)
        t = t + _dot3(t, p)
    return t


def _rwkv_kernel(z_ref, mu_ref, w0_ref, wl_ref, a0_ref, g2_ref, kk_ref, ka_ref, rk_ref, lng_ref, lnb_ref,
                 bd_ref, y_ref, prev_s, st_s, oh_s):
    ci = pl.program_id(1)

    @pl.when(ci == 0)
    def _():
        prev_s[...] = jnp.zeros_like(prev_s)
        st_s[...] = jnp.zeros_like(st_s)

    z = z_ref[0]
    row = lax.broadcasted_iota(I32, z.shape, 0)
    zp = jnp.where(row == 0, prev_s[...], pltpu.roll(z, 1, axis=0))
    prev_s[...] = z[CHUNK - 1:CHUNK, :]
    zs = z + (zp - z) * mu_ref[...]
    r = zs[:, 0:WIDTH]
    k = zs[:, WIDTH:2 * WIDTH]
    v = zs[:, 2 * WIDTH:3 * WIDTH]
    zl = zs[:, 3 * WIDTH:3 * WIDTH + LANES]
    zg = zs[:, 3 * WIDTH + LANES:]

    lane = lax.broadcasted_iota(I32, zl.shape, 1)
    zl = jnp.where(lane < DECAY_RANK, jnp.tanh(zl), zl)
    la = _mm(zl.astype(BF16), wl_ref[...])
    y0 = -(w0_ref[...] + la[:, :WIDTH])
    softplus = jnp.maximum(y0, 0.0) + jnp.log(1.0 + jnp.exp(-jnp.abs(y0)))
    ld = -jnp.exp(-softplus - 0.5)
    a = _sigmoid(a0_ref[...] + la[:, WIDTH:])
    g = _mm(_sigmoid(zg).astype(BF16), g2_ref[...])

    bd = bd_ref[...]

    def seg_sum(x):
        xh, xl = _split2(x)
        return _mm(xh, bd) + _mm(xl, bd)

    kk = k * kk_ref[...]
    kk = kk / jnp.maximum(jnp.sqrt(seg_sum(kk * kk)), 1e-12)
    kmod = k * (1.0 + (a - 1.0) * ka_ref[...])
    bvec = kk * a

    ti = lax.broadcasted_iota(I32, (CHUNK, CHUNK), 0)
    tj = lax.broadcasted_iota(I32, (CHUNK, CHUNK), 1)
    tri = (ti >= tj).astype(BF16)
    l1, l2, l3 = _split3(ld)
    cum = _mm(tri, l1) + _mm(tri, l2) + _mm(tri, l3)
    cum_x = cum - ld
    cref = cum[CHUNK // 2 - 1:CHUNK // 2, :]
    clast = cum[CHUNK - 1:CHUNK, :]
    e_neg = jnp.exp(cref - cum)
    rt = r * jnp.exp(cum - cref)
    at = -kk * jnp.exp(cum_x - cref)
    kt = kmod * e_neg
    bt = bvec * e_neg
    r0 = r * jnp.exp(cum)
    a0v = -kk * jnp.exp(cum_x)
    e_end = jnp.exp(clast - cum)
    kc = kmod * e_end
    bc = bvec * e_end
    pc = jnp.exp(clast)

    tok = lax.broadcasted_iota(I32, (CHUNK, 2 * CHUNK), 0)
    src = lax.broadcasted_iota(I32, (CHUNK, 2 * CHUNK), 1) & (CHUNK - 1)
    strict2 = tok > src
    incl2 = tok >= src

    for h in range(HEADS):
        sl = slice(h * HEAD_DIM, (h + 1) * HEAD_DIM)
        la_ = jnp.concatenate([at[:, sl], rt[:, sl]], axis=0).astype(BF16)
        rb_ = jnp.concatenate([bt[:, sl], kt[:, sl]], axis=0).astype(BF16)
        aa = _nt(la_, rb_)
        top = jnp.where(strict2, aa[:CHUNK], 0.0)
        arr = jnp.where(incl2, aa[CHUNK:], 0.0).astype(BF16)
        t = _inv_unit_lower(top[:, :CHUNK])
        s_h = st_s[h]
        s_b = s_h.astype(BF16)
        v_b = v[:, sl].astype(BF16)
        x0 = _nt(a0v[:, sl].astype(BF16), s_b)
        yk = _mm(top[:, CHUNK:].astype(BF16), v_b)
        u = _mm(t.astype(BF16), (x0 + yk).astype(BF16))
        uv = jnp.concatenate([u.astype(BF16), v_b], axis=0)
        o_h = _nt(r0[:, sl].astype(BF16), s_b) + _mm(arr, uv)
        bk = jnp.concatenate([bc[:, sl], kc[:, sl]], axis=0).astype(BF16)
        st_s[h] = s_h * pc[:, sl] + _tn(uv, bk)
        oh_s[:, sl] = o_h

    o = oh_s[...]
    inv_n = 1.0 / HEAD_DIM
    mean = seg_sum(o) * inv_n
    d = o - mean
    var = seg_sum(d * d) * inv_n
    on = d * lax.rsqrt(var + GN_EPS) * lng_ref[...] + lnb_ref[...]
    bonus = seg_sum(r * kmod * rk_ref[...]) * v
    y_ref[0] = ((on + bonus) * g).astype(BF16)


def _rwkv(zr, mu, w0, wl, a0, g2, k_k, k_a, r_k, ln_g, ln_b, bd):
    b, s, _ = zr.shape
    full = lambda a: pl.BlockSpec(a.shape, lambda i, j: (0,) * a.ndim)
    params = (mu, w0, wl, a0, g2, k_k, k_a, r_k, ln_g, ln_b, bd)
    return pl.pallas_call(
        _rwkv_kernel,
        grid=(b, s // CHUNK),
        in_specs=[pl.BlockSpec((1, CHUNK, RWKV_COLS), lambda i, j: (i, j, 0))] + [full(a) for a in params],
        out_specs=pl.BlockSpec((1, CHUNK, WIDTH), lambda i, j: (i, j, 0)),
        out_shape=jax.ShapeDtypeStruct((b, s, WIDTH), BF16),
        scratch_shapes=[
            pltpu.VMEM((1, RWKV_COLS), F32),
            pltpu.VMEM((HEADS, HEAD_DIM, HEAD_DIM), F32),
            pltpu.VMEM((CHUNK, WIDTH), F32),
        ],
        compiler_params=pltpu.CompilerParams(dimension_semantics=("arbitrary", "arbitrary"),
                                             vmem_limit_bytes=VMEM_LIMIT),
        name="rwkv",
    )(zr, *params)


def _sortable(x):
    x = jnp.where(x == 0.0, 0.0, x)
    b = lax.bitcast_convert_type(x, I32)
    return b ^ ((b >> 31) & INT_MAX)


def _dsa_kernel(qi_ref, q_ref, kiw_ref, k_ref, v_ref, band_ref, far_ref, lng_ref, lnb_ref, o_ref,
                ki3_s, kfar_s, knear_s, thr_s, jlim_s, acc_s, *, topk):
    i = pl.program_id(1)
    sp = kiw_ref.shape[1]
    pos_bits = int(sp).bit_length()
    n_far = (i + 1) // 4
    far_lim = CHUNK * i - FRONT_PAD
    lane = lax.broadcasted_iota(I32, (CHUNK, KEY_TILE), 1)

    @pl.when(i == 0)
    def _():
        rows = 320 if sp % 320 == 0 else CHUNK
        def body(c, carry):
            r0 = pl.multiple_of(c * rows, 64)
            ki = kiw_ref[0, pl.ds(r0, rows), :][:, :IDX_DIM]
            mean = jnp.mean(ki, axis=-1, keepdims=True)
            var = jnp.mean(jnp.square(ki - mean), axis=-1, keepdims=True)
            kn = (ki - mean) * lax.rsqrt(var + LN_EPS) * lng_ref[...] + lnb_ref[...]
            hi, lo = _split2(kn)
            ki3_s[pl.ds(r0, rows), :] = jnp.concatenate([hi, lo, hi], axis=1)
            return carry
        lax.fori_loop(0, sp // rows, body, 0)

    qi = qi_ref[0].reshape(IDX_HEADS * CHUNK, IDX_DIM)
    qh, ql = _split2(qi)
    lhs_i = jnp.concatenate([qh, qh, ql], axis=1)
    wq = kiw_ref[0, pl.ds(pl.multiple_of(FRONT_PAD + CHUNK * i, CHUNK), CHUNK), :]
    wscale = IDX_HEADS ** -0.5 * IDX_DIM ** -0.5
    wcol = jnp.concatenate([wq[:, IDX_DIM + h:IDX_DIM + h + 1] for h in range(IDX_HEADS)], axis=0) * wscale

    def score_tile(r0):
        x = _nt(lhs_i, ki3_s[pl.ds(r0, KEY_TILE), :])
        act = jnp.maximum(x, 0.0) * wcol
        sc = act[0:CHUNK]
        for h in range(1, IDX_HEADS):
            sc = sc + act[h * CHUNK:(h + 1) * CHUNK]
        return sc

    def far_valid(c):
        return (c * KEY_TILE + lane) < far_lim

    near_valid = ((far_lim + lane) >= 0) & (lane < NEAR_VALID)

    def far_score(c, carry):
        sc = score_tile(pl.multiple_of(FRONT_PAD + c * KEY_TILE, LANES))
        kfar_s[c] = _sortable(jnp.where(far_valid(c), sc, -jnp.inf))
        return carry

    lax.fori_loop(0, n_far, far_score, 0)
    sc = score_tile(pl.multiple_of(CHUNK * i, CHUNK))
    knear_s[...] = _sortable(jnp.where(near_valid, sc, -jnp.inf))

    thr_s[...] = jnp.full(thr_s.shape, INT_MIN, I32)
    jlim_s[...] = jnp.full(jlim_s.shape, INT_MAX, I32)
    kf = float(topk)

    def count(pred_far, pred_near):
        def body(c, acc):
            return acc + jnp.where(pred_far(c, kfar_s[c]), 1.0, 0.0)
        acc = lax.fori_loop(0, n_far, body, jnp.where(pred_near(knear_s[...]), 1.0, 0.0))
        return jnp.sum(acc, axis=1, keepdims=True)

    def count_ge(cand):
        return count(lambda c, key: key >= cand, lambda key: key >= cand)

    @pl.when(CHUNK * (i + 1) > topk)
    def _():
        ans = jnp.where(count_ge(jnp.zeros((CHUNK, 1), I32)) >= kf, 0, INT_MIN).astype(I32)

        def bit_body(t, ans):
            cand = ans | lax.shift_left(jnp.int32(1), 30 - t)
            return jnp.where(count_ge(cand) >= kf, cand, ans)

        ans = lax.fori_loop(0, 31, bit_body, ans)
        thr_s[...] = jnp.broadcast_to(ans, thr_s.shape)
        n_ge = count_ge(ans)

        @pl.when(jnp.max(n_ge) > kf)
        def _():
            need = kf - count_ge(ans + 1)

            def count_eq_before(x):
                return count(lambda c, key: (key == ans) & ((c * KEY_TILE + lane) < x),
                             lambda key: (key == ans) & ((far_lim + lane) < x))

            def pos_body(t, x):
                cand = x | lax.shift_left(jnp.int32(1), pos_bits - 1 - t)
                return jnp.where(count_eq_before(cand) < need, cand, x)

            x = lax.fori_loop(0, pos_bits, pos_body, jnp.zeros((CHUNK, 1), I32))
            jlim_s[...] = jnp.broadcast_to(x, jlim_s.shape)

    thr = thr_s[...][:, 0:1]
    jlim = jlim_s[...][:, 0:1]

    def selected(key, pos, valid):
        return valid & ((key > thr) | ((key == thr) & (pos <= jlim)))

    qs = q_ref[0] * jnp.asarray(HEAD_DIM ** -0.5, BF16)
    lane_head = lax.broadcasted_iota(I32, (CHUNK, WIDTH), 1) // HEAD_DIM
    qm = jnp.concatenate([jnp.where(lane_head == h, qs, jnp.zeros_like(qs)) for h in range(HEADS)], axis=0)
    acc_s[...] = jnp.zeros_like(acc_s)

    def attend(r0, bias, sel, m, l):
        kt = k_ref[0, pl.ds(r0, KEY_TILE), :]
        vt = v_ref[0, pl.ds(r0, KEY_TILE), :]
        s = _nt(qm, kt) + bias
        m_new, p_all = [], []
        for h in range(HEADS):
            sh = jnp.where(sel, s[h * CHUNK:(h + 1) * CHUNK], NEG_INF)
            mh = jnp.maximum(m[h * CHUNK:(h + 1) * CHUNK], jnp.max(sh, axis=1, keepdims=True))
            p_all.append(jnp.where(sel, jnp.exp(sh - mh), 0.0))
            m_new.append(mh)
        m_new = jnp.concatenate(m_new, axis=0)
        p = jnp.concatenate(p_all, axis=0)
        alpha = jnp.exp(m - m_new)
        l_new = alpha * l + jnp.sum(p, axis=1, keepdims=True)
        acc_s[...] = alpha * acc_s[...] + _mm(p.astype(BF16), vt)
        return m_new, l_new

    def far_attend(c, carry):
        m, l = carry
        sel = selected(kfar_s[c], c * KEY_TILE + lane, far_valid(c))
        return attend(pl.multiple_of(FRONT_PAD + c * KEY_TILE, LANES), far_ref[...][:, 0:1], sel, m, l)

    m0 = jnp.full((HEADS * CHUNK, 1), NEG_INF, F32)
    l0 = jnp.zeros((HEADS * CHUNK, 1), F32)
    m, l = lax.fori_loop(0, n_far, far_attend, (m0, l0))
    sel = selected(knear_s[...], far_lim + lane, near_valid)
    m, l = attend(pl.multiple_of(CHUNK * i, CHUNK), band_ref[...], sel, m, l)

    inv = 1.0 / l
    out = jnp.zeros((CHUNK, WIDTH), F32)
    for h in range(HEADS):
        rows = slice(h * CHUNK, (h + 1) * CHUNK)
        out = out + jnp.where(lane_head == h, acc_s[rows, :] * inv[rows], 0.0)
    o_ref[0] = out.astype(BF16)


def _dsa(qi_t, q, kiw_p, k_p, v_p, band, far, ln_g, ln_b):
    b, s, _ = q.shape
    sp = k_p.shape[1]
    topk = min(MAX_TOPK, s // 4)
    n_far_max = (s // CHUNK) // 4
    full = lambda a: pl.BlockSpec(a.shape, lambda i, j: (0,) * a.ndim)
    per_batch = lambda n: pl.BlockSpec((1, sp, n), lambda i, j: (i, 0, 0))
    return pl.pallas_call(
        functools.partial(_dsa_kernel, topk=topk),
        grid=(b, s // CHUNK),
        in_specs=[
            pl.BlockSpec((1, IDX_HEADS, CHUNK, IDX_DIM), lambda i, j: (i, 0, j, 0)),
            pl.BlockSpec((1, CHUNK, WIDTH), lambda i, j: (i, j, 0)),
            per_batch(LANES), per_batch(WIDTH), per_batch(WIDTH),
            full(band), full(far), full(ln_g), full(ln_b),
        ],
        out_specs=pl.BlockSpec((1, CHUNK, WIDTH), lambda i, j: (i, j, 0)),
        out_shape=jax.ShapeDtypeStruct((b, s, WIDTH), BF16),
        scratch_shapes=[
            pltpu.VMEM((sp, 3 * IDX_DIM), BF16),
            pltpu.VMEM((max(n_far_max, 1), CHUNK, KEY_TILE), I32),
            pltpu.VMEM((CHUNK, KEY_TILE), I32),
            pltpu.VMEM((CHUNK, LANES), I32),
            pltpu.VMEM((CHUNK, LANES), I32),
            pltpu.VMEM((HEADS * CHUNK, WIDTH), F32),
        ],
        compiler_params=pltpu.CompilerParams(dimension_semantics=("arbitrary", "arbitrary"),
                                             vmem_limit_bytes=VMEM_LIMIT),
        name="dsa",
    )(qi_t, q, kiw_p, k_p, v_p, band, far, ln_g, ln_b)


def _post_kernel(y1_ref, y2_ref, x_ref, p_ref, woa_ref, wob_ref, w1_ref, w2_ref, wp_ref, wg_ref,
                 g_mix_ref, g_f1_ref, g_f2_ref, g_p1_ref, g_p2_ref, o_ref):
    mix = _mm(y1_ref[...], woa_ref[...]) + _mm(y2_ref[...], wob_ref[...])
    x1 = x_ref[...] + _rms(mix, g_mix_ref[...])
    h = _rms(x1, g_f1_ref[...]).astype(BF16)
    hid = jnp.square(jnp.maximum(_mm(h, w1_ref[...]), 0.0)).astype(BF16)
    x2 = x1 + _rms(_mm(hid, w2_ref[...]), g_f2_ref[...])
    gate = _sigmoid(_mm(_rms(x2, g_p1_ref[...]).astype(BF16), wg_ref[...]))
    pe = _mm(p_ref[...].astype(BF16), wp_ref[...]) * gate
    o_ref[...] = x2 + _rms(pe, g_p2_ref[...])


def _post(y1, y2, x2d, p2d, woa, wob, w1, w2, wp, wg, gains, tm=256):
    m, d = x2d.shape
    row = lambda n: pl.BlockSpec((tm, n), lambda i: (i, 0))
    const = lambda a: pl.BlockSpec(a.shape, lambda i: (0, 0), pipeline_mode=pl.Buffered(1))
    weights = (woa, wob, w1, w2, wp, wg) + tuple(gains)
    return pl.pallas_call(
        _post_kernel,
        grid=(m // tm,),
        in_specs=[row(WIDTH), row(WIDTH), row(d), row(p2d.shape[1])] + [const(a) for a in weights],
        out_specs=row(d),
        out_shape=jax.ShapeDtypeStruct((m, d), F32),
        compiler_params=pltpu.CompilerParams(dimension_semantics=("arbitrary",), vmem_limit_bytes=VMEM_LIMIT),
        name="post",
    )(y1, y2, x2d, p2d, *weights)


def _layer(x, p, w_in, shift_mu, w0, w2r, a0, a2, g2, k_k, k_a, r_k, ln_g, ln_b, idx_g, idx_b, rel_bias,
           w_out, w_f1, w_f2, w_ple, w_gate, n_mix_pre, n_mix_post, n_ffn_pre, n_ffn_post, n_ple_pre, n_ple_post):
    b, s, d = x.shape
    m = b * s
    row = lambda a: a.reshape(1, -1).astype(F32)

    c = [0, WIDTH, WIDTH + DECAY_RANK, 2 * WIDTH + DECAY_RANK, 3 * WIDTH + DECAY_RANK,
         3 * WIDTH + DECAY_RANK + A_RANK, RWKV_COLS]
    perm = jnp.concatenate([jnp.arange(c[0], c[1]), jnp.arange(c[2], c[3]), jnp.arange(c[3], c[4]),
                            jnp.arange(c[1], c[2]), jnp.arange(c[4], c[5]), jnp.arange(c[5], c[6])])
    w_main = jnp.concatenate([w_in[:, :RWKV_COLS][:, perm], w_in[:, RWKV_COLS:RWKV_COLS + 3 * WIDTH]],
                             axis=1).astype(BF16)
    w_idx = w_in[:, RWKV_COLS + 3 * WIDTH:]
    w_idx = jnp.pad(w_idx, ((0, 0), (0, N_IDX - w_idx.shape[1])))
    w_idx_hi = w_idx.astype(BF16)
    w_idx_lo = (w_idx - w_idx_hi.astype(F32)).astype(BF16)

    zr, q, k, v, qi, kiw = _in_proj(x.reshape(m, d), row(n_mix_pre), w_main, w_idx_hi, w_idx_lo)

    zeros = jnp.zeros((DECAY_RANK, WIDTH), F32)
    w_lora = jnp.concatenate([jnp.concatenate([w2r, zeros], axis=1),
                              jnp.concatenate([zeros, a2], axis=1)], axis=0).astype(BF16)
    head_of = jnp.arange(WIDTH) // HEAD_DIM
    block_ones = (head_of[:, None] == head_of[None, :]).astype(BF16)
    y_rwkv = _rwkv(zr.reshape(b, s, RWKV_COLS), row(shift_mu[perm]), row(w0), w_lora, row(a0), g2.astype(BF16),
                   row(k_k), row(k_a), row(r_k), row(ln_g), row(ln_b), block_ones)

    band, far = _bias_tiles(rel_bias.astype(F32))
    pad = lambda a: jnp.pad(a.reshape(b, s, -1), ((0, 0), (FRONT_PAD, BACK_PAD), (0, 0)))
    qi_t = qi.reshape(b, s, IDX_HEADS, IDX_DIM).transpose(0, 2, 1, 3)
    y_dsa = _dsa(qi_t, q.reshape(b, s, WIDTH), pad(kiw), pad(k), pad(v), band, far, row(idx_g), row(idx_b))

    gains = tuple(row(a) for a in (n_mix_post, n_ffn_pre, n_ffn_post, n_ple_pre, n_ple_post))
    out = _post(y_rwkv.reshape(m, WIDTH), y_dsa.reshape(m, WIDTH), x.reshape(m, d), p.reshape(m, -1),
                w_out[:WIDTH].astype(BF16), w_out[WIDTH:].astype(BF16), w_f1.astype(BF16), w_f2.astype(BF16),
                w_ple.astype(BF16), w_gate.astype(BF16), gains)
    return out.reshape(b, s, d)


def kernel(x, p, w_in, shift_mu, rwkv_w0, rwkv_w2, rwkv_a0, rwkv_a2, rwkv_g2, rwkv_k_k, rwkv_k_a, rwkv_r_k, rwkv_ln_g, rwkv_ln_b, idx_k_ln_g, idx_k_ln_b, rel_bias, w_out, w_ffn1, w_ffn2, w_ple, w_ple_gate, norm_mix_pre, norm_mix_post, norm_ffn_pre, norm_ffn_post, norm_ple_pre, norm_ple_post):
    for i in range(p.shape[0]):
        x = _layer(x, p[i], w_in[i], shift_mu[i], rwkv_w0[i], rwkv_w2[i], rwkv_a0[i], rwkv_a2[i], rwkv_g2[i],
                   rwkv_k_k[i], rwkv_k_a[i], rwkv_r_k[i], rwkv_ln_g[i], rwkv_ln_b[i], idx_k_ln_g[i],
                   idx_k_ln_b[i], rel_bias, w_out[i], w_ffn1[i], w_ffn2[i], w_ple[i], w_ple_gate[i],
                   norm_mix_pre[i], norm_mix_post[i], norm_ffn_pre[i], norm_ffn_post[i], norm_ple_pre[i],
                   norm_ple_post[i])
    return x
```

```python
import functools
import math

import jax
import jax.numpy as jnp
from jax import lax
from jax.experimental import pallas as pl
from jax.experimental.pallas import tpu as pltpu

F32 = jnp.float32
BF16 = jnp.bfloat16
I32 = jnp.int32

CHUNK = 64
HEADS = 8
HEAD_DIM = 64
WIDTH = HEADS * HEAD_DIM
DECAY_RANK = 64
A_RANK = 64
GATE_RANK = 128
GN_EPS = 64e-5
IDX_HEADS = 8
IDX_DIM = 64
MAX_TOPK = 256
REL_BUCKETS = 32
REL_MAX_DISTANCE = 128
NORM_EPS = 1e-6
LN_EPS = 1e-6
NEG_INF = -1e30
RWKV_COLS = 1792
N_MAIN = RWKV_COLS + 3 * WIDTH
N_IDX = 640

LANES = 128
KEY_TILE = 256
FRONT_PAD = 2 * CHUNK
BACK_PAD = CHUNK
NEAR_VALID = 3 * CHUNK
VMEM_LIMIT = 56 * 1024 * 1024

INT_MIN = -(2**31)
INT_MAX = 2**31 - 1


def _nt(a, b):
    return lax.dot_general(a, b, (((1,), (1,)), ((), ())), preferred_element_type=F32)


def _tn(a, b):
    return lax.dot_general(a, b, (((0,), (0,)), ((), ())), preferred_element_type=F32)


def _mm(a, b):
    return jnp.dot(a, b, preferred_element_type=F32)


def _split2(x):
    hi = x.astype(BF16)
    lo = (x - hi.astype(F32)).astype(BF16)
    return hi, lo


def _split3(x):
    hi = x.astype(BF16)
    r = x - hi.astype(F32)
    mid = r.astype(BF16)
    lo = (r - mid.astype(F32)).astype(BF16)
    return hi, mid, lo


def _rms(x, g):
    return x * lax.rsqrt(jnp.mean(x * x, axis=-1, keepdims=True) + NORM_EPS) * g


def _sigmoid(x):
    return 1.0 / (1.0 + jnp.exp(-x))


def _in_proj_kernel(x_ref, g_ref, wm_ref, wih_ref, wil_ref, zr_ref, q_ref, k_ref, v_ref, qi_ref, kiw_ref):
    h = _rms(x_ref[...], g_ref[...])
    hh, hl = _split2(h)
    zm = _mm(hh, wm_ref[...])
    zr_ref[...] = zm[:, :RWKV_COLS]
    q_ref[...] = zm[:, RWKV_COLS:RWKV_COLS + WIDTH].astype(BF16)
    k_ref[...] = zm[:, RWKV_COLS + WIDTH:RWKV_COLS + 2 * WIDTH].astype(BF16)
    v_ref[...] = zm[:, RWKV_COLS + 2 * WIDTH:].astype(BF16)
    zi = _mm(hh, wih_ref[...]) + _mm(hh, wil_ref[...]) + _mm(hl, wih_ref[...])
    qi_ref[...] = zi[:, :WIDTH]
    kiw_ref[...] = zi[:, WIDTH:]


def _in_proj(x2, g, wm, wih, wil, tm=256):
    m, d = x2.shape
    row = lambda n: pl.BlockSpec((tm, n), lambda i: (i, 0))
    full = lambda a: pl.BlockSpec(a.shape, lambda i: (0, 0))
    return pl.pallas_call(
        _in_proj_kernel,
        grid=(m // tm,),
        in_specs=[row(d), full(g), full(wm), full(wih), full(wil)],
        out_specs=[row(RWKV_COLS), row(WIDTH), row(WIDTH), row(WIDTH), row(WIDTH), row(LANES)],
        out_shape=[
            jax.ShapeDtypeStruct((m, RWKV_COLS), F32),
            jax.ShapeDtypeStruct((m, WIDTH), BF16),
            jax.ShapeDtypeStruct((m, WIDTH), BF16),
            jax.ShapeDtypeStruct((m, WIDTH), BF16),
            jax.ShapeDtypeStruct((m, WIDTH), F32),
            jax.ShapeDtypeStruct((m, LANES), F32),
        ],
        compiler_params=pltpu.CompilerParams(dimension_semantics=("arbitrary",), vmem_limit_bytes=VMEM_LIMIT),
        name="in_proj",
    )(x2, g, wm, wih, wil)


def _t5_bucket(rel):
    half = REL_BUCKETS // 2
    max_exact = half // 2
    ret = jnp.where(rel > 0, half, 0)
    n = jnp.abs(rel)
    nf = jnp.maximum(n, 1).astype(F32)
    large = max_exact + (jnp.log(nf / max_exact) / math.log(REL_MAX_DISTANCE / max_exact)
                         * (half - max_exact)).astype(I32)
    large = jnp.minimum(large, half - 1)
    return ret + jnp.where(n < max_exact, n, large)


def _bias_kernel(tab_ref, band_ref, far_ref):
    r = lax.broadcasted_iota(I32, (CHUNK, KEY_TILE), 0)
    c = lax.broadcasted_iota(I32, (CHUNK, KEY_TILE), 1)
    near_bucket = _t5_bucket(c - FRONT_PAD - r)
    far_bucket = _t5_bucket(jnp.full((CHUNK, LANES), -(FRONT_PAD + 1), I32))
    for h in range(HEADS):
        near = jnp.zeros((CHUNK, KEY_TILE), F32)
        far = jnp.zeros((CHUNK, LANES), F32)
        for j in range(REL_BUCKETS):
            t = tab_ref[j, h]
            near = jnp.where(near_bucket == j, t, near)
            far = jnp.where(far_bucket == j, t, far)
        band_ref[h * CHUNK:(h + 1) * CHUNK, :] = near
        far_ref[h * CHUNK:(h + 1) * CHUNK, :] = far


def _bias_tiles(rel_bias):
    return pl.pallas_call(
        _bias_kernel,
        in_specs=[pl.BlockSpec(memory_space=pltpu.SMEM)],
        out_shape=[
            jax.ShapeDtypeStruct((HEADS * CHUNK, KEY_TILE), F32),
            jax.ShapeDtypeStruct((HEADS * CHUNK, LANES), F32),
        ],
        name="bias_tiles",
    )(rel_bias)


def _dot3(a, b):
    ah, al = _split2(a)
    bh, bl = _split2(b)
    lhs = jnp.concatenate([ah, ah, al], axis=1)
    rhs = jnp.concatenate([bh, bl, bh], axis=0)
    return _mm(lhs, rhs)


def _inv_unit_lower(lows):
    n = lows[0].shape[0]
    eye = (lax.broadcasted_iota(I32, (n, n), 0) == lax.broadcasted_iota(I32, (n, n), 1)).astype(F32)
    ts = [eye + low for low in lows]
    ps = list(lows)
    for _ in range(int(math.log2(n)) - 1):
        ps = [_dot3(p, p) for p in ps]
        ts = [t + _dot3(t, p) for t, p in zip(ts, ps)]
    return ts


def _rwkv_kernel(z_ref, mu_ref, w0_ref, wl_ref, a0_ref, g2_ref, kk_ref, ka_ref, rk_ref, lng_ref, lnb_ref,
                 bd_ref, y_ref, prev_s, st_s, oh_s):
    ci = pl.program_id(1)

    @pl.when(ci == 0)
    def _():
        prev_s[...] = jnp.zeros_like(prev_s)
        st_s[...] = jnp.zeros_like(st_s)

    z = z_ref[0]
    row = lax.broadcasted_iota(I32, z.shape, 0)
    zp = jnp.where(row == 0, prev_s[...], pltpu.roll(z, 1, axis=0))
    prev_s[...] = z[CHUNK - 1:CHUNK, :]
    zs = z + (zp - z) * mu_ref[...]
    r = zs[:, 0:WIDTH]
    k = zs[:, WIDTH:2 * WIDTH]
    v = zs[:, 2 * WIDTH:3 * WIDTH]
    zl = zs[:, 3 * WIDTH:3 * WIDTH + LANES]
    zg = zs[:, 3 * WIDTH + LANES:]

    lane = lax.broadcasted_iota(I32, zl.shape, 1)
    zl = jnp.where(lane < DECAY_RANK, jnp.tanh(zl), zl)
    la = _mm(zl.astype(BF16), wl_ref[...])
    y0 = -(w0_ref[...] + la[:, :WIDTH])
    softplus = jnp.maximum(y0, 0.0) + jnp.log(1.0 + jnp.exp(-jnp.abs(y0)))
    ld = -jnp.exp(-softplus - 0.5)
    a = _sigmoid(a0_ref[...] + la[:, WIDTH:])
    g = _mm(_sigmoid(zg).astype(BF16), g2_ref[...])

    bd = bd_ref[...]

    def seg_sum(x):
        xh, xl = _split2(x)
        return _mm(xh, bd) + _mm(xl, bd)

    kk = k * kk_ref[...]
    kk = kk / jnp.maximum(jnp.sqrt(seg_sum(kk * kk)), 1e-12)
    kmod = k * (1.0 + (a - 1.0) * ka_ref[...])
    bvec = kk * a

    ti = lax.broadcasted_iota(I32, (CHUNK, CHUNK), 0)
    tj = lax.broadcasted_iota(I32, (CHUNK, CHUNK), 1)
    tri = (ti >= tj).astype(BF16)
    l1, l2, l3 = _split3(ld)
    cum = _mm(tri, l1) + _mm(tri, l2) + _mm(tri, l3)
    cum_x = cum - ld
    cref = cum[CHUNK // 2 - 1:CHUNK // 2, :]
    clast = cum[CHUNK - 1:CHUNK, :]
    e_neg = jnp.exp(cref - cum)
    rt = r * jnp.exp(cum - cref)
    at = -kk * jnp.exp(cum_x - cref)
    kt = kmod * e_neg
    bt = bvec * e_neg
    r0 = r * jnp.exp(cum)
    a0v = -kk * jnp.exp(cum_x)
    e_end = jnp.exp(clast - cum)
    kc = kmod * e_end
    bc = bvec * e_end
    pc = jnp.exp(clast)

    tok = lax.broadcasted_iota(I32, (CHUNK, 2 * CHUNK), 0)
    src = lax.broadcasted_iota(I32, (CHUNK, 2 * CHUNK), 1) & (CHUNK - 1)
    strict2 = tok > src
    incl2 = tok >= src

    hs = range(HEADS)
    sls = [slice(h * HEAD_DIM, (h + 1) * HEAD_DIM) for h in hs]
    stack = lambda x, y, sl: jnp.concatenate([x[:, sl], y[:, sl]], axis=0).astype(BF16)
    aa = [_nt(stack(at, rt, sl), stack(bt, kt, sl)) for sl in sls]
    top = [jnp.where(strict2, x[:CHUNK], 0.0) for x in aa]
    arr = [jnp.where(incl2, x[CHUNK:], 0.0).astype(BF16) for x in aa]
    s_f = [st_s[h] for h in hs]
    s_b = [s.astype(BF16) for s in s_f]
    v_b = [v[:, sl].astype(BF16) for sl in sls]
    x0 = [_nt(a0v[:, sl].astype(BF16), s) for sl, s in zip(sls, s_b)]
    yk = [_mm(tp[:, CHUNK:].astype(BF16), vb) for tp, vb in zip(top, v_b)]
    o_st = [_nt(r0[:, sl].astype(BF16), s) for sl, s in zip(sls, s_b)]
    t = _inv_unit_lower([tp[:, :CHUNK] for tp in top])
    u = [_mm(th.astype(BF16), (a_ + b_).astype(BF16)) for th, a_, b_ in zip(t, x0, yk)]
    uv = [jnp.concatenate([uh.astype(BF16), vb], axis=0) for uh, vb in zip(u, v_b)]
    o_h = [a_ + _mm(ar, w) for a_, ar, w in zip(o_st, arr, uv)]
    s_new = [s * pc[:, sl] + _tn(w, stack(bc, kc, sl)) for s, sl, w in zip(s_f, sls, uv)]
    for h in hs:
        st_s[h] = s_new[h]
        oh_s[:, sls[h]] = o_h[h]

    o = oh_s[...]
    inv_n = 1.0 / HEAD_DIM
    mean = seg_sum(o) * inv_n
    d = o - mean
    var = seg_sum(d * d) * inv_n
    on = d * lax.rsqrt(var + GN_EPS) * lng_ref[...] + lnb_ref[...]
    bonus = seg_sum(r * kmod * rk_ref[...]) * v
    y_ref[0] = ((on + bonus) * g).astype(BF16)


def _rwkv(zr, mu, w0, wl, a0, g2, k_k, k_a, r_k, ln_g, ln_b, bd):
    b, s, _ = zr.shape
    full = lambda a: pl.BlockSpec(a.shape, lambda i, j: (0,) * a.ndim)
    params = (mu, w0, wl, a0, g2, k_k, k_a, r_k, ln_g, ln_b, bd)
    return pl.pallas_call(
        _rwkv_kernel,
        grid=(b, s // CHUNK),
        in_specs=[pl.BlockSpec((1, CHUNK, RWKV_COLS), lambda i, j: (i, j, 0))] + [full(a) for a in params],
        out_specs=pl.BlockSpec((1, CHUNK, WIDTH), lambda i, j: (i, j, 0)),
        out_shape=jax.ShapeDtypeStruct((b, s, WIDTH), BF16),
        scratch_shapes=[
            pltpu.VMEM((1, RWKV_COLS), F32),
            pltpu.VMEM((HEADS, HEAD_DIM, HEAD_DIM), F32),
            pltpu.VMEM((CHUNK, WIDTH), F32),
        ],
        compiler_params=pltpu.CompilerParams(dimension_semantics=("arbitrary", "arbitrary"),
                                             vmem_limit_bytes=VMEM_LIMIT),
        name="rwkv",
    )(zr, *params)


def _sortable(x):
    x = jnp.where(x == 0.0, 0.0, x)
    b = lax.bitcast_convert_type(x, I32)
    return b ^ ((b >> 31) & INT_MAX)


def _dsa_kernel(qi_ref, q_ref, kiw_ref, k_ref, v_ref, band_ref, far_ref, lng_ref, lnb_ref, o_ref,
                ki3_s, kfar_s, knear_s, thr_s, jlim_s, acc_s, *, topk):
    i = pl.program_id(1)
    sp = kiw_ref.shape[1]
    pos_bits = int(sp).bit_length()
    n_far = (i + 1) // 4
    far_lim = CHUNK * i - FRONT_PAD
    lane = lax.broadcasted_iota(I32, (CHUNK, KEY_TILE), 1)

    @pl.when(i == 0)
    def _():
        rows = 320 if sp % 320 == 0 else CHUNK
        def body(c, carry):
            r0 = pl.multiple_of(c * rows, 64)
            ki = kiw_ref[0, pl.ds(r0, rows), :][:, :IDX_DIM]
            mean = jnp.mean(ki, axis=-1, keepdims=True)
            var = jnp.mean(jnp.square(ki - mean), axis=-1, keepdims=True)
            kn = (ki - mean) * lax.rsqrt(var + LN_EPS) * lng_ref[...] + lnb_ref[...]
            hi, lo = _split2(kn)
            ki3_s[pl.ds(r0, rows), :] = jnp.concatenate([hi, lo, hi], axis=1)
            return carry
        lax.fori_loop(0, sp // rows, body, 0)

    qi = qi_ref[0].reshape(IDX_HEADS * CHUNK, IDX_DIM)
    qh, ql = _split2(qi)
    lhs_i = jnp.concatenate([qh, qh, ql], axis=1)
    wq = kiw_ref[0, pl.ds(pl.multiple_of(FRONT_PAD + CHUNK * i, CHUNK), CHUNK), :]
    wscale = IDX_HEADS ** -0.5 * IDX_DIM ** -0.5
    wcol = jnp.concatenate([wq[:, IDX_DIM + h:IDX_DIM + h + 1] for h in range(IDX_HEADS)], axis=0) * wscale

    def score_tile(r0):
        x = _nt(lhs_i, ki3_s[pl.ds(r0, KEY_TILE), :])
        act = jnp.maximum(x, 0.0) * wcol
        sc = act[0:CHUNK]
        for h in range(1, IDX_HEADS):
            sc = sc + act[h * CHUNK:(h + 1) * CHUNK]
        return sc

    def far_valid(c):
        return (c * KEY_TILE + lane) < far_lim

    near_valid = ((far_lim + lane) >= 0) & (lane < NEAR_VALID)

    def far_score(c, carry):
        sc = score_tile(pl.multiple_of(FRONT_PAD + c * KEY_TILE, LANES))
        kfar_s[c] = _sortable(jnp.where(far_valid(c), sc, -jnp.inf))
        return carry

    lax.fori_loop(0, n_far, far_score, 0)
    sc = score_tile(pl.multiple_of(CHUNK * i, CHUNK))
    knear_s[...] = _sortable(jnp.where(near_valid, sc, -jnp.inf))

    thr_s[...] = jnp.full(thr_s.shape, INT_MIN, I32)
    jlim_s[...] = jnp.full(jlim_s.shape, INT_MAX, I32)
    kf = float(topk)

    def count(pred_far, pred_near):
        def body(c, acc):
            return acc + jnp.where(pred_far(c, kfar_s[c]), 1.0, 0.0)
        acc = lax.fori_loop(0, n_far, body, jnp.where(pred_near(knear_s[...]), 1.0, 0.0))
        return jnp.sum(acc, axis=1, keepdims=True)

    def count_ge(cand):
        return count(lambda c, key: key >= cand, lambda key: key >= cand)

    @pl.when(CHUNK * (i + 1) > topk)
    def _():
        ans = jnp.where(count_ge(jnp.zeros((CHUNK, 1), I32)) >= kf, 0, INT_MIN).astype(I32)

        def bit_body(t, ans):
            cand = ans | lax.shift_left(jnp.int32(1), 30 - t)
            return jnp.where(count_ge(cand) >= kf, cand, ans)

        ans = lax.fori_loop(0, 31, bit_body, ans)
        thr_s[...] = jnp.broadcast_to(ans, thr_s.shape)
        n_ge = count_ge(ans)

        @pl.when(jnp.max(n_ge) > kf)
        def _():
            need = kf - count_ge(ans + 1)

            def count_eq_before(x):
                return count(lambda c, key: (key == ans) & ((c * KEY_TILE + lane) < x),
                             lambda key: (key == ans) & ((far_lim + lane) < x))

            def pos_body(t, x):
                cand = x | lax.shift_left(jnp.int32(1), pos_bits - 1 - t)
                return jnp.where(count_eq_before(cand) < need, cand, x)

            x = lax.fori_loop(0, pos_bits, pos_body, jnp.zeros((CHUNK, 1), I32))
            jlim_s[...] = jnp.broadcast_to(x, jlim_s.shape)

    thr = thr_s[...][:, 0:1]
    jlim = jlim_s[...][:, 0:1]

    def selected(key, pos, valid):
        return valid & ((key > thr) | ((key == thr) & (pos <= jlim)))

    qs = q_ref[0] * jnp.asarray(HEAD_DIM ** -0.5, BF16)
    lane_head = lax.broadcasted_iota(I32, (CHUNK, WIDTH), 1) // HEAD_DIM
    qm = jnp.concatenate([jnp.where(lane_head == h, qs, jnp.zeros_like(qs)) for h in range(HEADS)], axis=0)
    acc_s[...] = jnp.zeros_like(acc_s)

    def attend(r0, bias, sel, m, l):
        kt = k_ref[0, pl.ds(r0, KEY_TILE), :]
        vt = v_ref[0, pl.ds(r0, KEY_TILE), :]
        s = _nt(qm, kt) + bias
        m_new, p_all = [], []
        for h in range(HEADS):
            sh = jnp.where(sel, s[h * CHUNK:(h + 1) * CHUNK], NEG_INF)
            mh = jnp.maximum(m[h * CHUNK:(h + 1) * CHUNK], jnp.max(sh, axis=1, keepdims=True))
            p_all.append(jnp.where(sel, jnp.exp(sh - mh), 0.0))
            m_new.append(mh)
        m_new = jnp.concatenate(m_new, axis=0)
        p = jnp.concatenate(p_all, axis=0)
        alpha = jnp.exp(m - m_new)
        l_new = alpha * l + jnp.sum(p, axis=1, keepdims=True)
        acc_s[...] = alpha * acc_s[...] + _mm(p.astype(BF16), vt)
        return m_new, l_new

    def far_attend(c, carry):
        m, l = carry
        sel = selected(kfar_s[c], c * KEY_TILE + lane, far_valid(c))
        return attend(pl.multiple_of(FRONT_PAD + c * KEY_TILE, LANES), far_ref[...][:, 0:1], sel, m, l)

    m0 = jnp.full((HEADS * CHUNK, 1), NEG_INF, F32)
    l0 = jnp.zeros((HEADS * CHUNK, 1), F32)
    m, l = lax.fori_loop(0, n_far, far_attend, (m0, l0))
    sel = selected(knear_s[...], far_lim + lane, near_valid)
    m, l = attend(pl.multiple_of(CHUNK * i, CHUNK), band_ref[...], sel, m, l)

    inv = 1.0 / l
    out = jnp.zeros((CHUNK, WIDTH), F32)
    for h in range(HEADS):
        rows = slice(h * CHUNK, (h + 1) * CHUNK)
        out = out + jnp.where(lane_head == h, acc_s[rows, :] * inv[rows], 0.0)
    o_ref[0] = out.astype(BF16)


def _dsa(qi_t, q, kiw_p, k_p, v_p, band, far, ln_g, ln_b):
    b, s, _ = q.shape
    sp = k_p.shape[1]
    topk = min(MAX_TOPK, s // 4)
    n_far_max = (s // CHUNK) // 4
    full = lambda a: pl.BlockSpec(a.shape, lambda i, j: (0,) * a.ndim)
    per_batch = lambda n: pl.BlockSpec((1, sp, n), lambda i, j: (i, 0, 0))
    return pl.pallas_call(
        functools.partial(_dsa_kernel, topk=topk),
        grid=(b, s // CHUNK),
        in_specs=[
            pl.BlockSpec((1, IDX_HEADS, CHUNK, IDX_DIM), lambda i, j: (i, 0, j, 0)),
            pl.BlockSpec((1, CHUNK, WIDTH), lambda i, j: (i, j, 0)),
            per_batch(LANES), per_batch(WIDTH), per_batch(WIDTH),
            full(band), full(far), full(ln_g), full(ln_b),
        ],
        out_specs=pl.BlockSpec((1, CHUNK, WIDTH), lambda i, j: (i, j, 0)),
        out_shape=jax.ShapeDtypeStruct((b, s, WIDTH), BF16),
        scratch_shapes=[
            pltpu.VMEM((sp, 3 * IDX_DIM), BF16),
            pltpu.VMEM((max(n_far_max, 1), CHUNK, KEY_TILE), I32),
            pltpu.VMEM((CHUNK, KEY_TILE), I32),
            pltpu.VMEM((CHUNK, LANES), I32),
            pltpu.VMEM((CHUNK, LANES), I32),
            pltpu.VMEM((HEADS * CHUNK, WIDTH), F32),
        ],
        compiler_params=pltpu.CompilerParams(dimension_semantics=("arbitrary", "arbitrary"),
                                             vmem_limit_bytes=VMEM_LIMIT),
        name="dsa",
    )(qi_t, q, kiw_p, k_p, v_p, band, far, ln_g, ln_b)


def _post_kernel(y1_ref, y2_ref, x_ref, p_ref, woa_ref, wob_ref, w1_ref, w2_ref, wp_ref, wg_ref,
                 g_mix_ref, g_f1_ref, g_f2_ref, g_p1_ref, g_p2_ref, o_ref):
    mix = _mm(y1_ref[...], woa_ref[...]) + _mm(y2_ref[...], wob_ref[...])
    x1 = x_ref[...] + _rms(mix, g_mix_ref[...])
    h = _rms(x1, g_f1_ref[...]).astype(BF16)
    hid = jnp.square(jnp.maximum(_mm(h, w1_ref[...]), 0.0)).astype(BF16)
    x2 = x1 + _rms(_mm(hid, w2_ref[...]), g_f2_ref[...])
    gate = _sigmoid(_mm(_rms(x2, g_p1_ref[...]).astype(BF16), wg_ref[...]))
    pe = _mm(p_ref[...].astype(BF16), wp_ref[...]) * gate
    o_ref[...] = x2 + _rms(pe, g_p2_ref[...])


def _post(y1, y2, x2d, p2d, woa, wob, w1, w2, wp, wg, gains, tm=256):
    m, d = x2d.shape
    row = lambda n: pl.BlockSpec((tm, n), lambda i: (i, 0))
    const = lambda a: pl.BlockSpec(a.shape, lambda i: (0, 0), pipeline_mode=pl.Buffered(1))
    weights = (woa, wob, w1, w2, wp, wg) + tuple(gains)
    return pl.pallas_call(
        _post_kernel,
        grid=(m // tm,),
        in_specs=[row(WIDTH), row(WIDTH), row(d), row(p2d.shape[1])] + [const(a) for a in weights],
        out_specs=row(d),
        out_shape=jax.ShapeDtypeStruct((m, d), F32),
        compiler_params=pltpu.CompilerParams(dimension_semantics=("arbitrary",), vmem_limit_bytes=VMEM_LIMIT),
        name="post",
    )(y1, y2, x2d, p2d, *weights)


def _layer(x, p, w_in, shift_mu, w0, w2r, a0, a2, g2, k_k, k_a, r_k, ln_g, ln_b, idx_g, idx_b, rel_bias,
           w_out, w_f1, w_f2, w_ple, w_gate, n_mix_pre, n_mix_post, n_ffn_pre, n_ffn_post, n_ple_pre, n_ple_post):
    b, s, d = x.shape
    m = b * s
    row = lambda a: a.reshape(1, -1).astype(F32)

    c = [0, WIDTH, WIDTH + DECAY_RANK, 2 * WIDTH + DECAY_RANK, 3 * WIDTH + DECAY_RANK,
         3 * WIDTH + DECAY_RANK + A_RANK, RWKV_COLS]
    perm = jnp.concatenate([jnp.arange(c[0], c[1]), jnp.arange(c[2], c[3]), jnp.arange(c[3], c[4]),
                            jnp.arange(c[1], c[2]), jnp.arange(c[4], c[5]), jnp.arange(c[5], c[6])])
    w_main = jnp.concatenate([w_in[:, :RWKV_COLS][:, perm], w_in[:, RWKV_COLS:RWKV_COLS + 3 * WIDTH]],
                             axis=1).astype(BF16)
    w_idx = w_in[:, RWKV_COLS + 3 * WIDTH:]
    w_idx = jnp.pad(w_idx, ((0, 0), (0, N_IDX - w_idx.shape[1])))
    w_idx_hi = w_idx.astype(BF16)
    w_idx_lo = (w_idx - w_idx_hi.astype(F32)).astype(BF16)

    zr, q, k, v, qi, kiw = _in_proj(x.reshape(m, d), row(n_mix_pre), w_main, w_idx_hi, w_idx_lo)

    zeros = jnp.zeros((DECAY_RANK, WIDTH), F32)
    w_lora = jnp.concatenate([jnp.concatenate([w2r, zeros], axis=1),
                              jnp.concatenate([zeros, a2], axis=1)], axis=0).astype(BF16)
    head_of = jnp.arange(WIDTH) // HEAD_DIM
    block_ones = (head_of[:, None] == head_of[None, :]).astype(BF16)
    y_rwkv = _rwkv(zr.reshape(b, s, RWKV_COLS), row(shift_mu[perm]), row(w0), w_lora, row(a0), g2.astype(BF16),
                   row(k_k), row(k_a), row(r_k), row(ln_g), row(ln_b), block_ones)

    band, far = _bias_tiles(rel_bias.astype(F32))
    pad = lambda a: jnp.pad(a.reshape(b, s, -1), ((0, 0), (FRONT_PAD, BACK_PAD), (0, 0)))
    qi_t = qi.reshape(b, s, IDX_HEADS, IDX_DIM).transpose(0, 2, 1, 3)
    y_dsa = _dsa(qi_t, q.reshape(b, s, WIDTH), pad(kiw), pad(k), pad(v), band, far, row(idx_g), row(idx_b))

    gains = tuple(row(a) for a in (n_mix_post, n_ffn_pre, n_ffn_post, n_ple_pre, n_ple_post))
    out = _post(y_rwkv.reshape(m, WIDTH), y_dsa.reshape(m, WIDTH), x.reshape(m, d), p.reshape(m, -1),
                w_out[:WIDTH].astype(BF16), w_out[WIDTH:].astype(BF16), w_f1.astype(BF16), w_f2.astype(BF16),
                w_ple.astype(BF16), w_gate.astype(BF16), gains)
    return out.reshape(b, s, d)


def kernel(x, p, w_in, shift_mu, rwkv_w0, rwkv_w2, rwkv_a0, rwkv_a2, rwkv_g2, rwkv_k_k, rwkv_k_a, rwkv_r_k, rwkv_ln_g, rwkv_ln_b, idx_k_ln_g, idx_k_ln_b, rel_bias, w_out, w_ffn1, w_ffn2, w_ple, w_ple_gate, norm_mix_pre, norm_mix_post, norm_ffn_pre, norm_ffn_post, norm_ple_pre, norm_ple_post):
    for i in range(p.shape[0]):
        x = _layer(x, p[i], w_in[i], shift_mu[i], rwkv_w0[i], rwkv_w2[i], rwkv_a0[i], rwkv_a2[i], rwkv_g2[i],
                   rwkv_k_k[i], rwkv_k_a[i], rwkv_r_k[i], rwkv_ln_g[i], rwkv_ln_b[i], idx_k_ln_g[i],
                   idx_k_ln_b[i], rel_bias, w_out[i], w_ffn1[i], w_ffn2[i], w_ple[i], w_ple_gate[i],
                   norm_mix_pre[i], norm_mix_post[i], norm_ffn_pre[i], norm_ffn_post[i], norm_ple_pre[i],
                   norm_ple_post[i])
    return x
```

```python
import functools
import math

import jax
import jax.numpy as jnp
from jax import lax
from jax.experimental import pallas as pl
from jax.experimental.pallas import tpu as pltpu

F32 = jnp.float32
BF16 = jnp.bfloat16
I32 = jnp.int32

CHUNK = 64
HEADS = 8
HEAD_DIM = 64
WIDTH = HEADS * HEAD_DIM
DECAY_RANK = 64
A_RANK = 64
GATE_RANK = 128
GN_EPS = 64e-5
IDX_HEADS = 8
IDX_DIM = 64
MAX_TOPK = 256
REL_BUCKETS = 32
REL_MAX_DISTANCE = 128
NORM_EPS = 1e-6
LN_EPS = 1e-6
NEG_INF = -1e30
RWKV_COLS = 1792
N_MAIN = RWKV_COLS + 3 * WIDTH
N_IDX = 640

LANES = 128
KEY_TILE = 256
HEAD_PAIRS = WIDTH // LANES
FRONT_PAD = 2 * CHUNK
BACK_PAD = CHUNK
VMEM_LIMIT = 56 * 1024 * 1024

INT_MIN = -(2**31)
INT_MAX = 2**31 - 1


def _nt(a, b):
    return lax.dot_general(a, b, (((1,), (1,)), ((), ())), preferred_element_type=F32)


def _tn(a, b):
    return lax.dot_general(a, b, (((0,), (0,)), ((), ())), preferred_element_type=F32)


def _mm(a, b):
    return jnp.dot(a, b, preferred_element_type=F32)


def _split2(x):
    hi = x.astype(BF16)
    lo = (x - hi.astype(F32)).astype(BF16)
    return hi, lo


def _split3(x):
    hi = x.astype(BF16)
    r = x - hi.astype(F32)
    mid = r.astype(BF16)
    lo = (r - mid.astype(F32)).astype(BF16)
    return hi, mid, lo


def _rms(x, g):
    return x * lax.rsqrt(jnp.mean(x * x, axis=-1, keepdims=True) + NORM_EPS) * g


def _sigmoid(x):
    return 1.0 / (1.0 + jnp.exp(-x))


def _in_proj_kernel(x_ref, g_ref, wm_ref, wih_ref, wil_ref, zr_ref, q_ref, k_ref, v_ref, qi_ref, kiw_ref):
    h = _rms(x_ref[...], g_ref[...])
    hh, hl = _split2(h)
    zm = _mm(hh, wm_ref[...])
    zr_ref[...] = zm[:, :RWKV_COLS]
    q_ref[...] = zm[:, RWKV_COLS:RWKV_COLS + WIDTH].astype(BF16)
    k_ref[...] = zm[:, RWKV_COLS + WIDTH:RWKV_COLS + 2 * WIDTH].astype(BF16)
    v_ref[...] = zm[:, RWKV_COLS + 2 * WIDTH:].astype(BF16)
    zi = _mm(hh, wih_ref[...]) + _mm(hh, wil_ref[...]) + _mm(hl, wih_ref[...])
    qi_ref[...] = zi[:, :WIDTH]
    kiw_ref[...] = zi[:, WIDTH:]


def _in_proj(x2, g, wm, wih, wil, tm=256):
    m, d = x2.shape
    row = lambda n: pl.BlockSpec((tm, n), lambda i: (i, 0))
    full = lambda a: pl.BlockSpec(a.shape, lambda i: (0, 0))
    return pl.pallas_call(
        _in_proj_kernel,
        grid=(m // tm,),
        in_specs=[row(d), full(g), full(wm), full(wih), full(wil)],
        out_specs=[row(RWKV_COLS), row(WIDTH), row(WIDTH), row(WIDTH), row(WIDTH), row(LANES)],
        out_shape=[
            jax.ShapeDtypeStruct((m, RWKV_COLS), F32),
            jax.ShapeDtypeStruct((m, WIDTH), BF16),
            jax.ShapeDtypeStruct((m, WIDTH), BF16),
            jax.ShapeDtypeStruct((m, WIDTH), BF16),
            jax.ShapeDtypeStruct((m, WIDTH), F32),
            jax.ShapeDtypeStruct((m, LANES), F32),
        ],
        compiler_params=pltpu.CompilerParams(dimension_semantics=("arbitrary",), vmem_limit_bytes=VMEM_LIMIT),
        name="in_proj",
    )(x2, g, wm, wih, wil)


def _t5_bucket(rel):
    half = REL_BUCKETS // 2
    max_exact = half // 2
    ret = jnp.where(rel > 0, half, 0)
    n = jnp.abs(rel)
    nf = jnp.maximum(n, 1).astype(F32)
    large = max_exact + (jnp.log(nf / max_exact) / math.log(REL_MAX_DISTANCE / max_exact)
                         * (half - max_exact)).astype(I32)
    large = jnp.minimum(large, half - 1)
    return ret + jnp.where(n < max_exact, n, large)


def _bias_kernel(tab_ref, band_ref):
    r = lax.broadcasted_iota(I32, (CHUNK, KEY_TILE), 0)
    c = lax.broadcasted_iota(I32, (CHUNK, KEY_TILE), 1)
    near_bucket = _t5_bucket(c - FRONT_PAD - r)
    far_bucket = _t5_bucket(jnp.full((CHUNK, KEY_TILE), -(FRONT_PAD + 1), I32))
    for h in range(HEADS):
        near = jnp.zeros((CHUNK, KEY_TILE), F32)
        far = jnp.zeros((CHUNK, KEY_TILE), F32)
        for j in range(REL_BUCKETS):
            t = tab_ref[j, h]
            near = jnp.where(near_bucket == j, t, near)
            far = jnp.where(far_bucket == j, t, far)
        band_ref[h * CHUNK:(h + 1) * CHUNK, :] = near - far


def _bias_tiles(rel_bias):
    return pl.pallas_call(
        _bias_kernel,
        in_specs=[pl.BlockSpec(memory_space=pltpu.SMEM)],
        out_shape=jax.ShapeDtypeStruct((HEADS * CHUNK, KEY_TILE), F32),
        name="bias_tiles",
    )(rel_bias)


def _dot3(a, b):
    ah, al = _split2(a)
    bh, bl = _split2(b)
    lhs = jnp.concatenate([ah, ah, al], axis=1)
    rhs = jnp.concatenate([bh, bl, bh], axis=0)
    return _mm(lhs, rhs)


def _inv_unit_lower(lows):
    n = lows[0].shape[0]
    eye = (lax.broadcasted_iota(I32, (n, n), 0) == lax.broadcasted_iota(I32, (n, n), 1)).astype(F32)
    ts = [eye + low for low in lows]
    ps = list(lows)
    for _ in range(int(math.log2(n)) - 1):
        ps = [_dot3(p, p) for p in ps]
        ts = [t + _dot3(t, p) for t, p in zip(ts, ps)]
    return ts


def _rwkv_kernel(z_ref, mu_ref, w0_ref, wl_ref, a0_ref, g2_ref, kk_ref, ka_ref, rk_ref, lng_ref, lnb_ref,
                 bd_ref, y_ref, prev_s, st_s, oh_s):
    ci = pl.program_id(1)

    @pl.when(ci == 0)
    def _():
        prev_s[...] = jnp.zeros_like(prev_s)
        st_s[...] = jnp.zeros_like(st_s)

    z = z_ref[0]
    row = lax.broadcasted_iota(I32, z.shape, 0)
    zp = jnp.where(row == 0, prev_s[...], pltpu.roll(z, 1, axis=0))
    prev_s[...] = z[CHUNK - 1:CHUNK, :]
    zs = z + (zp - z) * mu_ref[...]
    r = zs[:, 0:WIDTH]
    k = zs[:, WIDTH:2 * WIDTH]
    v = zs[:, 2 * WIDTH:3 * WIDTH]
    zl = zs[:, 3 * WIDTH:3 * WIDTH + LANES]
    zg = zs[:, 3 * WIDTH + LANES:]

    lane = lax.broadcasted_iota(I32, zl.shape, 1)
    zl = jnp.where(lane < DECAY_RANK, jnp.tanh(zl), zl)
    la = _mm(zl.astype(BF16), wl_ref[...])
    y0 = -(w0_ref[...] + la[:, :WIDTH])
    softplus = jnp.maximum(y0, 0.0) + jnp.log(1.0 + jnp.exp(-jnp.abs(y0)))
    ld = -jnp.exp(-softplus - 0.5)
    a = _sigmoid(a0_ref[...] + la[:, WIDTH:])
    g = _mm(_sigmoid(zg).astype(BF16), g2_ref[...])

    bd = bd_ref[...]

    def seg_sum(x):
        xh, xl = _split2(x)
        return _mm(xh, bd) + _mm(xl, bd)

    kk = k * kk_ref[...]
    kk = kk / jnp.maximum(jnp.sqrt(seg_sum(kk * kk)), 1e-12)
    kmod = k * (1.0 + (a - 1.0) * ka_ref[...])
    bvec = kk * a

    ti = lax.broadcasted_iota(I32, (CHUNK, CHUNK), 0)
    tj = lax.broadcasted_iota(I32, (CHUNK, CHUNK), 1)
    tri = (ti >= tj).astype(BF16)
    l1, l2, l3 = _split3(ld)
    cum = _mm(tri, l1) + _mm(tri, l2) + _mm(tri, l3)
    cum_x = cum - ld
    cref = cum[CHUNK // 2 - 1:CHUNK // 2, :]
    clast = cum[CHUNK - 1:CHUNK, :]
    e_neg = jnp.exp(cref - cum)
    rt = r * jnp.exp(cum - cref)
    at = -kk * jnp.exp(cum_x - cref)
    kt = kmod * e_neg
    bt = bvec * e_neg
    r0 = r * jnp.exp(cum)
    a0v = -kk * jnp.exp(cum_x)
    e_end = jnp.exp(clast - cum)
    kc = kmod * e_end
    bc = bvec * e_end
    pc = jnp.exp(clast)

    tok = lax.broadcasted_iota(I32, (CHUNK, 2 * CHUNK), 0)
    src = lax.broadcasted_iota(I32, (CHUNK, 2 * CHUNK), 1) & (CHUNK - 1)
    strict2 = tok > src
    incl2 = tok >= src

    hs = range(HEADS)
    sls = [slice(h * HEAD_DIM, (h + 1) * HEAD_DIM) for h in hs]
    stack = lambda x, y, sl: jnp.concatenate([x[:, sl], y[:, sl]], axis=0).astype(BF16)
    aa = [_nt(stack(at, rt, sl), stack(bt, kt, sl)) for sl in sls]
    top = [jnp.where(strict2, x[:CHUNK], 0.0) for x in aa]
    arr = [jnp.where(incl2, x[CHUNK:], 0.0).astype(BF16) for x in aa]
    s_f = [st_s[h] for h in hs]
    s_b = [s.astype(BF16) for s in s_f]
    v_b = [v[:, sl].astype(BF16) for sl in sls]
    x0 = [_nt(a0v[:, sl].astype(BF16), s) for sl, s in zip(sls, s_b)]
    yk = [_mm(tp[:, CHUNK:].astype(BF16), vb) for tp, vb in zip(top, v_b)]
    o_st = [_nt(r0[:, sl].astype(BF16), s) for sl, s in zip(sls, s_b)]
    t = _inv_unit_lower([tp[:, :CHUNK] for tp in top])
    u = [_mm(th.astype(BF16), (a_ + b_).astype(BF16)) for th, a_, b_ in zip(t, x0, yk)]
    uv = [jnp.concatenate([uh.astype(BF16), vb], axis=0) for uh, vb in zip(u, v_b)]
    o_h = [a_ + _mm(ar, w) for a_, ar, w in zip(o_st, arr, uv)]
    s_new = [s * pc[:, sl] + _tn(w, stack(bc, kc, sl)) for s, sl, w in zip(s_f, sls, uv)]
    for h in hs:
        st_s[h] = s_new[h]
        oh_s[:, sls[h]] = o_h[h]

    o = oh_s[...]
    inv_n = 1.0 / HEAD_DIM
    mean = seg_sum(o) * inv_n
    d = o - mean
    var = seg_sum(d * d) * inv_n
    on = d * lax.rsqrt(var + GN_EPS) * lng_ref[...] + lnb_ref[...]
    bonus = seg_sum(r * kmod * rk_ref[...]) * v
    y_ref[0] = ((on + bonus) * g).astype(BF16)


def _rwkv(zr, mu, w0, wl, a0, g2, k_k, k_a, r_k, ln_g, ln_b, bd):
    b, s, _ = zr.shape
    full = lambda a: pl.BlockSpec(a.shape, lambda i, j: (0,) * a.ndim)
    params = (mu, w0, wl, a0, g2, k_k, k_a, r_k, ln_g, ln_b, bd)
    return pl.pallas_call(
        _rwkv_kernel,
        grid=(b, s // CHUNK),
        in_specs=[pl.BlockSpec((1, CHUNK, RWKV_COLS), lambda i, j: (i, j, 0))] + [full(a) for a in params],
        out_specs=pl.BlockSpec((1, CHUNK, WIDTH), lambda i, j: (i, j, 0)),
        out_shape=jax.ShapeDtypeStruct((b, s, WIDTH), BF16),
        scratch_shapes=[
            pltpu.VMEM((1, RWKV_COLS), F32),
            pltpu.VMEM((HEADS, HEAD_DIM, HEAD_DIM), F32),
            pltpu.VMEM((CHUNK, WIDTH), F32),
        ],
        compiler_params=pltpu.CompilerParams(dimension_semantics=("arbitrary", "arbitrary"),
                                             vmem_limit_bytes=VMEM_LIMIT),
        name="rwkv",
    )(zr, *params)


def _sortable(x):
    x = jnp.where(x == 0.0, 0.0, x)
    b = lax.bitcast_convert_type(x, I32)
    return b ^ ((b >> 31) & INT_MAX)


def _dsa_kernel(qi_ref, q_ref, kiw_ref, k_ref, v_ref, band_ref, lng_ref, lnb_ref, o_ref,
                ki3_s, key_s, thr_s, jlim_s, lg_s, mx_s, mb_s, sm_s, pv_s, *, topk, nb):
    i = pl.program_id(1)
    sp = kiw_ref.shape[1]
    pos_bits = int(sp).bit_length()
    n_far = (i + 1) // 4
    n_tiles = n_far + 1
    far_lim = CHUNK * i - FRONT_PAD
    lane1 = lax.broadcasted_iota(I32, (CHUNK, LANES), 1)
    halves = [slice(0, LANES), slice(LANES, KEY_TILE)]
    kf = float(topk)

    def tile_pos0(c):
        return jnp.where(c < n_far, c * KEY_TILE, far_lim)

    def tile_end(c):
        return jnp.where(c < n_far, far_lim, CHUNK * (i + 1))

    def tile_row0(c):
        return pl.multiple_of(FRONT_PAD + tile_pos0(c), CHUNK)

    def half_pos(c, half):
        return tile_pos0(c) + half * LANES + lane1

    @pl.when(i == 0)
    def _():
        rows = 320 if sp % 320 == 0 else CHUNK

        def body(t, carry):
            bb = t // (sp // rows)
            r0 = pl.multiple_of((t % (sp // rows)) * rows, CHUNK)
            ki = kiw_ref[bb, pl.ds(r0, rows), :][:, :IDX_DIM]
            mean = jnp.mean(ki, axis=-1, keepdims=True)
            var = jnp.mean(jnp.square(ki - mean), axis=-1, keepdims=True)
            kn = (ki - mean) * lax.rsqrt(var + LN_EPS) * lng_ref[...] + lnb_ref[...]
            hi, lo = _split2(kn)
            ki3_s[bb, pl.ds(r0, rows), :] = jnp.concatenate([hi, lo, hi], axis=1)
            return carry

        lax.fori_loop(0, nb * (sp // rows), body, 0)

    def phase_a(bb, carry):
        qi = qi_ref[bb].reshape(IDX_HEADS * CHUNK, IDX_DIM)
        qh, ql = _split2(qi)
        lhs_i = jnp.concatenate([qh, qh, ql], axis=1)
        wq = kiw_ref[bb, pl.ds(pl.multiple_of(FRONT_PAD + CHUNK * i, CHUNK), CHUNK), :]
        wscale = IDX_HEADS ** -0.5 * IDX_DIM ** -0.5
        wcol = jnp.concatenate([wq[:, IDX_DIM + h:IDX_DIM + h + 1] for h in range(IDX_HEADS)], axis=0) * wscale
        wb = jnp.broadcast_to(wcol, (IDX_HEADS * CHUNK, LANES))

        def tile(c, carry):
            x = _nt(lhs_i, ki3_s[bb, pl.ds(tile_row0(c), KEY_TILE), :])
            end = tile_end(c)
            for half, hs_ in enumerate(halves):
                act = jnp.maximum(x[:, hs_], 0.0) * wb
                sc = act[0:CHUNK]
                for h in range(1, IDX_HEADS):
                    sc = sc + act[h * CHUNK:(h + 1) * CHUNK]
                pos = half_pos(c, half)
                valid = (pos >= 0) & (pos < end)
                key_s[bb, c, :, hs_] = _sortable(jnp.where(valid, sc, -jnp.inf))
            return carry

        lax.fori_loop(0, n_tiles, tile, 0)
        return carry

    lax.fori_loop(0, nb, phase_a, 0)

    thr_s[...] = jnp.full(thr_s.shape, INT_MIN, I32)
    jlim_s[...] = jnp.full(jlim_s.shape, INT_MAX, I32)

    def count_all(pred):
        def body(c, accs):
            out = []
            for bb in range(nb):
                a = accs[bb]
                for half, hs_ in enumerate(halves):
                    a = a + jnp.where(pred(bb, key_s[bb, c, :, hs_], half_pos(c, half)), 1.0, 0.0)
                out.append(a)
            return tuple(out)

        accs = lax.fori_loop(0, n_tiles, body, tuple(jnp.zeros((CHUNK, LANES), F32) for _ in range(nb)))
        return [jnp.sum(a, axis=1, keepdims=True) for a in accs]

    @pl.when(CHUNK * (i + 1) > topk)
    def _():
        n_pos = count_all(lambda bb, key, pos: key >= 0)
        ans0 = tuple(jnp.where(n >= kf, 0, INT_MIN).astype(I32) for n in n_pos)

        def bit_body(t, ans):
            bit = lax.shift_left(jnp.int32(1), 30 - t)
            cands = [a | bit for a in ans]
            cnts = count_all(lambda bb, key, pos: key >= cands[bb])
            return tuple(jnp.where(n >= kf, cd, a) for n, cd, a in zip(cnts, cands, ans))

        ans = lax.fori_loop(0, 31, bit_body, ans0)
        for bb in range(nb):
            thr_s[bb] = jnp.broadcast_to(ans[bb], (CHUNK, LANES))
        n_ge = count_all(lambda bb, key, pos: key >= ans[bb])
        most = n_ge[0]
        for bb in range(1, nb):
            most = jnp.maximum(most, n_ge[bb])

        @pl.when(jnp.max(most) > kf)
        def _():
            n_gt = count_all(lambda bb, key, pos: key > ans[bb])
            need = [kf - n for n in n_gt]

            def pos_body(t, xs):
                bit = lax.shift_left(jnp.int32(1), pos_bits - 1 - t)
                cands = [x | bit for x in xs]
                cnts = count_all(lambda bb, key, pos: (key == ans[bb]) & (pos < cands[bb]))
                return tuple(jnp.where(n < nd, cd, x) for n, nd, cd, x in zip(cnts, need, cands, xs))

            xs = lax.fori_loop(0, pos_bits, pos_body, tuple(jnp.zeros((CHUNK, 1), I32) for _ in range(nb)))
            for bb in range(nb):
                jlim_s[bb] = jnp.broadcast_to(xs[bb], (CHUNK, LANES))

    pair_lane_head = lane1 // HEAD_DIM

    def phase_c(bb, carry):
        qs = q_ref[bb] * jnp.asarray(HEAD_DIM ** -0.5, BF16)
        zero = jnp.zeros((CHUNK, LANES), BF16)
        qm = []
        for g in range(HEAD_PAIRS):
            qg = qs[:, g * LANES:(g + 1) * LANES]
            qm.append(jnp.concatenate([jnp.where(pair_lane_head == 0, qg, zero),
                                       jnp.where(pair_lane_head == 1, qg, zero)], axis=0))
        thr = thr_s[bb]
        jlim = jlim_s[bb]
        mx_s[...] = jnp.full(mx_s.shape, NEG_INF, F32)

        def logits(c, bias):
            kt = k_ref[bb, pl.ds(tile_row0(c), KEY_TILE), :]
            end = tile_end(c)
            sel = []
            for half, hs_ in enumerate(halves):
                key = key_s[bb, c, :, hs_]
                pos = half_pos(c, half)
                sel.append((pos >= 0) & (pos < end) & ((key > thr) | ((key == thr) & (pos <= jlim))))
            for g in range(HEAD_PAIRS):
                s = _nt(qm[g], kt[:, g * LANES:(g + 1) * LANES])
                if bias is not None:
                    s = s + bias[g * 2 * CHUNK:(g + 1) * 2 * CHUNK]
                for hh in range(2):
                    rows = slice((2 * g + hh) * CHUNK, (2 * g + hh + 1) * CHUNK)
                    sub = s[hh * CHUNK:(hh + 1) * CHUNK]
                    m0 = jnp.where(sel[0], sub[:, halves[0]], NEG_INF)
                    m1 = jnp.where(sel[1], sub[:, halves[1]], NEG_INF)
                    lg_s[c, rows, halves[0]] = m0
                    lg_s[c, rows, halves[1]] = m1
                    mx_s[rows] = jnp.maximum(mx_s[rows], jnp.maximum(m0, m1))

        def far_logits(c, carry):
            logits(c, None)
            return carry

        lax.fori_loop(0, n_far, far_logits, 0)
        logits(n_far, band_ref[...])

        mb_s[...] = jnp.broadcast_to(jnp.max(mx_s[...], axis=1, keepdims=True), mb_s.shape)
        sm_s[...] = jnp.zeros_like(sm_s)
        pv_s[...] = jnp.zeros_like(pv_s)

        def weigh(c, carry):
            vt = v_ref[bb, pl.ds(tile_row0(c), KEY_TILE), :]
            for g in range(HEAD_PAIRS):
                rows = slice(g * 2 * CHUNK, (g + 1) * 2 * CHUNK)
                m = mb_s[rows]
                p0 = jnp.exp(lg_s[c, rows, halves[0]] - m)
                p1 = jnp.exp(lg_s[c, rows, halves[1]] - m)
                sm_s[rows] = sm_s[rows] + (p0 + p1)
                p = jnp.concatenate([p0, p1], axis=1).astype(BF16)
                pv_s[rows] = pv_s[rows] + _mm(p, vt[:, g * LANES:(g + 1) * LANES])
            return carry

        lax.fori_loop(0, n_tiles, weigh, 0)

        outn = pv_s[...] / jnp.sum(sm_s[...], axis=1, keepdims=True)
        cols = []
        for g in range(HEAD_PAIRS):
            top = outn[(2 * g) * CHUNK:(2 * g + 1) * CHUNK]
            bot = outn[(2 * g + 1) * CHUNK:(2 * g + 2) * CHUNK]
            cols.append(jnp.where(pair_lane_head == 0, top, bot))
        o_ref[bb] = jnp.concatenate(cols, axis=1).astype(BF16)
        return carry

    lax.fori_loop(0, nb, phase_c, 0)


def _dsa(qi_t, q, kiw_p, k_p, v_p, band, ln_g, ln_b):
    b, s, _ = q.shape
    sp = k_p.shape[1]
    topk = min(MAX_TOPK, s // 4)
    max_tiles = (s // CHUNK) // 4 + 1
    nb = 4 if b % 4 == 0 else (2 if b % 2 == 0 else 1)
    rows = HEADS * CHUNK
    full = lambda a: pl.BlockSpec(a.shape, lambda i, j: (0,) * a.ndim)
    per_group = lambda n: pl.BlockSpec((nb, sp, n), lambda i, j: (i, 0, 0), pipeline_mode=pl.Buffered(1))
    return pl.pallas_call(
        functools.partial(_dsa_kernel, topk=topk, nb=nb),
        grid=(b // nb, s // CHUNK),
        in_specs=[
            pl.BlockSpec((nb, IDX_HEADS, CHUNK, IDX_DIM), lambda i, j: (i, 0, j, 0)),
            pl.BlockSpec((nb, CHUNK, WIDTH), lambda i, j: (i, j, 0)),
            per_group(LANES), per_group(WIDTH), per_group(WIDTH),
            full(band), full(ln_g), full(ln_b),
        ],
        out_specs=pl.BlockSpec((nb, CHUNK, WIDTH), lambda i, j: (i, j, 0)),
        out_shape=jax.ShapeDtypeStruct((b, s, WIDTH), BF16),
        scratch_shapes=[
            pltpu.VMEM((nb, sp, 3 * IDX_DIM), BF16),
            pltpu.VMEM((nb, max_tiles, CHUNK, KEY_TILE), I32),
            pltpu.VMEM((nb, CHUNK, LANES), I32),
            pltpu.VMEM((nb, CHUNK, LANES), I32),
            pltpu.VMEM((max_tiles, rows, KEY_TILE), F32),
            pltpu.VMEM((rows, LANES), F32),
            pltpu.VMEM((rows, LANES), F32),
            pltpu.VMEM((rows, LANES), F32),
            pltpu.VMEM((rows, LANES), F32),
        ],
        compiler_params=pltpu.CompilerParams(dimension_semantics=("arbitrary", "arbitrary"),
                                             vmem_limit_bytes=VMEM_LIMIT),
        name="dsa",
    )(qi_t, q, kiw_p, k_p, v_p, band, ln_g, ln_b)


def _post_kernel(y1_ref, y2_ref, x_ref, p_ref, woa_ref, wob_ref, w1_ref, w2_ref, wp_ref, wg_ref,
                 g_mix_ref, g_f1_ref, g_f2_ref, g_p1_ref, g_p2_ref, o_ref):
    mix = _mm(y1_ref[...], woa_ref[...]) + _mm(y2_ref[...], wob_ref[...])
    x1 = x_ref[...] + _rms(mix, g_mix_ref[...])
    h = _rms(x1, g_f1_ref[...]).astype(BF16)
    hid = jnp.square(jnp.maximum(_mm(h, w1_ref[...]), 0.0)).astype(BF16)
    x2 = x1 + _rms(_mm(hid, w2_ref[...]), g_f2_ref[...])
    gate = _sigmoid(_mm(_rms(x2, g_p1_ref[...]).astype(BF16), wg_ref[...]))
    pe = _mm(p_ref[...].astype(BF16), wp_ref[...]) * gate
    o_ref[...] = x2 + _rms(pe, g_p2_ref[...])


def _post(y1, y2, x2d, p2d, woa, wob, w1, w2, wp, wg, gains, tm=256):
    m, d = x2d.shape
    row = lambda n: pl.BlockSpec((tm, n), lambda i: (i, 0))
    const = lambda a: pl.BlockSpec(a.shape, lambda i: (0, 0), pipeline_mode=pl.Buffered(1))
    weights = (woa, wob, w1, w2, wp, wg) + tuple(gains)
    return pl.pallas_call(
        _post_kernel,
        grid=(m // tm,),
        in_specs=[row(WIDTH), row(WIDTH), row(d), row(p2d.shape[1])] + [const(a) for a in weights],
        out_specs=row(d),
        out_shape=jax.ShapeDtypeStruct((m, d), F32),
        compiler_params=pltpu.CompilerParams(dimension_semantics=("arbitrary",), vmem_limit_bytes=VMEM_LIMIT),
        name="post",
    )(y1, y2, x2d, p2d, *weights)


def _layer(x, p, w_in, shift_mu, w0, w2r, a0, a2, g2, k_k, k_a, r_k, ln_g, ln_b, idx_g, idx_b, rel_bias,
           w_out, w_f1, w_f2, w_ple, w_gate, n_mix_pre, n_mix_post, n_ffn_pre, n_ffn_post, n_ple_pre, n_ple_post):
    b, s, d = x.shape
    m = b * s
    row = lambda a: a.reshape(1, -1).astype(F32)

    c = [0, WIDTH, WIDTH + DECAY_RANK, 2 * WIDTH + DECAY_RANK, 3 * WIDTH + DECAY_RANK,
         3 * WIDTH + DECAY_RANK + A_RANK, RWKV_COLS]
    perm = jnp.concatenate([jnp.arange(c[0], c[1]), jnp.arange(c[2], c[3]), jnp.arange(c[3], c[4]),
                            jnp.arange(c[1], c[2]), jnp.arange(c[4], c[5]), jnp.arange(c[5], c[6])])
    w_main = jnp.concatenate([w_in[:, :RWKV_COLS][:, perm], w_in[:, RWKV_COLS:RWKV_COLS + 3 * WIDTH]],
                             axis=1).astype(BF16)
    w_idx = w_in[:, RWKV_COLS + 3 * WIDTH:]
    w_idx = jnp.pad(w_idx, ((0, 0), (0, N_IDX - w_idx.shape[1])))
    w_idx_hi = w_idx.astype(BF16)
    w_idx_lo = (w_idx - w_idx_hi.astype(F32)).astype(BF16)

    zr, q, k, v, qi, kiw = _in_proj(x.reshape(m, d), row(n_mix_pre), w_main, w_idx_hi, w_idx_lo)

    zeros = jnp.zeros((DECAY_RANK, WIDTH), F32)
    w_lora = jnp.concatenate([jnp.concatenate([w2r, zeros], axis=1),
                              jnp.concatenate([zeros, a2], axis=1)], axis=0).astype(BF16)
    head_of = jnp.arange(WIDTH) // HEAD_DIM
    block_ones = (head_of[:, None] == head_of[None, :]).astype(BF16)
    y_rwkv = _rwkv(zr.reshape(b, s, RWKV_COLS), row(shift_mu[perm]), row(w0), w_lora, row(a0), g2.astype(BF16),
                   row(k_k), row(k_a), row(r_k), row(ln_g), row(ln_b), block_ones)

    band = _bias_tiles(rel_bias.astype(F32))
    pad = lambda a: jnp.pad(a.reshape(b, s, -1), ((0, 0), (FRONT_PAD, BACK_PAD), (0, 0)))
    qi_t = qi.reshape(b, s, IDX_HEADS, IDX_DIM).transpose(0, 2, 1, 3)
    y_dsa = _dsa(qi_t, q.reshape(b, s, WIDTH), pad(kiw), pad(k), pad(v), band, row(idx_g), row(idx_b))

    gains = tuple(row(a) for a in (n_mix_post, n_ffn_pre, n_ffn_post, n_ple_pre, n_ple_post))
    out = _post(y_rwkv.reshape(m, WIDTH), y_dsa.reshape(m, WIDTH), x.reshape(m, d), p.reshape(m, -1),
                w_out[:WIDTH].astype(BF16), w_out[WIDTH:].astype(BF16), w_f1.astype(BF16), w_f2.astype(BF16),
                w_ple.astype(BF16), w_gate.astype(BF16), gains)
    return out.reshape(b, s, d)


def kernel(x, p, w_in, shift_mu, rwkv_w0, rwkv_w2, rwkv_a0, rwkv_a2, rwkv_g2, rwkv_k_k, rwkv_k_a, rwkv_r_k, rwkv_ln_g, rwkv_ln_b, idx_k_ln_g, idx_k_ln_b, rel_bias, w_out, w_ffn1, w_ffn2, w_ple, w_ple_gate, norm_mix_pre, norm_mix_post, norm_ffn_pre, norm_ffn_post, norm_ple_pre, norm_ple_post):
    for i in range(p.shape[0]):
        x = _layer(x, p[i], w_in[i], shift_mu[i], rwkv_w0[i], rwkv_w2[i], rwkv_a0[i], rwkv_a2[i], rwkv_g2[i],
                   rwkv_k_k[i], rwkv_k_a[i], rwkv_r_k[i], rwkv_ln_g[i], rwkv_ln_b[i], idx_k_ln_g[i],
                   idx_k_ln_b[i], rel_bias, w_out[i], w_ffn1[i], w_ffn2[i], w_ple[i], w_ple_gate[i],
                   norm_mix_pre[i], norm_mix_post[i], norm_ffn_pre[i], norm_ffn_post[i], norm_ple_pre[i],
                   norm_ple_post[i])
    return x
```

```python
import functools
import math

import jax
import jax.numpy as jnp
from jax import lax
from jax.experimental import pallas as pl
from jax.experimental.pallas import tpu as pltpu

F32 = jnp.float32
BF16 = jnp.bfloat16
I32 = jnp.int32

CHUNK = 64
HEADS = 8
HEAD_DIM = 64
WIDTH = HEADS * HEAD_DIM
DECAY_RANK = 64
A_RANK = 64
GATE_RANK = 128
GN_EPS = 64e-5
IDX_HEADS = 8
IDX_DIM = 64
MAX_TOPK = 256
REL_BUCKETS = 32
REL_MAX_DISTANCE = 128
NORM_EPS = 1e-6
LN_EPS = 1e-6
NEG_INF = -1e30
RWKV_COLS = 1792
N_MAIN = RWKV_COLS + 3 * WIDTH
N_IDX = 640

LANES = 128
KEY_TILE = 256
HEAD_PAIRS = WIDTH // LANES
GROUP = 4
FRONT_PAD = 2 * CHUNK
BACK_PAD = CHUNK
VMEM_LIMIT = 56 * 1024 * 1024

INT_MIN = -(2**31)
INT_MAX = 2**31 - 1


def _nt(a, b):
    return lax.dot_general(a, b, (((1,), (1,)), ((), ())), preferred_element_type=F32)


def _tn(a, b):
    return lax.dot_general(a, b, (((0,), (0,)), ((), ())), preferred_element_type=F32)


def _mm(a, b):
    return jnp.dot(a, b, preferred_element_type=F32)


def _split2(x):
    hi = x.astype(BF16)
    lo = (x - hi.astype(F32)).astype(BF16)
    return hi, lo


def _split3(x):
    hi = x.astype(BF16)
    r = x - hi.astype(F32)
    mid = r.astype(BF16)
    lo = (r - mid.astype(F32)).astype(BF16)
    return hi, mid, lo


def _rms(x, g):
    return x * lax.rsqrt(jnp.mean(x * x, axis=-1, keepdims=True) + NORM_EPS) * g


def _sigmoid(x):
    return 1.0 / (1.0 + jnp.exp(-x))


def _in_proj_kernel(x_ref, g_ref, wm_ref, wih_ref, wil_ref, zr_ref, q_ref, k_ref, v_ref, qi_ref, kiw_ref):
    h = _rms(x_ref[...], g_ref[...])
    hh, hl = _split2(h)
    zm = _mm(hh, wm_ref[...])
    zr_ref[...] = zm[:, :RWKV_COLS]
    q_ref[...] = zm[:, RWKV_COLS:RWKV_COLS + WIDTH].astype(BF16)
    k_ref[...] = zm[:, RWKV_COLS + WIDTH:RWKV_COLS + 2 * WIDTH].astype(BF16)
    v_ref[...] = zm[:, RWKV_COLS + 2 * WIDTH:].astype(BF16)
    zi = _mm(hh, wih_ref[...]) + _mm(hh, wil_ref[...]) + _mm(hl, wih_ref[...])
    qi_ref[...] = zi[:, :WIDTH]
    kiw_ref[...] = zi[:, WIDTH:]


def _in_proj(x2, g, wm, wih, wil, tm=256):
    m, d = x2.shape
    row = lambda n: pl.BlockSpec((tm, n), lambda i: (i, 0))
    full = lambda a: pl.BlockSpec(a.shape, lambda i: (0, 0))
    return pl.pallas_call(
        _in_proj_kernel,
        grid=(m // tm,),
        in_specs=[row(d), full(g), full(wm), full(wih), full(wil)],
        out_specs=[row(RWKV_COLS), row(WIDTH), row(WIDTH), row(WIDTH), row(WIDTH), row(LANES)],
        out_shape=[
            jax.ShapeDtypeStruct((m, RWKV_COLS), F32),
            jax.ShapeDtypeStruct((m, WIDTH), BF16),
            jax.ShapeDtypeStruct((m, WIDTH), BF16),
            jax.ShapeDtypeStruct((m, WIDTH), BF16),
            jax.ShapeDtypeStruct((m, WIDTH), F32),
            jax.ShapeDtypeStruct((m, LANES), F32),
        ],
        compiler_params=pltpu.CompilerParams(dimension_semantics=("arbitrary",), vmem_limit_bytes=VMEM_LIMIT),
        name="in_proj",
    )(x2, g, wm, wih, wil)


def _t5_bucket(rel):
    half = REL_BUCKETS // 2
    max_exact = half // 2
    ret = jnp.where(rel > 0, half, 0)
    n = jnp.abs(rel)
    nf = jnp.maximum(n, 1).astype(F32)
    large = max_exact + (jnp.log(nf / max_exact) / math.log(REL_MAX_DISTANCE / max_exact)
                         * (half - max_exact)).astype(I32)
    large = jnp.minimum(large, half - 1)
    return ret + jnp.where(n < max_exact, n, large)


def _bias_kernel(tab_ref, band_ref):
    r = lax.broadcasted_iota(I32, (CHUNK, KEY_TILE), 0)
    c = lax.broadcasted_iota(I32, (CHUNK, KEY_TILE), 1)
    near_bucket = _t5_bucket(c - FRONT_PAD - r)
    far_bucket = _t5_bucket(jnp.full((CHUNK, KEY_TILE), -(FRONT_PAD + 1), I32))
    for h in range(HEADS):
        near = jnp.zeros((CHUNK, KEY_TILE), F32)
        far = jnp.zeros((CHUNK, KEY_TILE), F32)
        for j in range(REL_BUCKETS):
            t = tab_ref[j, h]
            near = jnp.where(near_bucket == j, t, near)
            far = jnp.where(far_bucket == j, t, far)
        band_ref[h * CHUNK:(h + 1) * CHUNK, :] = near - far


def _bias_tiles(rel_bias):
    return pl.pallas_call(
        _bias_kernel,
        in_specs=[pl.BlockSpec(memory_space=pltpu.SMEM)],
        out_shape=jax.ShapeDtypeStruct((HEADS * CHUNK, KEY_TILE), F32),
        name="bias_tiles",
    )(rel_bias)


def _seg_sum(x):
    rows = x.shape[0]
    li = lax.broadcasted_iota(I32, (LANES, LANES), 0) // HEAD_DIM
    lj = lax.broadcasted_iota(I32, (LANES, LANES), 1) // HEAD_DIM
    ones = (li == lj).astype(BF16)
    xs = jnp.concatenate([x[:, g * LANES:(g + 1) * LANES] for g in range(HEAD_PAIRS)], axis=0)
    y = _mm(xs.astype(BF16), ones)
    return jnp.concatenate([y[g * rows:(g + 1) * rows] for g in range(HEAD_PAIRS)], axis=1)


def _rwkv_kernel(z_ref, mu_ref, w0_ref, wl_ref, a0_ref, g2_ref, kk_ref, ka_ref, rk_ref, lng_ref, lnb_ref,
                 y_ref, prev_s, st_s, *, nb):
    ci = pl.program_id(1)
    rows = nb * CHUNK

    @pl.when(ci == 0)
    def _():
        prev_s[...] = jnp.zeros_like(prev_s)
        st_s[...] = jnp.zeros_like(st_s)

    z = z_ref[...].reshape(rows, RWKV_COLS)
    row = lax.broadcasted_iota(I32, z.shape, 0)
    zp = pltpu.roll(z, 1, axis=0)
    for bb in range(nb):
        zp = jnp.where(row == bb * CHUNK, prev_s[bb], zp)
        prev_s[bb] = z[(bb + 1) * CHUNK - 1:(bb + 1) * CHUNK, :]
    zs = z + (zp - z) * mu_ref[...]
    r = zs[:, 0:WIDTH]
    k = zs[:, WIDTH:2 * WIDTH]
    v = zs[:, 2 * WIDTH:3 * WIDTH]
    zl = zs[:, 3 * WIDTH:3 * WIDTH + LANES]
    zg = zs[:, 3 * WIDTH + LANES:]

    lane = lax.broadcasted_iota(I32, zl.shape, 1)
    zl = jnp.where(lane < DECAY_RANK, jnp.tanh(zl), zl)
    la = _mm(zl.astype(BF16), wl_ref[...])
    y0 = -(w0_ref[...] + la[:, :WIDTH])
    softplus = jnp.maximum(y0, 0.0) + jnp.log(1.0 + jnp.exp(-jnp.abs(y0)))
    ld = -jnp.exp(-softplus - 0.5)
    a = _sigmoid(a0_ref[...] + la[:, WIDTH:])
    gate = _mm(_sigmoid(zg).astype(BF16), g2_ref[...])

    kk = k * kk_ref[...]
    kk = kk * jnp.minimum(lax.rsqrt(_seg_sum(kk * kk)), 1e12)
    kmod = k * (1.0 + (a - 1.0) * ka_ref[...])
    bvec = kk * a

    ti = lax.broadcasted_iota(I32, (rows, rows), 0)
    tj = lax.broadcasted_iota(I32, (rows, rows), 1)
    tri = ((ti >= tj) & (ti // CHUNK == tj // CHUNK)).astype(BF16)
    l1, l2, l3 = _split3(ld)
    cum = _mm(tri, l1) + _mm(tri, l2) + _mm(tri, l3)
    cum_x = cum - ld
    row_w = lax.broadcasted_iota(I32, (rows, WIDTH), 0)
    cref = cum[CHUNK // 2 - 1:CHUNK // 2, :]
    clast = cum[CHUNK - 1:CHUNK, :]
    for bb in range(1, nb):
        mid = bb * CHUNK + CHUNK // 2 - 1
        cref = jnp.where(row_w >= bb * CHUNK, cum[mid:mid + 1, :], cref)
        clast = jnp.where(row_w >= bb * CHUNK, cum[(bb + 1) * CHUNK - 1:(bb + 1) * CHUNK, :], clast)
    e_neg = jnp.exp(cref - cum)
    rt = r * jnp.exp(cum - cref)
    at = -kk * jnp.exp(cum_x - cref)
    kt = kmod * e_neg
    bt = bvec * e_neg
    r0 = r * jnp.exp(cum)
    a0v = -kk * jnp.exp(cum_x)
    e_end = jnp.exp(clast - cum)
    kc = kmod * e_end
    bc = bvec * e_end
    gw = GROUP * HEAD_DIM
    lane_h = lax.broadcasted_iota(I32, (CHUNK, gw), 1) // HEAD_DIM
    src = lax.broadcasted_iota(I32, (CHUNK, gw), 1) & (HEAD_DIM - 1)
    tok = lax.broadcasted_iota(I32, (CHUNK, gw), 0)
    strict = tok > src
    incl = tok >= src
    eye = (tok == src).astype(F32)
    diag_blk = (lax.broadcasted_iota(I32, (gw, gw), 0) // HEAD_DIM) == (lax.broadcasted_iota(I32, (gw, gw), 1) // HEAD_DIM)

    def bdiag(x):
        xb = x.astype(BF16)
        zero = jnp.zeros_like(xb)
        return jnp.concatenate([jnp.where(lane_h == h, xb, zero) for h in range(GROUP)], axis=0)

    units = [(bb, g, slice(bb * CHUNK, (bb + 1) * CHUNK), slice(g * gw, (g + 1) * gw))
             for bb in range(nb) for g in range(HEADS // GROUP)]
    cut = lambda x: [x[rs, gs] for _, _, rs, gs in units]
    at_w, rt_w, bt_w, kt_w, v_w = cut(at), cut(rt), cut(bt), cut(kt), cut(v)
    lhs = [jnp.concatenate([a_, r_], axis=0).astype(BF16) for a_, r_ in zip(at_w, rt_w)]
    ab_rb = [_nt(l_, bdiag(b_)) for l_, b_ in zip(lhs, bt_w)]
    ak_rk = [_nt(l_, bdiag(k_)) for l_, k_ in zip(lhs, kt_w)]
    low = [jnp.where(strict, x[:CHUNK], 0.0) for x in ab_rb]
    a_ak = [jnp.where(strict, x[:CHUNK], 0.0).astype(BF16) for x in ak_rk]
    a_rb = [jnp.where(incl, x[CHUNK:], 0.0).astype(BF16) for x in ab_rb]
    a_rk = [jnp.where(incl, x[CHUNK:], 0.0).astype(BF16) for x in ak_rk]
    v_bd = [bdiag(x) for x in v_w]
    s_f = [st_s[bb, g] for bb, g, _, _ in units]
    s_b = [s.astype(BF16) for s in s_f]
    x0 = [_nt(x.astype(BF16), s) for x, s in zip(cut(a0v), s_b)]
    o_st = [_nt(x.astype(BF16), s) for x, s in zip(cut(r0), s_b)]
    yk = [_mm(a_, vb) for a_, vb in zip(a_ak, v_bd)]
    t = [eye + x for x in low]
    p = low
    p_bd = [bdiag(x) for x in p]
    for _ in range(int(math.log2(CHUNK)) - 1):
        p = [_mm(x.astype(BF16), xb) for x, xb in zip(p, p_bd)]
        p_bd = [bdiag(x) for x in p]
        t = [t_ + _mm(t_.astype(BF16), xb) for t_, xb in zip(t, p_bd)]
    u = [_mm(t_.astype(BF16), bdiag(a_ + b_)) for t_, a_, b_ in zip(t, x0, yk)]
    o_w = [a_ + _mm(rb, bdiag(u_)) + _mm(rk, vb) for a_, rb, u_, rk, vb in zip(o_st, a_rb, u, a_rk, v_bd)]
    uv = [jnp.concatenate([u_, v_], axis=0).astype(BF16) for u_, v_ in zip(u, v_w)]
    bk = [jnp.concatenate([b_, k_], axis=0).astype(BF16) for b_, k_ in zip(cut(bc), cut(kc))]
    for (bb, gi, _, gs), s, w, x in zip(units, s_f, uv, bk):
        pc = jnp.exp(cum[(bb + 1) * CHUNK - 1:(bb + 1) * CHUNK, gs])
        st_s[bb, gi] = s * pc + jnp.where(diag_blk, _tn(w, x), 0.0)

    groups = HEADS // GROUP
    o = jnp.concatenate([jnp.concatenate(o_w[bb * groups:(bb + 1) * groups], axis=1) for bb in range(nb)], axis=0)
    inv_n = 1.0 / HEAD_DIM
    mean = _seg_sum(o) * inv_n
    d = o - mean
    var = _seg_sum(d * d) * inv_n
    on = d * lax.rsqrt(var + GN_EPS) * lng_ref[...] + lnb_ref[...]
    bonus = _seg_sum(r * kmod * rk_ref[...]) * v
    y_ref[...] = ((on + bonus) * gate).astype(BF16).reshape(nb, CHUNK, WIDTH)


def _rwkv(zr, mu, w0, wl, a0, g2, k_k, k_a, r_k, ln_g, ln_b, nb_max=4):
    b, s, _ = zr.shape
    nb = max(n for n in range(1, nb_max + 1) if b % n == 0)
    full = lambda a: pl.BlockSpec(a.shape, lambda i, j: (0,) * a.ndim)
    params = (mu, w0, wl, a0, g2, k_k, k_a, r_k, ln_g, ln_b)
    return pl.pallas_call(
        functools.partial(_rwkv_kernel, nb=nb),
        grid=(b // nb, s // CHUNK),
        in_specs=[pl.BlockSpec((nb, CHUNK, RWKV_COLS), lambda i, j: (i, j, 0))] + [full(a) for a in params],
        out_specs=pl.BlockSpec((nb, CHUNK, WIDTH), lambda i, j: (i, j, 0)),
        out_shape=jax.ShapeDtypeStruct((b, s, WIDTH), BF16),
        scratch_shapes=[
            pltpu.VMEM((nb, 1, RWKV_COLS), F32),
            pltpu.VMEM((nb, HEADS // GROUP, GROUP * HEAD_DIM, GROUP * HEAD_DIM), F32),
        ],
        compiler_params=pltpu.CompilerParams(dimension_semantics=("arbitrary", "arbitrary"),
                                             vmem_limit_bytes=VMEM_LIMIT),
        name="rwkv",
    )(zr, *params)


def _sortable(x):
    x = jnp.where(x == 0.0, 0.0, x)
    b = lax.bitcast_convert_type(x, I32)
    return b ^ ((b >> 31) & INT_MAX)


def _dsa_kernel(qi_ref, q_ref, kiw_ref, k_ref, v_ref, band_ref, lng_ref, lnb_ref, o_ref,
                ki3_s, lq_s, wb_s, key_s, thr_s, jlim_s, lg_s, mx_s, mb_s, sm_s, pv_s, *, topk, nb):
    i = pl.program_id(1)
    sp = kiw_ref.shape[1]
    pos_bits = int(sp).bit_length()
    n_far = (i + 1) // 4
    n_tiles = n_far + 1
    far_lim = CHUNK * i - FRONT_PAD
    lane1 = lax.broadcasted_iota(I32, (CHUNK, LANES), 1)
    halves = [slice(0, LANES), slice(LANES, KEY_TILE)]
    kf = float(topk)

    def tile_pos0(c):
        return jnp.where(c < n_far, c * KEY_TILE, far_lim)

    def tile_end(c):
        return jnp.where(c < n_far, far_lim, CHUNK * (i + 1))

    def tile_row0(c):
        return pl.multiple_of(FRONT_PAD + tile_pos0(c), CHUNK)

    def half_pos(c, half):
        return tile_pos0(c) + half * LANES + lane1

    @pl.when(i == 0)
    def _():
        rows = 320 if sp % 320 == 0 else CHUNK

        def body(t, carry):
            bb = t // (sp // rows)
            r0 = pl.multiple_of((t % (sp // rows)) * rows, CHUNK)
            ki = kiw_ref[bb, pl.ds(r0, rows), :][:, :IDX_DIM]
            mean = jnp.mean(ki, axis=-1, keepdims=True)
            var = jnp.mean(jnp.square(ki - mean), axis=-1, keepdims=True)
            kn = (ki - mean) * lax.rsqrt(var + LN_EPS) * lng_ref[...] + lnb_ref[...]
            hi, lo = _split2(kn)
            ki3_s[bb, pl.ds(r0, rows), :] = jnp.concatenate([hi, lo, hi], axis=1)
            return carry

        lax.fori_loop(0, nb * (sp // rows), body, 0)

    def stage_queries(bb, carry):
        qi = qi_ref[bb].reshape(IDX_HEADS * CHUNK, IDX_DIM)
        qh, ql = _split2(qi)
        lq_s[bb] = jnp.concatenate([qh, qh, ql], axis=1)
        wq = kiw_ref[bb, pl.ds(pl.multiple_of(FRONT_PAD + CHUNK * i, CHUNK), CHUNK), :]
        wscale = IDX_HEADS ** -0.5 * IDX_DIM ** -0.5
        wcol = jnp.concatenate([wq[:, IDX_DIM + h:IDX_DIM + h + 1] for h in range(IDX_HEADS)], axis=0) * wscale
        wb_s[bb] = jnp.broadcast_to(wcol, (IDX_HEADS * CHUNK, LANES))
        return carry

    lax.fori_loop(0, nb, stage_queries, 0)

    def score_tile(c, carry):
        r0 = tile_row0(c)
        end = tile_end(c)
        xs = [_nt(lq_s[bb], ki3_s[bb, pl.ds(r0, KEY_TILE), :]) for bb in range(nb)]
        for bb in range(nb):
            wb = wb_s[bb]
            for half, hs_ in enumerate(halves):
                act = jnp.maximum(xs[bb][:, hs_], 0.0) * wb
                sc = act[0:CHUNK]
                for h in range(1, IDX_HEADS):
                    sc = sc + act[h * CHUNK:(h + 1) * CHUNK]
                pos = half_pos(c, half)
                valid = (pos >= 0) & (pos < end)
                key_s[bb, c, :, hs_] = _sortable(jnp.where(valid, sc, -jnp.inf))
        return carry

    lax.fori_loop(0, n_tiles, score_tile, 0)

    thr_s[...] = jnp.full(thr_s.shape, INT_MIN, I32)
    jlim_s[...] = jnp.full(jlim_s.shape, INT_MAX, I32)

    def count_all(pred):
        def body(c, accs):
            out = []
            for bb in range(nb):
                a = accs[bb]
                for half, hs_ in enumerate(halves):
                    a = a + jnp.where(pred(bb, key_s[bb, c, :, hs_], half_pos(c, half)), 1.0, 0.0)
                out.append(a)
            return tuple(out)

        accs = lax.fori_loop(0, n_tiles, body, tuple(jnp.zeros((CHUNK, LANES), F32) for _ in range(nb)))
        return [jnp.sum(a, axis=1, keepdims=True) for a in accs]

    def count_ge(cand_sets):
        m = len(cand_sets)
        wide = [[jnp.broadcast_to(cd, (CHUNK, LANES)) for cd in cs] for cs in cand_sets]

        def body(c, accs):
            out = list(accs)
            for bb in range(nb):
                for hs_ in halves:
                    key = key_s[bb, c, :, hs_]
                    for j in range(m):
                        out[j * nb + bb] = out[j * nb + bb] + jnp.where(key >= wide[j][bb], 1.0, 0.0)
            return tuple(out)

        accs = lax.fori_loop(0, n_tiles, body, tuple(jnp.zeros((CHUNK, LANES), F32) for _ in range(m * nb)))
        return [[jnp.sum(accs[j * nb + bb], axis=1, keepdims=True) for bb in range(nb)] for j in range(m)]

    @pl.when(CHUNK * (i + 1) > topk)
    def _():
        n_pos, = count_ge([[jnp.zeros((CHUNK, 1), I32)] * nb])
        ans0 = tuple(jnp.where(n >= kf, 0, INT_MIN).astype(I32) for n in n_pos)

        def one_bit(t, ans):
            bit = lax.shift_left(jnp.int32(1), 30 - t)
            cands = [a | bit for a in ans]
            cnts, = count_ge([cands])
            return tuple(jnp.where(n >= kf, cd, a) for n, cd, a in zip(cnts, cands, ans))

        ans = lax.fori_loop(0, 31, one_bit, ans0)
        for bb in range(nb):
            thr_s[bb] = jnp.broadcast_to(ans[bb], (CHUNK, LANES))
        n_ge, = count_ge([list(ans)])
        most = n_ge[0]
        for bb in range(1, nb):
            most = jnp.maximum(most, n_ge[bb])

        @pl.when(jnp.max(most) > kf)
        def _():
            n_gt = count_all(lambda bb, key, pos: key > ans[bb])
            need = [kf - n for n in n_gt]

            def pos_body(t, xs):
                bit = lax.shift_left(jnp.int32(1), pos_bits - 1 - t)
                cands = [x | bit for x in xs]
                cnts = count_all(lambda bb, key, pos: (key == ans[bb]) & (pos < cands[bb]))
                return tuple(jnp.where(n < nd, cd, x) for n, nd, cd, x in zip(cnts, need, cands, xs))

            xs = lax.fori_loop(0, pos_bits, pos_body, tuple(jnp.zeros((CHUNK, 1), I32) for _ in range(nb)))
            for bb in range(nb):
                jlim_s[bb] = jnp.broadcast_to(xs[bb], (CHUNK, LANES))

    pair_lane_head = lane1 // HEAD_DIM

    def phase_c(bb, carry):
        qs = q_ref[bb] * jnp.asarray(HEAD_DIM ** -0.5, BF16)
        zero = jnp.zeros((CHUNK, LANES), BF16)
        qm = []
        for g in range(HEAD_PAIRS):
            qg = qs[:, g * LANES:(g + 1) * LANES]
            qm.append(jnp.concatenate([jnp.where(pair_lane_head == 0, qg, zero),
                                       jnp.where(pair_lane_head == 1, qg, zero)], axis=0))
        thr = thr_s[bb]
        jlim = jlim_s[bb]
        mx_s[...] = jnp.full(mx_s.shape, NEG_INF, F32)

        def logits(c, bias):
            kt = k_ref[bb, pl.ds(tile_row0(c), KEY_TILE), :]
            end = tile_end(c)
            sel = []
            for half, hs_ in enumerate(halves):
                key = key_s[bb, c, :, hs_]
                pos = half_pos(c, half)
                sel.append((pos >= 0) & (pos < end) & ((key > thr) | ((key == thr) & (pos <= jlim))))
            for g in range(HEAD_PAIRS):
                s = _nt(qm[g], kt[:, g * LANES:(g + 1) * LANES])
                if bias is not None:
                    s = s + bias[g * 2 * CHUNK:(g + 1) * 2 * CHUNK]
                for hh in range(2):
                    rows = slice((2 * g + hh) * CHUNK, (2 * g + hh + 1) * CHUNK)
                    sub = s[hh * CHUNK:(hh + 1) * CHUNK]
                    m0 = jnp.where(sel[0], sub[:, halves[0]], NEG_INF)
                    m1 = jnp.where(sel[1], sub[:, halves[1]], NEG_INF)
                    lg_s[c, rows, halves[0]] = m0
                    lg_s[c, rows, halves[1]] = m1
                    mx_s[rows] = jnp.maximum(mx_s[rows], jnp.maximum(m0, m1))

        def far_logits(c, carry):
            logits(c, None)
            return carry

        lax.fori_loop(0, n_far, far_logits, 0)
        logits(n_far, band_ref[...])

        mb_s[...] = jnp.broadcast_to(jnp.max(mx_s[...], axis=1, keepdims=True), mb_s.shape)
        sm_s[...] = jnp.zeros_like(sm_s)
        pv_s[...] = jnp.zeros_like(pv_s)

        def weigh(c, carry):
            vt = v_ref[bb, pl.ds(tile_row0(c), KEY_TILE), :]
            for g in range(HEAD_PAIRS):
                rows = slice(g * 2 * CHUNK, (g + 1) * 2 * CHUNK)
                m = mb_s[rows]
                p0 = jnp.exp(lg_s[c, rows, halves[0]] - m)
                p1 = jnp.exp(lg_s[c, rows, halves[1]] - m)
                sm_s[rows] = sm_s[rows] + (p0 + p1)
                p = jnp.concatenate([p0, p1], axis=1).astype(BF16)
                pv_s[rows] = pv_s[rows] + _mm(p, vt[:, g * LANES:(g + 1) * LANES])
            return carry

        lax.fori_loop(0, n_tiles, weigh, 0)

        outn = pv_s[...] / jnp.sum(sm_s[...], axis=1, keepdims=True)
        cols = []
        for g in range(HEAD_PAIRS):
            top = outn[(2 * g) * CHUNK:(2 * g + 1) * CHUNK]
            bot = outn[(2 * g + 1) * CHUNK:(2 * g + 2) * CHUNK]
            cols.append(jnp.where(pair_lane_head == 0, top, bot))
        o_ref[bb] = jnp.concatenate(cols, axis=1).astype(BF16)
        return carry

    lax.fori_loop(0, nb, phase_c, 0)


def _dsa(qi_t, q, kiw_p, k_p, v_p, band, ln_g, ln_b):
    b, s, _ = q.shape
    sp = k_p.shape[1]
    topk = min(MAX_TOPK, s // 4)
    max_tiles = (s // CHUNK) // 4 + 1
    nb = 4 if b % 4 == 0 else (2 if b % 2 == 0 else 1)
    rows = HEADS * CHUNK
    full = lambda a: pl.BlockSpec(a.shape, lambda i, j: (0,) * a.ndim)
    per_group = lambda n: pl.BlockSpec((nb, sp, n), lambda i, j: (i, 0, 0), pipeline_mode=pl.Buffered(1))
    return pl.pallas_call(
        functools.partial(_dsa_kernel, topk=topk, nb=nb),
        grid=(b // nb, s // CHUNK),
        in_specs=[
            pl.BlockSpec((nb, IDX_HEADS, CHUNK, IDX_DIM), lambda i, j: (i, 0, j, 0)),
            pl.BlockSpec((nb, CHUNK, WIDTH), lambda i, j: (i, j, 0)),
            per_group(LANES), per_group(WIDTH), per_group(WIDTH),
            full(band), full(ln_g), full(ln_b),
        ],
        out_specs=pl.BlockSpec((nb, CHUNK, WIDTH), lambda i, j: (i, j, 0)),
        out_shape=jax.ShapeDtypeStruct((b, s, WIDTH), BF16),
        scratch_shapes=[
            pltpu.VMEM((nb, sp, 3 * IDX_DIM), BF16),
            pltpu.VMEM((nb, rows, 3 * IDX_DIM), BF16),
            pltpu.VMEM((nb, rows, LANES), F32),
            pltpu.VMEM((nb, max_tiles, CHUNK, KEY_TILE), I32),
            pltpu.VMEM((nb, CHUNK, LANES), I32),
            pltpu.VMEM((nb, CHUNK, LANES), I32),
            pltpu.VMEM((max_tiles, rows, KEY_TILE), F32),
            pltpu.VMEM((rows, LANES), F32),
            pltpu.VMEM((rows, LANES), F32),
            pltpu.VMEM((rows, LANES), F32),
            pltpu.VMEM((rows, LANES), F32),
        ],
        compiler_params=pltpu.CompilerParams(dimension_semantics=("arbitrary", "arbitrary"),
                                             vmem_limit_bytes=VMEM_LIMIT),
        name="dsa",
    )(qi_t, q, kiw_p, k_p, v_p, band, ln_g, ln_b)


def _post_kernel(y1_ref, y2_ref, x_ref, p_ref, woa_ref, wob_ref, w1_ref, w2_ref, wp_ref, wg_ref,
                 g_mix_ref, g_f1_ref, g_f2_ref, g_p1_ref, g_p2_ref, o_ref):
    mix = _mm(y1_ref[...], woa_ref[...]) + _mm(y2_ref[...], wob_ref[...])
    x1 = x_ref[...] + _rms(mix, g_mix_ref[...])
    h = _rms(x1, g_f1_ref[...]).astype(BF16)
    hid = jnp.square(jnp.maximum(_mm(h, w1_ref[...]), 0.0)).astype(BF16)
    x2 = x1 + _rms(_mm(hid, w2_ref[...]), g_f2_ref[...])
    gate = _sigmoid(_mm(_rms(x2, g_p1_ref[...]).astype(BF16), wg_ref[...]))
    pe = _mm(p_ref[...].astype(BF16), wp_ref[...]) * gate
    o_ref[...] = x2 + _rms(pe, g_p2_ref[...])


def _post(y1, y2, x2d, p2d, woa, wob, w1, w2, wp, wg, gains, tm=256):
    m, d = x2d.shape
    row = lambda n: pl.BlockSpec((tm, n), lambda i: (i, 0))
    const = lambda a: pl.BlockSpec(a.shape, lambda i: (0, 0), pipeline_mode=pl.Buffered(1))
    weights = (woa, wob, w1, w2, wp, wg) + tuple(gains)
    return pl.pallas_call(
        _post_kernel,
        grid=(m // tm,),
        in_specs=[row(WIDTH), row(WIDTH), row(d), row(p2d.shape[1])] + [const(a) for a in weights],
        out_specs=row(d),
        out_shape=jax.ShapeDtypeStruct((m, d), F32),
        compiler_params=pltpu.CompilerParams(dimension_semantics=("arbitrary",), vmem_limit_bytes=VMEM_LIMIT),
        name="post",
    )(y1, y2, x2d, p2d, *weights)


def _layer(x, p, w_in, shift_mu, w0, w2r, a0, a2, g2, k_k, k_a, r_k, ln_g, ln_b, idx_g, idx_b, rel_bias,
           w_out, w_f1, w_f2, w_ple, w_gate, n_mix_pre, n_mix_post, n_ffn_pre, n_ffn_post, n_ple_pre, n_ple_post):
    b, s, d = x.shape
    m = b * s
    row = lambda a: a.reshape(1, -1).astype(F32)

    c = [0, WIDTH, WIDTH + DECAY_RANK, 2 * WIDTH + DECAY_RANK, 3 * WIDTH + DECAY_RANK,
         3 * WIDTH + DECAY_RANK + A_RANK, RWKV_COLS]
    perm = jnp.concatenate([jnp.arange(c[0], c[1]), jnp.arange(c[2], c[3]), jnp.arange(c[3], c[4]),
                            jnp.arange(c[1], c[2]), jnp.arange(c[4], c[5]), jnp.arange(c[5], c[6])])
    w_main = jnp.concatenate([w_in[:, :RWKV_COLS][:, perm], w_in[:, RWKV_COLS:RWKV_COLS + 3 * WIDTH]],
                             axis=1).astype(BF16)
    w_idx = w_in[:, RWKV_COLS + 3 * WIDTH:]
    w_idx = jnp.pad(w_idx, ((0, 0), (0, N_IDX - w_idx.shape[1])))
    w_idx_hi = w_idx.astype(BF16)
    w_idx_lo = (w_idx - w_idx_hi.astype(F32)).astype(BF16)

    zr, q, k, v, qi, kiw = _in_proj(x.reshape(m, d), row(n_mix_pre), w_main, w_idx_hi, w_idx_lo)

    zeros = jnp.zeros((DECAY_RANK, WIDTH), F32)
    w_lora = jnp.concatenate([jnp.concatenate([w2r, zeros], axis=1),
                              jnp.concatenate([zeros, a2], axis=1)], axis=0).astype(BF16)
    y_rwkv = _rwkv(zr.reshape(b, s, RWKV_COLS), row(shift_mu[perm]), row(w0), w_lora, row(a0), g2.astype(BF16),
                   row(k_k), row(k_a), row(r_k), row(ln_g), row(ln_b))

    band = _bias_tiles(rel_bias.astype(F32))
    pad = lambda a: jnp.pad(a.reshape(b, s, -1), ((0, 0), (FRONT_PAD, BACK_PAD), (0, 0)))
    qi_t = qi.reshape(b, s, IDX_HEADS, IDX_DIM).transpose(0, 2, 1, 3)
    y_dsa = _dsa(qi_t, q.reshape(b, s, WIDTH), pad(kiw), pad(k), pad(v), band, row(idx_g), row(idx_b))

    gains = tuple(row(a) for a in (n_mix_post, n_ffn_pre, n_ffn_post, n_ple_pre, n_ple_post))
    out = _post(y_rwkv.reshape(m, WIDTH), y_dsa.reshape(m, WIDTH), x.reshape(m, d), p.reshape(m, -1),
                w_out[:WIDTH].astype(BF16), w_out[WIDTH:].astype(BF16), w_f1.astype(BF16), w_f2.astype(BF16),
                w_ple.astype(BF16), w_gate.astype(BF16), gains)
    return out.reshape(b, s, d)


def kernel(x, p, w_in, shift_mu, rwkv_w0, rwkv_w2, rwkv_a0, rwkv_a2, rwkv_g2, rwkv_k_k, rwkv_k_a, rwkv_r_k, rwkv_ln_g, rwkv_ln_b, idx_k_ln_g, idx_k_ln_b, rel_bias, w_out, w_ffn1, w_ffn2, w_ple, w_ple_gate, norm_mix_pre, norm_mix_post, norm_ffn_pre, norm_ffn_post, norm_ple_pre, norm_ple_post):
    for i in range(p.shape[0]):
        x = _layer(x, p[i], w_in[i], shift_mu[i], rwkv_w0[i], rwkv_w2[i], rwkv_a0[i], rwkv_a2[i], rwkv_g2[i],
                   rwkv_k_k[i], rwkv_k_a[i], rwkv_r_k[i], rwkv_ln_g[i], rwkv_ln_b[i], idx_k_ln_g[i],
                   idx_k_ln_b[i], rel_bias, w_out[i], w_ffn1[i], w_ffn2[i], w_ple[i], w_ple_gate[i],
                   norm_mix_pre[i], norm_mix_post[i], norm_ffn_pre[i], norm_ffn_post[i], norm_ple_pre[i],
                   norm_ple_post[i])
    return x
```

```python
import functools
import math

import jax
import jax.numpy as jnp
from jax import lax
from jax.experimental import pallas as pl
from jax.experimental.pallas import tpu as pltpu

F32 = jnp.float32
BF16 = jnp.bfloat16
I32 = jnp.int32

CHUNK = 64
HEADS = 8
HEAD_DIM = 64
WIDTH = HEADS * HEAD_DIM
DECAY_RANK = 64
A_RANK = 64
GATE_RANK = 128
GN_EPS = 64e-5
IDX_HEADS = 8
IDX_DIM = 64
MAX_TOPK = 256
REL_BUCKETS = 32
REL_MAX_DISTANCE = 128
NORM_EPS = 1e-6
LN_EPS = 1e-6
NEG_INF = -1e30
RWKV_COLS = 1792
N_MAIN = RWKV_COLS + 3 * WIDTH
N_IDX = 640

LANES = 128
KEY_TILE = 256
HEAD_PAIRS = WIDTH // LANES
GROUP = 4
FRONT_PAD = 2 * CHUNK
BACK_PAD = CHUNK
VMEM_LIMIT = 56 * 1024 * 1024

INT_MIN = -(2**31)
INT_MAX = 2**31 - 1


def _nt(a, b):
    return lax.dot_general(a, b, (((1,), (1,)), ((), ())), preferred_element_type=F32)


def _tn(a, b):
    return lax.dot_general(a, b, (((0,), (0,)), ((), ())), preferred_element_type=F32)


def _mm(a, b):
    return jnp.dot(a, b, preferred_element_type=F32)


def _split2(x):
    hi = x.astype(BF16)
    lo = (x - hi.astype(F32)).astype(BF16)
    return hi, lo


def _split3(x):
    hi = x.astype(BF16)
    r = x - hi.astype(F32)
    mid = r.astype(BF16)
    lo = (r - mid.astype(F32)).astype(BF16)
    return hi, mid, lo


def _rms(x, g):
    return x * lax.rsqrt(jnp.mean(x * x, axis=-1, keepdims=True) + NORM_EPS) * g


def _sigmoid(x):
    return 1.0 / (1.0 + jnp.exp(-x))


def _in_proj_kernel(x_ref, g_ref, wm_ref, wih_ref, wil_ref, zr_ref, q_ref, k_ref, v_ref, qi_ref, kiw_ref):
    h = _rms(x_ref[...], g_ref[...])
    hh, hl = _split2(h)
    zm = _mm(hh, wm_ref[...])
    zr_ref[...] = zm[:, :RWKV_COLS]
    q_ref[...] = zm[:, RWKV_COLS:RWKV_COLS + WIDTH].astype(BF16)
    k_ref[...] = zm[:, RWKV_COLS + WIDTH:RWKV_COLS + 2 * WIDTH].astype(BF16)
    v_ref[...] = zm[:, RWKV_COLS + 2 * WIDTH:].astype(BF16)
    zi = _mm(hh, wih_ref[...]) + _mm(hh, wil_ref[...]) + _mm(hl, wih_ref[...])
    qi_ref[...] = zi[:, :WIDTH]
    kiw_ref[...] = zi[:, WIDTH:]


def _in_proj(x2, g, wm, wih, wil, tm=256):
    m, d = x2.shape
    row = lambda n: pl.BlockSpec((tm, n), lambda i: (i, 0))
    full = lambda a: pl.BlockSpec(a.shape, lambda i: (0, 0))
    return pl.pallas_call(
        _in_proj_kernel,
        grid=(m // tm,),
        in_specs=[row(d), full(g), full(wm), full(wih), full(wil)],
        out_specs=[row(RWKV_COLS), row(WIDTH), row(WIDTH), row(WIDTH), row(WIDTH), row(LANES)],
        out_shape=[
            jax.ShapeDtypeStruct((m, RWKV_COLS), F32),
            jax.ShapeDtypeStruct((m, WIDTH), BF16),
            jax.ShapeDtypeStruct((m, WIDTH), BF16),
            jax.ShapeDtypeStruct((m, WIDTH), BF16),
            jax.ShapeDtypeStruct((m, WIDTH), F32),
            jax.ShapeDtypeStruct((m, LANES), F32),
        ],
        compiler_params=pltpu.CompilerParams(dimension_semantics=("arbitrary",), vmem_limit_bytes=VMEM_LIMIT),
        name="in_proj",
    )(x2, g, wm, wih, wil)


def _t5_bucket(rel):
    half = REL_BUCKETS // 2
    max_exact = half // 2
    ret = jnp.where(rel > 0, half, 0)
    n = jnp.abs(rel)
    nf = jnp.maximum(n, 1).astype(F32)
    steps = jnp.log(nf / max_exact) / math.log(REL_MAX_DISTANCE / max_exact) * (half - max_exact)
    large = max_exact
    for level in range(1, half - max_exact):
        large = large + jnp.where(steps >= level - 1e-4, 1, 0)
    return ret + jnp.where(n < max_exact, n, large)


def _bias_kernel(tab_ref, band_ref):
    r = lax.broadcasted_iota(I32, (CHUNK, KEY_TILE), 0)
    c = lax.broadcasted_iota(I32, (CHUNK, KEY_TILE), 1)
    near_bucket = _t5_bucket(c - FRONT_PAD - r)
    far_bucket = _t5_bucket(jnp.full((CHUNK, KEY_TILE), -(FRONT_PAD + 1), I32))
    for h in range(HEADS):
        near = jnp.zeros((CHUNK, KEY_TILE), F32)
        far = jnp.zeros((CHUNK, KEY_TILE), F32)
        for j in range(REL_BUCKETS):
            t = tab_ref[j, h]
            near = jnp.where(near_bucket == j, t, near)
            far = jnp.where(far_bucket == j, t, far)
        band_ref[h * CHUNK:(h + 1) * CHUNK, :] = near - far


def _bias_tiles(rel_bias):
    return pl.pallas_call(
        _bias_kernel,
        in_specs=[pl.BlockSpec(memory_space=pltpu.SMEM)],
        out_shape=jax.ShapeDtypeStruct((HEADS * CHUNK, KEY_TILE), F32),
        name="bias_tiles",
    )(rel_bias)


def _seg_sum(x):
    rows = x.shape[0]
    li = lax.broadcasted_iota(I32, (LANES, LANES), 0) // HEAD_DIM
    lj = lax.broadcasted_iota(I32, (LANES, LANES), 1) // HEAD_DIM
    ones = (li == lj).astype(BF16)
    xs = jnp.concatenate([x[:, g * LANES:(g + 1) * LANES] for g in range(HEAD_PAIRS)], axis=0)
    y = _mm(xs.astype(BF16), ones)
    return jnp.concatenate([y[g * rows:(g + 1) * rows] for g in range(HEAD_PAIRS)], axis=1)


def _rwkv_kernel(z_ref, mu_ref, w0_ref, wl_ref, a0_ref, g2_ref, kk_ref, ka_ref, rk_ref, lng_ref, lnb_ref,
                 y_ref, prev_s, st_s, *, nb):
    ci = pl.program_id(1)
    rows = nb * CHUNK

    @pl.when(ci == 0)
    def _():
        prev_s[...] = jnp.zeros_like(prev_s)
        st_s[...] = jnp.zeros_like(st_s)

    z = z_ref[...].reshape(rows, RWKV_COLS)
    row = lax.broadcasted_iota(I32, z.shape, 0)
    zp = pltpu.roll(z, 1, axis=0)
    for bb in range(nb):
        zp = jnp.where(row == bb * CHUNK, prev_s[bb], zp)
        prev_s[bb] = z[(bb + 1) * CHUNK - 1:(bb + 1) * CHUNK, :]
    zs = z + (zp - z) * mu_ref[...]
    r = zs[:, 0:WIDTH]
    k = zs[:, WIDTH:2 * WIDTH]
    v = zs[:, 2 * WIDTH:3 * WIDTH]
    zl = zs[:, 3 * WIDTH:3 * WIDTH + LANES]
    zg = zs[:, 3 * WIDTH + LANES:]

    lane = lax.broadcasted_iota(I32, zl.shape, 1)
    zl = jnp.where(lane < DECAY_RANK, jnp.tanh(zl), zl)
    la = _mm(zl.astype(BF16), wl_ref[...])
    y0 = -(w0_ref[...] + la[:, :WIDTH])
    softplus = jnp.maximum(y0, 0.0) + jnp.log(1.0 + jnp.exp(-jnp.abs(y0)))
    ld = -jnp.exp(-softplus - 0.5)
    a = _sigmoid(a0_ref[...] + la[:, WIDTH:])
    gate = _mm(_sigmoid(zg).astype(BF16), g2_ref[...])

    kk = k * kk_ref[...]
    kk = kk * jnp.minimum(lax.rsqrt(_seg_sum(kk * kk)), 1e12)
    kmod = k * (1.0 + (a - 1.0) * ka_ref[...])
    bvec = kk * a

    ti = lax.broadcasted_iota(I32, (rows, rows), 0)
    tj = lax.broadcasted_iota(I32, (rows, rows), 1)
    tri = ((ti >= tj) & (ti // CHUNK == tj // CHUNK)).astype(BF16)
    l1, l2, l3 = _split3(ld)
    cum = _mm(tri, l1) + _mm(tri, l2) + _mm(tri, l3)
    cum_x = cum - ld
    row_w = lax.broadcasted_iota(I32, (rows, WIDTH), 0)
    cref = cum[CHUNK // 2 - 1:CHUNK // 2, :]
    clast = cum[CHUNK - 1:CHUNK, :]
    for bb in range(1, nb):
        mid = bb * CHUNK + CHUNK // 2 - 1
        cref = jnp.where(row_w >= bb * CHUNK, cum[mid:mid + 1, :], cref)
        clast = jnp.where(row_w >= bb * CHUNK, cum[(bb + 1) * CHUNK - 1:(bb + 1) * CHUNK, :], clast)
    e_neg = jnp.exp(cref - cum)
    rt = r * jnp.exp(cum - cref)
    at = -kk * jnp.exp(cum_x - cref)
    kt = kmod * e_neg
    bt = bvec * e_neg
    r0 = r * jnp.exp(cum)
    a0v = -kk * jnp.exp(cum_x)
    e_end = jnp.exp(clast - cum)
    kc = kmod * e_end
    bc = bvec * e_end
    gw = GROUP * HEAD_DIM
    lane_h = lax.broadcasted_iota(I32, (CHUNK, gw), 1) // HEAD_DIM
    src = lax.broadcasted_iota(I32, (CHUNK, gw), 1) & (HEAD_DIM - 1)
    tok = lax.broadcasted_iota(I32, (CHUNK, gw), 0)
    strict = tok > src
    incl = tok >= src
    eye = (tok == src).astype(F32)
    diag_blk = (lax.broadcasted_iota(I32, (gw, gw), 0) // HEAD_DIM) == (lax.broadcasted_iota(I32, (gw, gw), 1) // HEAD_DIM)

    def bdiag(x):
        xb = x.astype(BF16)
        zero = jnp.zeros_like(xb)
        return jnp.concatenate([jnp.where(lane_h == h, xb, zero) for h in range(GROUP)], axis=0)

    units = [(bb, g, slice(bb * CHUNK, (bb + 1) * CHUNK), slice(g * gw, (g + 1) * gw))
             for bb in range(nb) for g in range(HEADS // GROUP)]
    cut = lambda x: [x[rs, gs] for _, _, rs, gs in units]
    at_w, rt_w, bt_w, kt_w, v_w = cut(at), cut(rt), cut(bt), cut(kt), cut(v)
    lhs = [jnp.concatenate([a_, r_], axis=0).astype(BF16) for a_, r_ in zip(at_w, rt_w)]
    ab_rb = [_nt(l_, bdiag(b_)) for l_, b_ in zip(lhs, bt_w)]
    ak_rk = [_nt(l_, bdiag(k_)) for l_, k_ in zip(lhs, kt_w)]
    low = [jnp.where(strict, x[:CHUNK], 0.0) for x in ab_rb]
    a_ak = [jnp.where(strict, x[:CHUNK], 0.0).astype(BF16) for x in ak_rk]
    a_rb = [jnp.where(incl, x[CHUNK:], 0.0).astype(BF16) for x in ab_rb]
    a_rk = [jnp.where(incl, x[CHUNK:], 0.0).astype(BF16) for x in ak_rk]
    v_bd = [bdiag(x) for x in v_w]
    s_f = [st_s[bb, g] for bb, g, _, _ in units]
    s_b = [s.astype(BF16) for s in s_f]
    x0 = [_nt(x.astype(BF16), s) for x, s in zip(cut(a0v), s_b)]
    o_st = [_nt(x.astype(BF16), s) for x, s in zip(cut(r0), s_b)]
    yk = [_mm(a_, vb) for a_, vb in zip(a_ak, v_bd)]
    t = [eye + x for x in low]
    p = low
    p_bd = [bdiag(x) for x in p]
    for _ in range(int(math.log2(CHUNK)) - 1):
        p = [_mm(x.astype(BF16), xb) for x, xb in zip(p, p_bd)]
        p_bd = [bdiag(x) for x in p]
        t = [t_ + _mm(t_.astype(BF16), xb) for t_, xb in zip(t, p_bd)]
    u = [_mm(t_.astype(BF16), bdiag(a_ + b_)) for t_, a_, b_ in zip(t, x0, yk)]
    o_w = [a_ + _mm(rb, bdiag(u_)) + _mm(rk, vb) for a_, rb, u_, rk, vb in zip(o_st, a_rb, u, a_rk, v_bd)]
    uv = [jnp.concatenate([u_, v_], axis=0).astype(BF16) for u_, v_ in zip(u, v_w)]
    bk = [jnp.concatenate([b_, k_], axis=0).astype(BF16) for b_, k_ in zip(cut(bc), cut(kc))]
    for (bb, gi, _, gs), s, w, x in zip(units, s_f, uv, bk):
        pc = jnp.exp(cum[(bb + 1) * CHUNK - 1:(bb + 1) * CHUNK, gs])
        st_s[bb, gi] = s * pc + jnp.where(diag_blk, _tn(w, x), 0.0)

    groups = HEADS // GROUP
    o = jnp.concatenate([jnp.concatenate(o_w[bb * groups:(bb + 1) * groups], axis=1) for bb in range(nb)], axis=0)
    inv_n = 1.0 / HEAD_DIM
    mean = _seg_sum(o) * inv_n
    d = o - mean
    var = _seg_sum(d * d) * inv_n
    on = d * lax.rsqrt(var + GN_EPS) * lng_ref[...] + lnb_ref[...]
    bonus = _seg_sum(r * kmod * rk_ref[...]) * v
    y_ref[...] = ((on + bonus) * gate).astype(BF16).reshape(nb, CHUNK, WIDTH)


def _rwkv(zr, mu, w0, wl, a0, g2, k_k, k_a, r_k, ln_g, ln_b, nb_max=4):
    b, s, _ = zr.shape
    nb = max(n for n in range(1, nb_max + 1) if b % n == 0)
    full = lambda a: pl.BlockSpec(a.shape, lambda i, j: (0,) * a.ndim)
    params = (mu, w0, wl, a0, g2, k_k, k_a, r_k, ln_g, ln_b)
    return pl.pallas_call(
        functools.partial(_rwkv_kernel, nb=nb),
        grid=(b // nb, s // CHUNK),
        in_specs=[pl.BlockSpec((nb, CHUNK, RWKV_COLS), lambda i, j: (i, j, 0))] + [full(a) for a in params],
        out_specs=pl.BlockSpec((nb, CHUNK, WIDTH), lambda i, j: (i, j, 0)),
        out_shape=jax.ShapeDtypeStruct((b, s, WIDTH), BF16),
        scratch_shapes=[
            pltpu.VMEM((nb, 1, RWKV_COLS), F32),
            pltpu.VMEM((nb, HEADS // GROUP, GROUP * HEAD_DIM, GROUP * HEAD_DIM), F32),
        ],
        compiler_params=pltpu.CompilerParams(dimension_semantics=("arbitrary", "arbitrary"),
                                             vmem_limit_bytes=VMEM_LIMIT),
        name="rwkv",
    )(zr, *params)


def _key_to_float(key):
    return lax.bitcast_convert_type(key ^ ((key >> 31) & INT_MAX), F32)


def _dsa_kernel(qi_ref, q_ref, kiw_ref, k_ref, v_ref, band_ref, lng_ref, lnb_ref, o_ref,
                ki3_s, lq_s, wb_s, key_s, keyt_s, thr_s, jlim_s, lg_s, mx_s, mb_s, sm_s, pv_s, *, topk, nb):
    i = pl.program_id(1)
    sp = kiw_ref.shape[1]
    n_far = (i + 1) // 4
    n_tiles = n_far + 1
    far_lim = CHUNK * i - FRONT_PAD
    lane1 = lax.broadcasted_iota(I32, (CHUNK, LANES), 1)
    halves = [slice(0, LANES), slice(LANES, KEY_TILE)]
    n_pairs = (nb + 1) // 2
    kf = float(topk)

    def tile_pos0(c):
        return jnp.where(c < n_far, c * KEY_TILE, far_lim)

    def tile_end(c):
        return jnp.where(c < n_far, far_lim, CHUNK * (i + 1))

    def tile_row0(c):
        return pl.multiple_of(FRONT_PAD + tile_pos0(c), CHUNK)

    def half_pos(c, half):
        return tile_pos0(c) + half * LANES + lane1

    @pl.when(i == 0)
    def _():
        rows = 320 if sp % 320 == 0 else CHUNK

        def body(t, carry):
            bb = t // (sp // rows)
            r0 = pl.multiple_of((t % (sp // rows)) * rows, CHUNK)
            ki = kiw_ref[bb, pl.ds(r0, rows), :][:, :IDX_DIM]
            mean = jnp.mean(ki, axis=-1, keepdims=True)
            var = jnp.mean(jnp.square(ki - mean), axis=-1, keepdims=True)
            kn = (ki - mean) * lax.rsqrt(var + LN_EPS) * lng_ref[...] + lnb_ref[...]
            hi, lo = _split2(kn)
            ki3_s[bb, pl.ds(r0, rows), :] = jnp.concatenate([hi, lo, hi], axis=1)
            return carry

        lax.fori_loop(0, nb * (sp // rows), body, 0)

    def stage_queries(bb, carry):
        qi = qi_ref[bb].reshape(IDX_HEADS * CHUNK, IDX_DIM)
        qh, ql = _split2(qi)
        lq_s[bb] = jnp.concatenate([qh, qh, ql], axis=1)
        wq = kiw_ref[bb, pl.ds(pl.multiple_of(FRONT_PAD + CHUNK * i, CHUNK), CHUNK), :]
        wscale = IDX_HEADS ** -0.5 * IDX_DIM ** -0.5
        wcol = jnp.concatenate([wq[:, IDX_DIM + h:IDX_DIM + h + 1] for h in range(IDX_HEADS)], axis=0) * wscale
        wb_s[bb] = jnp.broadcast_to(wcol, (IDX_HEADS * CHUNK, LANES))
        return carry

    lax.fori_loop(0, nb, stage_queries, 0)

    def score_tile(c, carry):
        r0 = tile_row0(c)
        end = tile_end(c)
        xs = [_nt(lq_s[bb], ki3_s[bb, pl.ds(r0, KEY_TILE), :]) for bb in range(nb)]
        keys = []
        for bb in range(nb):
            wb = wb_s[bb]
            parts = []
            for half, hs_ in enumerate(halves):
                act = jnp.maximum(xs[bb][:, hs_], 0.0) * wb
                sc = act[0:CHUNK]
                for h in range(1, IDX_HEADS):
                    sc = sc + act[h * CHUNK:(h + 1) * CHUNK]
                pos = half_pos(c, half)
                valid = (pos >= 0) & (pos < end)
                parts.append(jnp.where(valid, sc, -jnp.inf))
            keys.append(jnp.concatenate(parts, axis=1))
            key_s[bb, c] = keys[bb]
        for pr in range(n_pairs):
            both = jnp.concatenate([keys[2 * pr], keys[min(2 * pr + 1, nb - 1)]], axis=0)
            keyt_s[pr, c] = both.T
        return carry

    lax.fori_loop(0, n_tiles, score_tile, 0)

    @pl.when(n_tiles % 2 == 1)
    def _():
        for pr in range(n_pairs):
            keyt_s[pr, n_tiles] = jnp.full((KEY_TILE, 2 * CHUNK), -jnp.inf, F32)

    thr_s[...] = jnp.full(thr_s.shape, -jnp.inf, F32)
    jlim_s[...] = jnp.full(jlim_s.shape, INT_MAX, I32)

    part = 32

    def count(cands, above):
        wide = [jnp.broadcast_to(cd, (part, LANES)) for cd in cands]

        def body(c2, accs):
            out = list(accs)
            for c in (2 * c2, 2 * c2 + 1):
                for pr in range(n_pairs):
                    sc = keyt_s[pr, c].reshape(KEY_TILE // part, part, LANES)
                    out[pr] = out[pr] + jnp.sum(jnp.where(above(sc, wide[pr]), 1.0, 0.0), axis=0)
            return tuple(out)

        accs = lax.fori_loop(0, (n_tiles + 1) // 2, body,
                             tuple(jnp.zeros((part, LANES), F32) for _ in range(n_pairs)))
        return [jnp.sum(a, axis=0, keepdims=True) for a in accs]

    def count_ge(keys):
        return count([_key_to_float(k) for k in keys], lambda s, c: s >= c)

    @pl.when(CHUNK * (i + 1) > topk)
    def _():
        n_all = float(KEY_TILE) * n_tiles.astype(F32)
        n_pos = count_ge([jnp.zeros((1, LANES), I32)] * n_pairs)
        ans0 = tuple(jnp.where(n >= kf, 0, INT_MIN).astype(I32) for n in n_pos)
        cnt0 = tuple(jnp.where(n >= kf, n, n_all) for n in n_pos)

        def one_bit(t, carry):
            ans, n_ans = carry
            bit = lax.shift_left(jnp.int32(1), 30 - t)
            cands = [a | bit for a in ans]
            cnts = count_ge(cands)
            take = [n >= kf for n in cnts]
            return (tuple(jnp.where(tk, cd, a) for tk, cd, a in zip(take, cands, ans)),
                    tuple(jnp.where(tk, n, na) for tk, n, na in zip(take, cnts, n_ans)))

        ans, n_ans = lax.fori_loop(0, 31, one_bit, (ans0, cnt0))
        def to_columns(rows_, dst):
            for pr in range(n_pairs):
                col = jnp.broadcast_to(rows_[pr], (LANES, LANES)).T
                dst[2 * pr] = col[:CHUNK]
                if 2 * pr + 1 < nb:
                    dst[2 * pr + 1] = col[CHUNK:]

        kth = [_key_to_float(a) for a in ans]
        to_columns(kth, thr_s)
        most = n_ans[0]
        for pr in range(1, n_pairs):
            most = jnp.maximum(most, n_ans[pr])

        @pl.when(jnp.max(most) > kf)
        def _():
            n_gt = count(kth, lambda s, c: s > c)
            need = [kf - n for n in n_gt]
            ans_w = [jnp.broadcast_to(a, (KEY_TILE, LANES)) for a in kth]
            krow = lax.broadcasted_iota(I32, (KEY_TILE, LANES), 0)
            tri = (lax.broadcasted_iota(I32, (KEY_TILE, KEY_TILE), 0)
                   >= lax.broadcasted_iota(I32, (KEY_TILE, KEY_TILE), 1)).astype(BF16)

            def body(c, carry):
                seen, last = carry
                pos = tile_pos0(c) + krow
                seen_out, last_out = [], []
                for pr in range(n_pairs):
                    tie = keyt_s[pr, c] == ans_w[pr]
                    rank = _mm(tri, jnp.where(tie, 1.0, 0.0).astype(BF16)) + seen[pr]
                    kept = jnp.where(tie & (rank <= need[pr]), pos, -1)
                    last_out.append(jnp.maximum(last[pr], jnp.max(kept, axis=0, keepdims=True)))
                    seen_out.append(rank[KEY_TILE - 1:KEY_TILE, :])
                return tuple(seen_out), tuple(last_out)

            init = (tuple(jnp.zeros((1, LANES), F32) for _ in range(n_pairs)),
                    tuple(jnp.full((1, LANES), -1, I32) for _ in range(n_pairs)))
            _, last = lax.fori_loop(0, n_tiles, body, init)
            to_columns(last, jlim_s)

    pair_lane_head = lane1 // HEAD_DIM

    def phase_c(bb, carry):
        qs = q_ref[bb] * jnp.asarray(HEAD_DIM ** -0.5, BF16)
        zero = jnp.zeros((CHUNK, LANES), BF16)
        qm = []
        for g in range(HEAD_PAIRS):
            qg = qs[:, g * LANES:(g + 1) * LANES]
            qm.append(jnp.concatenate([jnp.where(pair_lane_head == 0, qg, zero),
                                       jnp.where(pair_lane_head == 1, qg, zero)], axis=0))
        thr = thr_s[bb]
        jlim = jlim_s[bb]
        mx_s[...] = jnp.full(mx_s.shape, NEG_INF, F32)

        def logits(c, bias):
            kt = k_ref[bb, pl.ds(tile_row0(c), KEY_TILE), :]
            end = tile_end(c)
            sel = []
            for half, hs_ in enumerate(halves):
                key = key_s[bb, c, :, hs_]
                pos = half_pos(c, half)
                sel.append((pos >= 0) & (pos < end) & ((key > thr) | ((key == thr) & (pos <= jlim))))
            for g in range(HEAD_PAIRS):
                s = _nt(qm[g], kt[:, g * LANES:(g + 1) * LANES])
                if bias is not None:
                    s = s + bias[g * 2 * CHUNK:(g + 1) * 2 * CHUNK]
                for hh in range(2):
                    rows = slice((2 * g + hh) * CHUNK, (2 * g + hh + 1) * CHUNK)
                    sub = s[hh * CHUNK:(hh + 1) * CHUNK]
                    m0 = jnp.where(sel[0], sub[:, halves[0]], NEG_INF)
                    m1 = jnp.where(sel[1], sub[:, halves[1]], NEG_INF)
                    lg_s[c, rows, halves[0]] = m0
                    lg_s[c, rows, halves[1]] = m1
                    mx_s[rows] = jnp.maximum(mx_s[rows], jnp.maximum(m0, m1))

        def far_logits(c, carry):
            logits(c, None)
            return carry

        lax.fori_loop(0, n_far, far_logits, 0)
        logits(n_far, band_ref[...])

        mb_s[...] = jnp.broadcast_to(jnp.max(mx_s[...], axis=1, keepdims=True), mb_s.shape)
        sm_s[...] = jnp.zeros_like(sm_s)
        pv_s[...] = jnp.zeros_like(pv_s)

        def weigh(c, carry):
            vt = v_ref[bb, pl.ds(tile_row0(c), KEY_TILE), :]
            for g in range(HEAD_PAIRS):
                rows = slice(g * 2 * CHUNK, (g + 1) * 2 * CHUNK)
                m = mb_s[rows]
                p0 = jnp.exp(lg_s[c, rows, halves[0]] - m)
                p1 = jnp.exp(lg_s[c, rows, halves[1]] - m)
                sm_s[rows] = sm_s[rows] + (p0 + p1)
                p = jnp.concatenate([p0, p1], axis=1).astype(BF16)
                pv_s[rows] = pv_s[rows] + _mm(p, vt[:, g * LANES:(g + 1) * LANES])
            return carry

        lax.fori_loop(0, n_tiles, weigh, 0)

        outn = pv_s[...] / jnp.sum(sm_s[...], axis=1, keepdims=True)
        cols = []
        for g in range(HEAD_PAIRS):
            top = outn[(2 * g) * CHUNK:(2 * g + 1) * CHUNK]
            bot = outn[(2 * g + 1) * CHUNK:(2 * g + 2) * CHUNK]
            cols.append(jnp.where(pair_lane_head == 0, top, bot))
        o_ref[bb] = jnp.concatenate(cols, axis=1).astype(BF16)
        return carry

    lax.fori_loop(0, nb, phase_c, 0)


def _dsa(qi_t, q, kiw_p, k_p, v_p, band, ln_g, ln_b):
    b, s, _ = q.shape
    sp = k_p.shape[1]
    topk = min(MAX_TOPK, s // 4)
    max_tiles = (s // CHUNK) // 4 + 1
    nb = 4 if b % 4 == 0 else (2 if b % 2 == 0 else 1)
    rows = HEADS * CHUNK
    full = lambda a: pl.BlockSpec(a.shape, lambda i, j: (0,) * a.ndim)
    per_group = lambda n: pl.BlockSpec((nb, sp, n), lambda i, j: (i, 0, 0), pipeline_mode=pl.Buffered(1))
    return pl.pallas_call(
        functools.partial(_dsa_kernel, topk=topk, nb=nb),
        grid=(b // nb, s // CHUNK),
        in_specs=[
            pl.BlockSpec((nb, IDX_HEADS, CHUNK, IDX_DIM), lambda i, j: (i, 0, j, 0)),
            pl.BlockSpec((nb, CHUNK, WIDTH), lambda i, j: (i, j, 0)),
            per_group(LANES), per_group(WIDTH), per_group(WIDTH),
            full(band), full(ln_g), full(ln_b),
        ],
        out_specs=pl.BlockSpec((nb, CHUNK, WIDTH), lambda i, j: (i, j, 0)),
        out_shape=jax.ShapeDtypeStruct((b, s, WIDTH), BF16),
        scratch_shapes=[
            pltpu.VMEM((nb, sp, 3 * IDX_DIM), BF16),
            pltpu.VMEM((nb, rows, 3 * IDX_DIM), BF16),
            pltpu.VMEM((nb, rows, LANES), F32),
            pltpu.VMEM((nb, max_tiles, CHUNK, KEY_TILE), F32),
            pltpu.VMEM(((nb + 1) // 2, max_tiles + 1, KEY_TILE, 2 * CHUNK), F32),
            pltpu.VMEM((nb, CHUNK, LANES), F32),
            pltpu.VMEM((nb, CHUNK, LANES), I32),
            pltpu.VMEM((max_tiles, rows, KEY_TILE), F32),
            pltpu.VMEM((rows, LANES), F32),
            pltpu.VMEM((rows, LANES), F32),
            pltpu.VMEM((rows, LANES), F32),
            pltpu.VMEM((rows, LANES), F32),
        ],
        compiler_params=pltpu.CompilerParams(dimension_semantics=("arbitrary", "arbitrary"),
                                             vmem_limit_bytes=VMEM_LIMIT),
        name="dsa",
    )(qi_t, q, kiw_p, k_p, v_p, band, ln_g, ln_b)


def _post_kernel(y1_ref, y2_ref, x_ref, p_ref, woa_ref, wob_ref, w1_ref, w2_ref, wp_ref, wg_ref,
                 g_mix_ref, g_f1_ref, g_f2_ref, g_p1_ref, g_p2_ref, o_ref):
    mix = _mm(y1_ref[...], woa_ref[...]) + _mm(y2_ref[...], wob_ref[...])
    x1 = x_ref[...] + _rms(mix, g_mix_ref[...])
    h = _rms(x1, g_f1_ref[...]).astype(BF16)
    hid = jnp.square(jnp.maximum(_mm(h, w1_ref[...]), 0.0)).astype(BF16)
    x2 = x1 + _rms(_mm(hid, w2_ref[...]), g_f2_ref[...])
    gate = _sigmoid(_mm(_rms(x2, g_p1_ref[...]).astype(BF16), wg_ref[...]))
    pe = _mm(p_ref[...].astype(BF16), wp_ref[...]) * gate
    o_ref[...] = x2 + _rms(pe, g_p2_ref[...])


def _post(y1, y2, x2d, p2d, woa, wob, w1, w2, wp, wg, gains, tm=256):
    m, d = x2d.shape
    row = lambda n: pl.BlockSpec((tm, n), lambda i: (i, 0))
    const = lambda a: pl.BlockSpec(a.shape, lambda i: (0, 0), pipeline_mode=pl.Buffered(1))
    weights = (woa, wob, w1, w2, wp, wg) + tuple(gains)
    return pl.pallas_call(
        _post_kernel,
        grid=(m // tm,),
        in_specs=[row(WIDTH), row(WIDTH), row(d), row(p2d.shape[1])] + [const(a) for a in weights],
        out_specs=row(d),
        out_shape=jax.ShapeDtypeStruct((m, d), F32),
        compiler_params=pltpu.CompilerParams(dimension_semantics=("arbitrary",), vmem_limit_bytes=VMEM_LIMIT),
        name="post",
    )(y1, y2, x2d, p2d, *weights)


def _layer(x, p, w_in, shift_mu, w0, w2r, a0, a2, g2, k_k, k_a, r_k, ln_g, ln_b, idx_g, idx_b, rel_bias,
           w_out, w_f1, w_f2, w_ple, w_gate, n_mix_pre, n_mix_post, n_ffn_pre, n_ffn_post, n_ple_pre, n_ple_post):
    b, s, d = x.shape
    m = b * s
    row = lambda a: a.reshape(1, -1).astype(F32)

    c = [0, WIDTH, WIDTH + DECAY_RANK, 2 * WIDTH + DECAY_RANK, 3 * WIDTH + DECAY_RANK,
         3 * WIDTH + DECAY_RANK + A_RANK, RWKV_COLS]
    perm = jnp.concatenate([jnp.arange(c[0], c[1]), jnp.arange(c[2], c[3]), jnp.arange(c[3], c[4]),
                            jnp.arange(c[1], c[2]), jnp.arange(c[4], c[5]), jnp.arange(c[5], c[6])])
    w_main = jnp.concatenate([w_in[:, :RWKV_COLS][:, perm], w_in[:, RWKV_COLS:RWKV_COLS + 3 * WIDTH]],
                             axis=1).astype(BF16)
    w_idx = w_in[:, RWKV_COLS + 3 * WIDTH:]
    w_idx = jnp.pad(w_idx, ((0, 0), (0, N_IDX - w_idx.shape[1])))
    w_idx_hi = w_idx.astype(BF16)
    w_idx_lo = (w_idx - w_idx_hi.astype(F32)).astype(BF16)

    zr, q, k, v, qi, kiw = _in_proj(x.reshape(m, d), row(n_mix_pre), w_main, w_idx_hi, w_idx_lo)

    zeros = jnp.zeros((DECAY_RANK, WIDTH), F32)
    w_lora = jnp.concatenate([jnp.concatenate([w2r, zeros], axis=1),
                              jnp.concatenate([zeros, a2], axis=1)], axis=0).astype(BF16)
    y_rwkv = _rwkv(zr.reshape(b, s, RWKV_COLS), row(shift_mu[perm]), row(w0), w_lora, row(a0), g2.astype(BF16),
                   row(k_k), row(k_a), row(r_k), row(ln_g), row(ln_b))

    band = _bias_tiles(rel_bias.astype(F32))
    pad = lambda a: jnp.pad(a.reshape(b, s, -1), ((0, 0), (FRONT_PAD, BACK_PAD), (0, 0)))
    qi_t = qi.reshape(b, s, IDX_HEADS, IDX_DIM).transpose(0, 2, 1, 3)
    y_dsa = _dsa(qi_t, q.reshape(b, s, WIDTH), pad(kiw), pad(k), pad(v), band, row(idx_g), row(idx_b))

    gains = tuple(row(a) for a in (n_mix_post, n_ffn_pre, n_ffn_post, n_ple_pre, n_ple_post))
    out = _post(y_rwkv.reshape(m, WIDTH), y_dsa.reshape(m, WIDTH), x.reshape(m, d), p.reshape(m, -1),
                w_out[:WIDTH].astype(BF16), w_out[WIDTH:].astype(BF16), w_f1.astype(BF16), w_f2.astype(BF16),
                w_ple.astype(BF16), w_gate.astype(BF16), gains)
    return out.reshape(b, s, d)


def kernel(x, p, w_in, shift_mu, rwkv_w0, rwkv_w2, rwkv_a0, rwkv_a2, rwkv_g2, rwkv_k_k, rwkv_k_a, rwkv_r_k, rwkv_ln_g, rwkv_ln_b, idx_k_ln_g, idx_k_ln_b, rel_bias, w_out, w_ffn1, w_ffn2, w_ple, w_ple_gate, norm_mix_pre, norm_mix_post, norm_ffn_pre, norm_ffn_post, norm_ple_pre, norm_ple_post):
    for i in range(p.shape[0]):
        x = _layer(x, p[i], w_in[i], shift_mu[i], rwkv_w0[i], rwkv_w2[i], rwkv_a0[i], rwkv_a2[i], rwkv_g2[i],
                   rwkv_k_k[i], rwkv_k_a[i], rwkv_r_k[i], rwkv_ln_g[i], rwkv_ln_b[i], idx_k_ln_g[i],
                   idx_k_ln_b[i], rel_bias, w_out[i], w_ffn1[i], w_ffn2[i], w_ple[i], w_ple_gate[i],
                   norm_mix_pre[i], norm_mix_post[i], norm_ffn_pre[i], norm_ffn_post[i], norm_ple_pre[i],
                   norm_ple_post[i])
    return x
```

```python
import functools
import math

import jax
import jax.numpy as jnp
from jax import lax
from jax.experimental import pallas as pl
from jax.experimental.pallas import tpu as pltpu

F32 = jnp.float32
BF16 = jnp.bfloat16
I32 = jnp.int32

CHUNK = 64
HEADS = 8
HEAD_DIM = 64
WIDTH = HEADS * HEAD_DIM
DECAY_RANK = 64
A_RANK = 64
GATE_RANK = 128
GN_EPS = 64e-5
IDX_HEADS = 8
IDX_DIM = 64
MAX_TOPK = 256
REL_BUCKETS = 32
REL_MAX_DISTANCE = 128
NORM_EPS = 1e-6
LN_EPS = 1e-6
NEG_INF = -1e30
RWKV_COLS = 1792
N_MAIN = RWKV_COLS + 3 * WIDTH
N_IDX = 640

LANES = 128
KEY_TILE = 256
HEAD_PAIRS = WIDTH // LANES
GROUP = 4
WAVE = 8
FRONT_PAD = 2 * CHUNK
BACK_PAD = CHUNK
VMEM_LIMIT = 56 * 1024 * 1024

INT_MIN = -(2**31)
INT_MAX = 2**31 - 1


def _nt(a, b):
    return lax.dot_general(a, b, (((1,), (1,)), ((), ())), preferred_element_type=F32)


def _tn(a, b):
    return lax.dot_general(a, b, (((0,), (0,)), ((), ())), preferred_element_type=F32)


def _mm(a, b):
    return jnp.dot(a, b, preferred_element_type=F32)


def _split2(x):
    hi = x.astype(BF16)
    lo = (x - hi.astype(F32)).astype(BF16)
    return hi, lo


def _split3(x):
    hi = x.astype(BF16)
    r = x - hi.astype(F32)
    mid = r.astype(BF16)
    lo = (r - mid.astype(F32)).astype(BF16)
    return hi, mid, lo


def _rms(x, g):
    return x * lax.rsqrt(jnp.mean(x * x, axis=-1, keepdims=True) + NORM_EPS) * g


def _sigmoid(x):
    return 1.0 / (1.0 + jnp.exp(-x))


def _in_proj_kernel(x_ref, g_ref, wm_ref, wih_ref, wil_ref, zr_ref, q_ref, k_ref, v_ref, qi_ref, kiw_ref):
    h = _rms(x_ref[...], g_ref[...])
    hh, hl = _split2(h)
    zm = _mm(hh, wm_ref[...])
    zr_ref[...] = zm[:, :RWKV_COLS]
    q_ref[...] = zm[:, RWKV_COLS:RWKV_COLS + WIDTH].astype(BF16)
    k_ref[...] = zm[:, RWKV_COLS + WIDTH:RWKV_COLS + 2 * WIDTH].astype(BF16)
    v_ref[...] = zm[:, RWKV_COLS + 2 * WIDTH:].astype(BF16)
    zi = _mm(hh, wih_ref[...]) + _mm(hh, wil_ref[...]) + _mm(hl, wih_ref[...])
    for h in range(IDX_HEADS):
        qi_ref[h] = zi[:, h * IDX_DIM:(h + 1) * IDX_DIM]
    kiw_ref[...] = zi[:, WIDTH:]


def _in_proj(x2, g, wm, wih, wil, tm=512):
    m, d = x2.shape
    row = lambda n: pl.BlockSpec((tm, n), lambda i: (i, 0))
    full = lambda a: pl.BlockSpec(a.shape, lambda i: (0, 0))
    return pl.pallas_call(
        _in_proj_kernel,
        grid=(m // tm,),
        in_specs=[row(d), full(g), full(wm), full(wih), full(wil)],
        out_specs=[row(RWKV_COLS), row(WIDTH), row(WIDTH), row(WIDTH),
                   pl.BlockSpec((IDX_HEADS, tm, IDX_DIM), lambda i: (0, i, 0)), row(LANES)],
        out_shape=[
            jax.ShapeDtypeStruct((m, RWKV_COLS), F32),
            jax.ShapeDtypeStruct((m, WIDTH), BF16),
            jax.ShapeDtypeStruct((m, WIDTH), BF16),
            jax.ShapeDtypeStruct((m, WIDTH), BF16),
            jax.ShapeDtypeStruct((IDX_HEADS, m, IDX_DIM), F32),
            jax.ShapeDtypeStruct((m, LANES), F32),
        ],
        compiler_params=pltpu.CompilerParams(dimension_semantics=("arbitrary",), vmem_limit_bytes=VMEM_LIMIT),
        name="in_proj",
    )(x2, g, wm, wih, wil)


def _t5_bucket(rel):
    half = REL_BUCKETS // 2
    max_exact = half // 2
    ret = jnp.where(rel > 0, half, 0)
    n = jnp.abs(rel)
    nf = jnp.maximum(n, 1).astype(F32)
    steps = jnp.log(nf / max_exact) / math.log(REL_MAX_DISTANCE / max_exact) * (half - max_exact)
    large = max_exact
    for level in range(1, half - max_exact):
        large = large + jnp.where(steps >= level - 1e-4, 1, 0)
    return ret + jnp.where(n < max_exact, n, large)


def _bias_kernel(tab_ref, band_ref):
    r = lax.broadcasted_iota(I32, (CHUNK, KEY_TILE), 0)
    c = lax.broadcasted_iota(I32, (CHUNK, KEY_TILE), 1)
    near_bucket = _t5_bucket(c - FRONT_PAD - r)
    far_bucket = _t5_bucket(jnp.full((CHUNK, KEY_TILE), -(FRONT_PAD + 1), I32))
    for h in range(HEADS):
        near = jnp.zeros((CHUNK, KEY_TILE), F32)
        far = jnp.zeros((CHUNK, KEY_TILE), F32)
        for j in range(REL_BUCKETS):
            t = tab_ref[j, h]
            near = jnp.where(near_bucket == j, t, near)
            far = jnp.where(far_bucket == j, t, far)
        band_ref[h * CHUNK:(h + 1) * CHUNK, :] = near - far


def _bias_tiles(rel_bias):
    return pl.pallas_call(
        _bias_kernel,
        in_specs=[pl.BlockSpec(memory_space=pltpu.SMEM)],
        out_shape=jax.ShapeDtypeStruct((HEADS * CHUNK, KEY_TILE), F32),
        name="bias_tiles",
    )(rel_bias)


def _seg_sum(x):
    rows = x.shape[0]
    li = lax.broadcasted_iota(I32, (LANES, LANES), 0) // HEAD_DIM
    lj = lax.broadcasted_iota(I32, (LANES, LANES), 1) // HEAD_DIM
    ones = (li == lj).astype(BF16)
    xs = jnp.concatenate([x[:, g * LANES:(g + 1) * LANES] for g in range(HEAD_PAIRS)], axis=0)
    y = _mm(xs.astype(BF16), ones)
    return jnp.concatenate([y[g * rows:(g + 1) * rows] for g in range(HEAD_PAIRS)], axis=1)


def _rwkv_kernel(z_ref, mu_ref, w0_ref, wl_ref, a0_ref, g2_ref, kk_ref, ka_ref, rk_ref, lng_ref, lnb_ref,
                 y_ref, prev_s, st_s, *, nb):
    ci = pl.program_id(1)
    rows = nb * CHUNK

    @pl.when(ci == 0)
    def _():
        prev_s[...] = jnp.zeros_like(prev_s)
        st_s[...] = jnp.zeros_like(st_s)

    z = z_ref[...].reshape(rows, RWKV_COLS)
    row = lax.broadcasted_iota(I32, z.shape, 0)
    zp = pltpu.roll(z, 1, axis=0)
    for bb in range(nb):
        zp = jnp.where(row == bb * CHUNK, prev_s[bb], zp)
        prev_s[bb] = z[(bb + 1) * CHUNK - 1:(bb + 1) * CHUNK, :]
    zs = z + (zp - z) * mu_ref[...]
    r = zs[:, 0:WIDTH]
    k = zs[:, WIDTH:2 * WIDTH]
    v = zs[:, 2 * WIDTH:3 * WIDTH]
    zl = zs[:, 3 * WIDTH:3 * WIDTH + LANES]
    zg = zs[:, 3 * WIDTH + LANES:]

    lane = lax.broadcasted_iota(I32, zl.shape, 1)
    zl = jnp.where(lane < DECAY_RANK, jnp.tanh(zl), zl)
    la = _mm(zl.astype(BF16), wl_ref[...])
    y0 = -(w0_ref[...] + la[:, :WIDTH])
    softplus = jnp.maximum(y0, 0.0) + jnp.log(1.0 + jnp.exp(-jnp.abs(y0)))
    ld = -jnp.exp(-softplus - 0.5)
    a = _sigmoid(a0_ref[...] + la[:, WIDTH:])
    gate = _mm(_sigmoid(zg).astype(BF16), g2_ref[...])

    kk = k * kk_ref[...]
    kk = kk * jnp.minimum(lax.rsqrt(_seg_sum(kk * kk)), 1e12)
    kmod = k * (1.0 + (a - 1.0) * ka_ref[...])
    bvec = kk * a

    ti = lax.broadcasted_iota(I32, (rows, rows), 0)
    tj = lax.broadcasted_iota(I32, (rows, rows), 1)
    tri = ((ti >= tj) & (ti // CHUNK == tj // CHUNK)).astype(BF16)
    l1, l2, l3 = _split3(ld)
    cum = _mm(tri, l1) + _mm(tri, l2) + _mm(tri, l3)
    cum_x = cum - ld
    row_w = lax.broadcasted_iota(I32, (rows, WIDTH), 0)
    cref = cum[CHUNK // 2 - 1:CHUNK // 2, :]
    clast = cum[CHUNK - 1:CHUNK, :]
    for bb in range(1, nb):
        mid = bb * CHUNK + CHUNK // 2 - 1
        cref = jnp.where(row_w >= bb * CHUNK, cum[mid:mid + 1, :], cref)
        clast = jnp.where(row_w >= bb * CHUNK, cum[(bb + 1) * CHUNK - 1:(bb + 1) * CHUNK, :], clast)
    e_neg = jnp.exp(cref - cum)
    rt = r * jnp.exp(cum - cref)
    at = -kk * jnp.exp(cum_x - cref)
    kt = kmod * e_neg
    bt = bvec * e_neg
    r0 = r * jnp.exp(cum)
    a0v = -kk * jnp.exp(cum_x)
    e_end = jnp.exp(clast - cum)
    kc = kmod * e_end
    bc = bvec * e_end
    gw = GROUP * HEAD_DIM
    lane_h = lax.broadcasted_iota(I32, (CHUNK, gw), 1) // HEAD_DIM
    src = lax.broadcasted_iota(I32, (CHUNK, gw), 1) & (HEAD_DIM - 1)
    tok = lax.broadcasted_iota(I32, (CHUNK, gw), 0)
    strict = tok > src
    incl = tok >= src
    eye = (tok == src).astype(F32)
    diag_blk = (lax.broadcasted_iota(I32, (gw, gw), 0) // HEAD_DIM) == (lax.broadcasted_iota(I32, (gw, gw), 1) // HEAD_DIM)

    def bdiag(x):
        xb = x.astype(BF16)
        zero = jnp.zeros_like(xb)
        return jnp.concatenate([jnp.where(lane_h == h, xb, zero) for h in range(GROUP)], axis=0)

    all_units = [(bb, g, slice(bb * CHUNK, (bb + 1) * CHUNK), slice(g * gw, (g + 1) * gw))
                 for bb in range(nb) for g in range(HEADS // GROUP)]

    def run_wave(units):
        cut = lambda x: [x[rs, gs] for _, _, rs, gs in units]
        at_w, rt_w, bt_w, kt_w, v_w = cut(at), cut(rt), cut(bt), cut(kt), cut(v)
        lhs = [jnp.concatenate([a_, r_], axis=0).astype(BF16) for a_, r_ in zip(at_w, rt_w)]
        ab_rb = [_nt(l_, bdiag(b_)) for l_, b_ in zip(lhs, bt_w)]
        ak_rk = [_nt(l_, bdiag(k_)) for l_, k_ in zip(lhs, kt_w)]
        low = [jnp.where(strict, x[:CHUNK], 0.0) for x in ab_rb]
        a_ak = [jnp.where(strict, x[:CHUNK], 0.0).astype(BF16) for x in ak_rk]
        a_rb = [jnp.where(incl, x[CHUNK:], 0.0).astype(BF16) for x in ab_rb]
        a_rk = [jnp.where(incl, x[CHUNK:], 0.0).astype(BF16) for x in ak_rk]
        v_bd = [bdiag(x) for x in v_w]
        s_f = [st_s[bb, g] for bb, g, _, _ in units]
        s_b = [s.astype(BF16) for s in s_f]
        x0 = [_nt(x.astype(BF16), s) for x, s in zip(cut(a0v), s_b)]
        o_st = [_nt(x.astype(BF16), s) for x, s in zip(cut(r0), s_b)]
        yk = [_mm(a_, vb) for a_, vb in zip(a_ak, v_bd)]
        t = [eye + x for x in low]
        p = low
        p_bd = [bdiag(x) for x in p]
        for _ in range(int(math.log2(CHUNK)) - 1):
            p = [_mm(x.astype(BF16), xb) for x, xb in zip(p, p_bd)]
            p_bd = [bdiag(x) for x in p]
            t = [t_ + _mm(t_.astype(BF16), xb) for t_, xb in zip(t, p_bd)]
        u = [_mm(t_.astype(BF16), bdiag(a_ + b_)) for t_, a_, b_ in zip(t, x0, yk)]
        o_w = [a_ + _mm(rb, bdiag(u_)) + _mm(rk, vb) for a_, rb, u_, rk, vb in zip(o_st, a_rb, u, a_rk, v_bd)]
        uv = [jnp.concatenate([u_, v_], axis=0).astype(BF16) for u_, v_ in zip(u, v_w)]
        bk = [jnp.concatenate([b_, k_], axis=0).astype(BF16) for b_, k_ in zip(cut(bc), cut(kc))]
        for (bb, gi, _, gs), s, w, x in zip(units, s_f, uv, bk):
            pc = jnp.exp(cum[(bb + 1) * CHUNK - 1:(bb + 1) * CHUNK, gs])
            st_s[bb, gi] = s * pc + jnp.where(diag_blk, _tn(w, x), 0.0)
        return o_w

    o_w = []
    for w0 in range(0, len(all_units), WAVE):
        o_w += run_wave(all_units[w0:w0 + WAVE])

    groups = HEADS // GROUP
    o = jnp.concatenate([jnp.concatenate(o_w[bb * groups:(bb + 1) * groups], axis=1) for bb in range(nb)], axis=0)
    inv_n = 1.0 / HEAD_DIM
    mean = _seg_sum(o) * inv_n
    d = o - mean
    var = _seg_sum(d * d) * inv_n
    on = d * lax.rsqrt(var + GN_EPS) * lng_ref[...] + lnb_ref[...]
    bonus = _seg_sum(r * kmod * rk_ref[...]) * v
    y_ref[...] = ((on + bonus) * gate).astype(BF16).reshape(nb, CHUNK, WIDTH)


def _rwkv(zr, mu, w0, wl, a0, g2, k_k, k_a, r_k, ln_g, ln_b, nb_max=8):
    b, s, _ = zr.shape
    nb = max(n for n in range(1, nb_max + 1) if b % n == 0)
    full = lambda a: pl.BlockSpec(a.shape, lambda i, j: (0,) * a.ndim)
    params = (mu, w0, wl, a0, g2, k_k, k_a, r_k, ln_g, ln_b)
    return pl.pallas_call(
        functools.partial(_rwkv_kernel, nb=nb),
        grid=(b // nb, s // CHUNK),
        in_specs=[pl.BlockSpec((nb, CHUNK, RWKV_COLS), lambda i, j: (i, j, 0))] + [full(a) for a in params],
        out_specs=pl.BlockSpec((nb, CHUNK, WIDTH), lambda i, j: (i, j, 0)),
        out_shape=jax.ShapeDtypeStruct((b, s, WIDTH), BF16),
        scratch_shapes=[
            pltpu.VMEM((nb, 1, RWKV_COLS), F32),
            pltpu.VMEM((nb, HEADS // GROUP, GROUP * HEAD_DIM, GROUP * HEAD_DIM), F32),
        ],
        compiler_params=pltpu.CompilerParams(dimension_semantics=("arbitrary", "arbitrary"),
                                             vmem_limit_bytes=VMEM_LIMIT),
        name="rwkv",
    )(zr, *params)


def _key_to_float(key):
    return lax.bitcast_convert_type(key ^ ((key >> 31) & INT_MAX), F32)


def _dsa_kernel(qi_ref, q_ref, kiw_ref, k_ref, v_ref, band_ref, lng_ref, lnb_ref, o_ref,
                ki3_s, lq_s, wb_s, key_s, keyt_s, thr_s, jlim_s, lg_s, mx_s, mb_s, sm_s, pv_s, *, topk, nb):
    i = pl.program_id(1)
    sp = kiw_ref.shape[1]
    n_far = (i + 1) // 4
    n_tiles = n_far + 1
    far_lim = CHUNK * i - FRONT_PAD
    lane1 = lax.broadcasted_iota(I32, (CHUNK, LANES), 1)
    halves = [slice(0, LANES), slice(LANES, KEY_TILE)]
    n_pairs = (nb + 1) // 2
    kf = float(topk)

    def tile_pos0(c):
        return jnp.where(c < n_far, c * KEY_TILE, far_lim)

    def tile_end(c):
        return jnp.where(c < n_far, far_lim, CHUNK * (i + 1))

    def tile_row0(c):
        return pl.multiple_of(FRONT_PAD + tile_pos0(c), CHUNK)

    def half_pos(c, half):
        return tile_pos0(c) + half * LANES + lane1

    @pl.when(i == 0)
    def _():
        rows = 320 if sp % 320 == 0 else CHUNK

        def body(t, carry):
            bb = t // (sp // rows)
            r0 = pl.multiple_of((t % (sp // rows)) * rows, CHUNK)
            ki = kiw_ref[bb, pl.ds(r0, rows), :][:, :IDX_DIM]
            mean = jnp.mean(ki, axis=-1, keepdims=True)
            var = jnp.mean(jnp.square(ki - mean), axis=-1, keepdims=True)
            kn = (ki - mean) * lax.rsqrt(var + LN_EPS) * lng_ref[...] + lnb_ref[...]
            hi, lo = _split2(kn)
            ki3_s[bb, pl.ds(r0, rows), :] = jnp.concatenate([hi, lo, hi], axis=1)
            return carry

        lax.fori_loop(0, nb * (sp // rows), body, 0)

    def stage_queries(bb, carry):
        qi = qi_ref[:, bb].reshape(IDX_HEADS * CHUNK, IDX_DIM)
        qh, ql = _split2(qi)
        lq_s[bb] = jnp.concatenate([qh, qh, ql], axis=1)
        wq = kiw_ref[bb, pl.ds(pl.multiple_of(FRONT_PAD + CHUNK * i, CHUNK), CHUNK), :]
        wscale = IDX_HEADS ** -0.5 * IDX_DIM ** -0.5
        wcol = jnp.concatenate([wq[:, IDX_DIM + h:IDX_DIM + h + 1] for h in range(IDX_HEADS)], axis=0) * wscale
        wb_s[bb] = jnp.broadcast_to(wcol, (IDX_HEADS * CHUNK, LANES))
        return carry

    lax.fori_loop(0, nb, stage_queries, 0)

    def score_tile(c, carry):
        r0 = tile_row0(c)
        end = tile_end(c)
        xs = [_nt(lq_s[bb], ki3_s[bb, pl.ds(r0, KEY_TILE), :]) for bb in range(nb)]
        keys = []
        for bb in range(nb):
            wb = wb_s[bb]
            parts = []
            for half, hs_ in enumerate(halves):
                act = jnp.maximum(xs[bb][:, hs_], 0.0) * wb
                sc = act[0:CHUNK]
                for h in range(1, IDX_HEADS):
                    sc = sc + act[h * CHUNK:(h + 1) * CHUNK]
                pos = half_pos(c, half)
                valid = (pos >= 0) & (pos < end)
                parts.append(jnp.where(valid, sc, -jnp.inf))
            keys.append(jnp.concatenate(parts, axis=1))
            key_s[bb, c] = keys[bb]
        for pr in range(n_pairs):
            both = jnp.concatenate([keys[2 * pr], keys[min(2 * pr + 1, nb - 1)]], axis=0)
            keyt_s[pr, c] = both.T
        return carry

    lax.fori_loop(0, n_tiles, score_tile, 0)

    @pl.when(n_tiles % 2 == 1)
    def _():
        for pr in range(n_pairs):
            keyt_s[pr, n_tiles] = jnp.full((KEY_TILE, 2 * CHUNK), -jnp.inf, F32)

    thr_s[...] = jnp.full(thr_s.shape, -jnp.inf, F32)
    jlim_s[...] = jnp.full(jlim_s.shape, INT_MAX, I32)

    part = 32

    def count(cands, above):
        wide = [jnp.broadcast_to(cd, (part, LANES)) for cd in cands]

        def body(c2, accs):
            out = list(accs)
            for c in (2 * c2, 2 * c2 + 1):
                for pr in range(n_pairs):
                    sc = keyt_s[pr, c].reshape(KEY_TILE // part, part, LANES)
                    out[pr] = out[pr] + jnp.sum(jnp.where(above(sc, wide[pr]), 1.0, 0.0), axis=0)
            return tuple(out)

        accs = lax.fori_loop(0, (n_tiles + 1) // 2, body,
                             tuple(jnp.zeros((part, LANES), F32) for _ in range(n_pairs)))
        return [jnp.sum(a, axis=0, keepdims=True) for a in accs]

    def count_ge(keys):
        return count([_key_to_float(k) for k in keys], lambda s, c: s >= c)

    @pl.when(CHUNK * (i + 1) > topk)
    def _():
        n_all = float(KEY_TILE) * n_tiles.astype(F32)
        n_pos = count_ge([jnp.zeros((1, LANES), I32)] * n_pairs)
        ans0 = tuple(jnp.where(n >= kf, 0, INT_MIN).astype(I32) for n in n_pos)
        cnt0 = tuple(jnp.where(n >= kf, n, n_all) for n in n_pos)

        def one_bit(t, carry):
            ans, n_ans = carry
            bit = lax.shift_left(jnp.int32(1), 30 - t)
            cands = [a | bit for a in ans]
            cnts = count_ge(cands)
            take = [n >= kf for n in cnts]
            return (tuple(jnp.where(tk, cd, a) for tk, cd, a in zip(take, cands, ans)),
                    tuple(jnp.where(tk, n, na) for tk, n, na in zip(take, cnts, n_ans)))

        ans, n_ans = lax.fori_loop(0, 31, one_bit, (ans0, cnt0))
        def to_columns(rows_, dst):
            for pr in range(n_pairs):
                col = jnp.broadcast_to(rows_[pr], (LANES, LANES)).T
                dst[2 * pr] = col[:CHUNK]
                if 2 * pr + 1 < nb:
                    dst[2 * pr + 1] = col[CHUNK:]

        kth = [_key_to_float(a) for a in ans]
        to_columns(kth, thr_s)
        most = n_ans[0]
        for pr in range(1, n_pairs):
            most = jnp.maximum(most, n_ans[pr])

        @pl.when(jnp.max(most) > kf)
        def _():
            n_gt = count(kth, lambda s, c: s > c)
            need = [kf - n for n in n_gt]
            ans_w = [jnp.broadcast_to(a, (KEY_TILE, LANES)) for a in kth]
            krow = lax.broadcasted_iota(I32, (KEY_TILE, LANES), 0)
            tri = (lax.broadcasted_iota(I32, (KEY_TILE, KEY_TILE), 0)
                   >= lax.broadcasted_iota(I32, (KEY_TILE, KEY_TILE), 1)).astype(BF16)

            def body(c, carry):
                seen, last = carry
                pos = tile_pos0(c) + krow
                seen_out, last_out = [], []
                for pr in range(n_pairs):
                    tie = keyt_s[pr, c] == ans_w[pr]
                    rank = _mm(tri, jnp.where(tie, 1.0, 0.0).astype(BF16)) + seen[pr]
                    kept = jnp.where(tie & (rank <= need[pr]), pos, -1)
                    last_out.append(jnp.maximum(last[pr], jnp.max(kept, axis=0, keepdims=True)))
                    seen_out.append(rank[KEY_TILE - 1:KEY_TILE, :])
                return tuple(seen_out), tuple(last_out)

            init = (tuple(jnp.zeros((1, LANES), F32) for _ in range(n_pairs)),
                    tuple(jnp.full((1, LANES), -1, I32) for _ in range(n_pairs)))
            _, last = lax.fori_loop(0, n_tiles, body, init)
            to_columns(last, jlim_s)

    pair_lane_head = lane1 // HEAD_DIM

    cw = 2 if nb % 2 == 0 else 1

    def phase_c(step, carry):
        bbs = [step * cw + j for j in range(cw)]
        zero = jnp.zeros((CHUNK, LANES), BF16)
        qm = []
        for bb in bbs:
            qs = q_ref[bb] * jnp.asarray(HEAD_DIM ** -0.5, BF16)
            qm.append([jnp.concatenate([jnp.where(pair_lane_head == 0, qs[:, g * LANES:(g + 1) * LANES], zero),
                                        jnp.where(pair_lane_head == 1, qs[:, g * LANES:(g + 1) * LANES], zero)],
                                       axis=0) for g in range(HEAD_PAIRS)])
        thr = [thr_s[bb] for bb in bbs]
        jlim = [jlim_s[bb] for bb in bbs]
        mx_s[...] = jnp.full(mx_s.shape, NEG_INF, F32)

        def logits(c, bias):
            r0 = tile_row0(c)
            end = tile_end(c)
            kt = [k_ref[bb, pl.ds(r0, KEY_TILE), :] for bb in bbs]
            s = [[_nt(qm[j][g], kt[j][:, g * LANES:(g + 1) * LANES]) for g in range(HEAD_PAIRS)]
                 for j in range(cw)]
            for j, bb in enumerate(bbs):
                sel = []
                for half, hs_ in enumerate(halves):
                    key = key_s[bb, c, :, hs_]
                    pos = half_pos(c, half)
                    sel.append((pos >= 0) & (pos < end) & ((key > thr[j]) | ((key == thr[j]) & (pos <= jlim[j]))))
                for g in range(HEAD_PAIRS):
                    sg = s[j][g]
                    if bias is not None:
                        sg = sg + bias[g * 2 * CHUNK:(g + 1) * 2 * CHUNK]
                    for hh in range(2):
                        rows = slice((2 * g + hh) * CHUNK, (2 * g + hh + 1) * CHUNK)
                        sub = sg[hh * CHUNK:(hh + 1) * CHUNK]
                        m0 = jnp.where(sel[0], sub[:, halves[0]], NEG_INF)
                        m1 = jnp.where(sel[1], sub[:, halves[1]], NEG_INF)
                        lg_s[j, c, rows, halves[0]] = m0
                        lg_s[j, c, rows, halves[1]] = m1
                        mx_s[j, rows] = jnp.maximum(mx_s[j, rows], jnp.maximum(m0, m1))

        def far_logits(c, carry):
            logits(c, None)
            return carry

        lax.fori_loop(0, n_far, far_logits, 0)
        logits(n_far, band_ref[...])

        for j in range(cw):
            mb_s[j] = jnp.broadcast_to(jnp.max(mx_s[j], axis=1, keepdims=True), mb_s.shape[1:])
        sm_s[...] = jnp.zeros_like(sm_s)
        pv_s[...] = jnp.zeros_like(pv_s)

        def weigh(c, carry):
            r0 = tile_row0(c)
            vt = [v_ref[bb, pl.ds(r0, KEY_TILE), :] for bb in bbs]
            for g in range(HEAD_PAIRS):
                rows = slice(g * 2 * CHUNK, (g + 1) * 2 * CHUNK)
                ps = []
                for j in range(cw):
                    m = mb_s[j, rows]
                    p0 = jnp.exp(lg_s[j, c, rows, halves[0]] - m)
                    p1 = jnp.exp(lg_s[j, c, rows, halves[1]] - m)
                    sm_s[j, rows] = sm_s[j, rows] + (p0 + p1)
                    ps.append(jnp.concatenate([p0, p1], axis=1).astype(BF16))
                for j in range(cw):
                    pv_s[j, rows] = pv_s[j, rows] + _mm(ps[j], vt[j][:, g * LANES:(g + 1) * LANES])
            return carry

        lax.fori_loop(0, n_tiles, weigh, 0)

        for j, bb in enumerate(bbs):
            outn = pv_s[j] / jnp.sum(sm_s[j], axis=1, keepdims=True)
            cols = []
            for g in range(HEAD_PAIRS):
                top = outn[(2 * g) * CHUNK:(2 * g + 1) * CHUNK]
                bot = outn[(2 * g + 1) * CHUNK:(2 * g + 2) * CHUNK]
                cols.append(jnp.where(pair_lane_head == 0, top, bot))
            o_ref[bb] = jnp.concatenate(cols, axis=1).astype(BF16)
        return carry

    lax.fori_loop(0, nb // cw, phase_c, 0)


def _dsa(qi_t, q, kiw_p, k_p, v_p, band, ln_g, ln_b):
    b, s, _ = q.shape
    sp = k_p.shape[1]
    topk = min(MAX_TOPK, s // 4)
    max_tiles = (s // CHUNK) // 4 + 1
    nb = 4 if b % 4 == 0 else (2 if b % 2 == 0 else 1)
    cw = 2 if nb % 2 == 0 else 1
    rows = HEADS * CHUNK
    full = lambda a: pl.BlockSpec(a.shape, lambda i, j: (0,) * a.ndim)
    per_group = lambda n: pl.BlockSpec((nb, sp, n), lambda i, j: (i, 0, 0), pipeline_mode=pl.Buffered(1))
    return pl.pallas_call(
        functools.partial(_dsa_kernel, topk=topk, nb=nb),
        grid=(b // nb, s // CHUNK),
        in_specs=[
            pl.BlockSpec((IDX_HEADS, nb, CHUNK, IDX_DIM), lambda i, j: (0, i, j, 0)),
            pl.BlockSpec((nb, CHUNK, WIDTH), lambda i, j: (i, j, 0)),
            per_group(LANES), per_group(WIDTH), per_group(WIDTH),
            full(band), full(ln_g), full(ln_b),
        ],
        out_specs=pl.BlockSpec((nb, CHUNK, WIDTH), lambda i, j: (i, j, 0)),
        out_shape=jax.ShapeDtypeStruct((b, s, WIDTH), BF16),
        scratch_shapes=[
            pltpu.VMEM((nb, sp, 3 * IDX_DIM), BF16),
            pltpu.VMEM((nb, rows, 3 * IDX_DIM), BF16),
            pltpu.VMEM((nb, rows, LANES), F32),
            pltpu.VMEM((nb, max_tiles, CHUNK, KEY_TILE), F32),
            pltpu.VMEM(((nb + 1) // 2, max_tiles + 1, KEY_TILE, 2 * CHUNK), F32),
            pltpu.VMEM((nb, CHUNK, LANES), F32),
            pltpu.VMEM((nb, CHUNK, LANES), I32),
            pltpu.VMEM((cw, max_tiles, rows, KEY_TILE), F32),
            pltpu.VMEM((cw, rows, LANES), F32),
            pltpu.VMEM((cw, rows, LANES), F32),
            pltpu.VMEM((cw, rows, LANES), F32),
            pltpu.VMEM((cw, rows, LANES), F32),
        ],
        compiler_params=pltpu.CompilerParams(dimension_semantics=("arbitrary", "arbitrary"),
                                             vmem_limit_bytes=VMEM_LIMIT),
        name="dsa",
    )(qi_t, q, kiw_p, k_p, v_p, band, ln_g, ln_b)


def _post_kernel(y1_ref, y2_ref, x_ref, p_ref, woa_ref, wob_ref, w1_ref, w2_ref, wp_ref, wg_ref,
                 g_mix_ref, g_f1_ref, g_f2_ref, g_p1_ref, g_p2_ref, o_ref):
    mix = _mm(y1_ref[...], woa_ref[...]) + _mm(y2_ref[...], wob_ref[...])
    x1 = x_ref[...] + _rms(mix, g_mix_ref[...])
    h = _rms(x1, g_f1_ref[...]).astype(BF16)
    hid = jnp.square(jnp.maximum(_mm(h, w1_ref[...]), 0.0)).astype(BF16)
    x2 = x1 + _rms(_mm(hid, w2_ref[...]), g_f2_ref[...])
    gate = _sigmoid(_mm(_rms(x2, g_p1_ref[...]).astype(BF16), wg_ref[...]))
    pe = _mm(p_ref[...].astype(BF16), wp_ref[...]) * gate
    o_ref[...] = x2 + _rms(pe, g_p2_ref[...])


def _post(y1, y2, x2d, p2d, woa, wob, w1, w2, wp, wg, gains, tm=512):
    m, d = x2d.shape
    row = lambda n: pl.BlockSpec((tm, n), lambda i: (i, 0))
    const = lambda a: pl.BlockSpec(a.shape, lambda i: (0, 0), pipeline_mode=pl.Buffered(1))
    weights = (woa, wob, w1, w2, wp, wg) + tuple(gains)
    return pl.pallas_call(
        _post_kernel,
        grid=(m // tm,),
        in_specs=[row(WIDTH), row(WIDTH), row(d), row(p2d.shape[1])] + [const(a) for a in weights],
        out_specs=row(d),
        out_shape=jax.ShapeDtypeStruct((m, d), F32),
        compiler_params=pltpu.CompilerParams(dimension_semantics=("arbitrary",), vmem_limit_bytes=VMEM_LIMIT),
        name="post",
    )(y1, y2, x2d, p2d, *weights)


def _layer(x, p, w_in, shift_mu, w0, w2r, a0, a2, g2, k_k, k_a, r_k, ln_g, ln_b, idx_g, idx_b, rel_bias,
           w_out, w_f1, w_f2, w_ple, w_gate, n_mix_pre, n_mix_post, n_ffn_pre, n_ffn_post, n_ple_pre, n_ple_post):
    b, s, d = x.shape
    m = b * s
    row = lambda a: a.reshape(1, -1).astype(F32)

    c = [0, WIDTH, WIDTH + DECAY_RANK, 2 * WIDTH + DECAY_RANK, 3 * WIDTH + DECAY_RANK,
         3 * WIDTH + DECAY_RANK + A_RANK, RWKV_COLS]
    order = [(c[0], c[1]), (c[2], c[3]), (c[3], c[4]), (c[1], c[2]), (c[4], c[5]), (c[5], c[6])]
    reorder = lambda a: jnp.concatenate([a[..., lo:hi] for lo, hi in order], axis=-1)
    w_main = jnp.concatenate([reorder(w_in[:, :RWKV_COLS]), w_in[:, RWKV_COLS:RWKV_COLS + 3 * WIDTH]],
                             axis=1).astype(BF16)
    w_idx = w_in[:, RWKV_COLS + 3 * WIDTH:]
    w_idx = jnp.pad(w_idx, ((0, 0), (0, N_IDX - w_idx.shape[1])))
    w_idx_hi = w_idx.astype(BF16)
    w_idx_lo = (w_idx - w_idx_hi.astype(F32)).astype(BF16)

    zr, q, k, v, qi, kiw = _in_proj(x.reshape(m, d), row(n_mix_pre), w_main, w_idx_hi, w_idx_lo)

    zeros = jnp.zeros((DECAY_RANK, WIDTH), F32)
    w_lora = jnp.concatenate([jnp.concatenate([w2r, zeros], axis=1),
                              jnp.concatenate([zeros, a2], axis=1)], axis=0).astype(BF16)
    y_rwkv = _rwkv(zr.reshape(b, s, RWKV_COLS), row(reorder(shift_mu)), row(w0), w_lora, row(a0), g2.astype(BF16),
                   row(k_k), row(k_a), row(r_k), row(ln_g), row(ln_b))

    band = _bias_tiles(rel_bias.astype(F32))
    pad = lambda a: jnp.pad(a.reshape(b, s, -1), ((0, 0), (FRONT_PAD, BACK_PAD), (0, 0)))
    y_dsa = _dsa(qi.reshape(IDX_HEADS, b, s, IDX_DIM), q.reshape(b, s, WIDTH), pad(kiw), pad(k), pad(v), band,
                 row(idx_g), row(idx_b))

    gains = tuple(row(a) for a in (n_mix_post, n_ffn_pre, n_ffn_post, n_ple_pre, n_ple_post))
    out = _post(y_rwkv.reshape(m, WIDTH), y_dsa.reshape(m, WIDTH), x.reshape(m, d), p.reshape(m, -1),
                w_out[:WIDTH].astype(BF16), w_out[WIDTH:].astype(BF16), w_f1.astype(BF16), w_f2.astype(BF16),
                w_ple.astype(BF16), w_gate.astype(BF16), gains)
    return out.reshape(b, s, d)


def kernel(x, p, w_in, shift_mu, rwkv_w0, rwkv_w2, rwkv_a0, rwkv_a2, rwkv_g2, rwkv_k_k, rwkv_k_a, rwkv_r_k, rwkv_ln_g, rwkv_ln_b, idx_k_ln_g, idx_k_ln_b, rel_bias, w_out, w_ffn1, w_ffn2, w_ple, w_ple_gate, norm_mix_pre, norm_mix_post, norm_ffn_pre, norm_ffn_post, norm_ple_pre, norm_ple_post):
    for i in range(p.shape[0]):
        x = _layer(x, p[i], w_in[i], shift_mu[i], rwkv_w0[i], rwkv_w2[i], rwkv_a0[i], rwkv_a2[i], rwkv_g2[i],
                   rwkv_k_k[i], rwkv_k_a[i], rwkv_r_k[i], rwkv_ln_g[i], rwkv_ln_b[i], idx_k_ln_g[i],
                   idx_k_ln_b[i], rel_bias, w_out[i], w_ffn1[i], w_ffn2[i], w_ple[i], w_ple_gate[i],
                   norm_mix_pre[i], norm_mix_post[i], norm_ffn_pre[i], norm_ffn_post[i], norm_ple_pre[i],
                   norm_ple_post[i])
    return x
```

```python
import functools
import math

import jax
import jax.numpy as jnp
from jax import lax
from jax.experimental import pallas as pl
from jax.experimental.pallas import tpu as pltpu

F32 = jnp.float32
BF16 = jnp.bfloat16
I32 = jnp.int32

CHUNK = 64
HEADS = 8
HEAD_DIM = 64
WIDTH = HEADS * HEAD_DIM
DECAY_RANK = 64
A_RANK = 64
GATE_RANK = 128
GN_EPS = 64e-5
IDX_HEADS = 8
IDX_DIM = 64
MAX_TOPK = 256
REL_BUCKETS = 32
REL_MAX_DISTANCE = 128
NORM_EPS = 1e-6
LN_EPS = 1e-6
NEG_INF = -1e30
RWKV_COLS = 1792
N_MAIN = RWKV_COLS + 3 * WIDTH
N_IDX = 640

LANES = 128
KEY_TILE = 256
HEAD_PAIRS = WIDTH // LANES
GROUP = 4
WAVE = 8
FRONT_PAD = 2 * CHUNK
BACK_PAD = CHUNK
VMEM_LIMIT = 62 * 1024 * 1024

INT_MIN = -(2**31)
INT_MAX = 2**31 - 1


def _nt(a, b):
    return lax.dot_general(a, b, (((1,), (1,)), ((), ())), preferred_element_type=F32)


def _tn(a, b):
    return lax.dot_general(a, b, (((0,), (0,)), ((), ())), preferred_element_type=F32)


def _mm(a, b):
    return jnp.dot(a, b, preferred_element_type=F32)


def _split2(x):
    hi = x.astype(BF16)
    lo = (x - hi.astype(F32)).astype(BF16)
    return hi, lo


def _rms(x, g):
    return x * lax.rsqrt(jnp.mean(x * x, axis=-1, keepdims=True) + NORM_EPS) * g


def _sigmoid(x):
    return 1.0 / (1.0 + jnp.exp(-x))


def _in_proj_kernel(x_ref, g_ref, wm_ref, wih_ref, wil_ref, zr_ref, q_ref, k_ref, v_ref, qi_ref, kiw_ref):
    h = _rms(x_ref[...], g_ref[...])
    hh, hl = _split2(h)
    zm = _mm(hh, wm_ref[...])
    zr_ref[...] = zm[:, :RWKV_COLS]
    q_ref[...] = zm[:, RWKV_COLS:RWKV_COLS + WIDTH].astype(BF16)
    k_ref[...] = zm[:, RWKV_COLS + WIDTH:RWKV_COLS + 2 * WIDTH].astype(BF16)
    v_ref[...] = zm[:, RWKV_COLS + 2 * WIDTH:].astype(BF16)
    zi = _mm(hh, wih_ref[...]) + _mm(hh, wil_ref[...]) + _mm(hl, wih_ref[...])
    for h in range(IDX_HEADS):
        qi_ref[h] = zi[:, h * IDX_DIM:(h + 1) * IDX_DIM]
    kiw_ref[...] = zi[:, WIDTH:]


def _in_proj(x2, g, wm, wih, wil, tm=512):
    m, d = x2.shape
    row = lambda n: pl.BlockSpec((tm, n), lambda i: (i, 0))
    full = lambda a: pl.BlockSpec(a.shape, lambda i: (0, 0))
    return pl.pallas_call(
        _in_proj_kernel,
        grid=(m // tm,),
        in_specs=[row(d), full(g), full(wm), full(wih), full(wil)],
        out_specs=[row(RWKV_COLS), row(WIDTH), row(WIDTH), row(WIDTH),
                   pl.BlockSpec((IDX_HEADS, tm, IDX_DIM), lambda i: (0, i, 0)), row(LANES)],
        out_shape=[
            jax.ShapeDtypeStruct((m, RWKV_COLS), F32),
            jax.ShapeDtypeStruct((m, WIDTH), BF16),
            jax.ShapeDtypeStruct((m, WIDTH), BF16),
            jax.ShapeDtypeStruct((m, WIDTH), BF16),
            jax.ShapeDtypeStruct((IDX_HEADS, m, IDX_DIM), F32),
            jax.ShapeDtypeStruct((m, LANES), F32),
        ],
        compiler_params=pltpu.CompilerParams(dimension_semantics=("arbitrary",), vmem_limit_bytes=VMEM_LIMIT),
        name="in_proj",
    )(x2, g, wm, wih, wil)


def _t5_bucket(rel):
    half = REL_BUCKETS // 2
    max_exact = half // 2
    ret = jnp.where(rel > 0, half, 0)
    n = jnp.abs(rel)
    nf = jnp.maximum(n, 1).astype(F32)
    steps = jnp.log(nf / max_exact) / math.log(REL_MAX_DISTANCE / max_exact) * (half - max_exact)
    large = max_exact
    for level in range(1, half - max_exact):
        large = large + jnp.where(steps >= level - 1e-4, 1, 0)
    return ret + jnp.where(n < max_exact, n, large)


def _bias_kernel(tab_ref, band_ref):
    r = lax.broadcasted_iota(I32, (CHUNK, KEY_TILE), 0)
    c = lax.broadcasted_iota(I32, (CHUNK, KEY_TILE), 1)
    near_bucket = _t5_bucket(c - FRONT_PAD - r)
    far_bucket = _t5_bucket(jnp.full((CHUNK, KEY_TILE), -(FRONT_PAD + 1), I32))
    for h in range(HEADS):
        near = jnp.zeros((CHUNK, KEY_TILE), F32)
        far = jnp.zeros((CHUNK, KEY_TILE), F32)
        for j in range(REL_BUCKETS):
            t = tab_ref[j, h]
            near = jnp.where(near_bucket == j, t, near)
            far = jnp.where(far_bucket == j, t, far)
        band_ref[h * CHUNK:(h + 1) * CHUNK, :] = near - far


def _bias_tiles(rel_bias):
    return pl.pallas_call(
        _bias_kernel,
        in_specs=[pl.BlockSpec(memory_space=pltpu.SMEM)],
        out_shape=jax.ShapeDtypeStruct((HEADS * CHUNK, KEY_TILE), F32),
        name="bias_tiles",
    )(rel_bias)


def _seg_sum(x):
    rows = x.shape[0]
    li = lax.broadcasted_iota(I32, (LANES, LANES), 0) // HEAD_DIM
    lj = lax.broadcasted_iota(I32, (LANES, LANES), 1) // HEAD_DIM
    ones = (li == lj).astype(BF16)
    xs = jnp.concatenate([x[:, g * LANES:(g + 1) * LANES] for g in range(HEAD_PAIRS)], axis=0)
    y = _mm(xs.astype(BF16), ones)
    return jnp.concatenate([y[g * rows:(g + 1) * rows] for g in range(HEAD_PAIRS)], axis=1)


def _rwkv_kernel(z_ref, mu_ref, w0_ref, wl_ref, a0_ref, g2_ref, kk_ref, ka_ref, rk_ref, lng_ref, lnb_ref,
                 y_ref, prev_s, st_s, *, nb):
    rows = nb * CHUNK

    @pl.when(pl.program_id(1) == 0)
    def _():
        prev_s[...] = jnp.zeros_like(prev_s)
        st_s[...] = jnp.zeros_like(st_s)

    z = z_ref[...].reshape(rows, RWKV_COLS)
    rolled = pltpu.roll(z, 1, axis=0)
    first = lax.broadcasted_iota(I32, (8, RWKV_COLS), 0) == 0
    pieces = []
    for bb in range(nb):
        base = bb * CHUNK
        pieces += [jnp.where(first, prev_s[bb], rolled[base:base + 8]), rolled[base + 8:base + CHUNK]]
        prev_s[bb] = z[base + CHUNK - 1:base + CHUNK, :]
    zp = jnp.concatenate(pieces, axis=0)
    zs = z + (zp - z) * mu_ref[...]
    r = zs[:, 0:WIDTH]
    k = zs[:, WIDTH:2 * WIDTH]
    v = zs[:, 2 * WIDTH:3 * WIDTH]
    zl = zs[:, 3 * WIDTH:3 * WIDTH + LANES]
    zg = zs[:, 3 * WIDTH + LANES:]

    lane = lax.broadcasted_iota(I32, zl.shape, 1)
    zl = jnp.where(lane < DECAY_RANK, jnp.tanh(zl), zl)
    la = _mm(zl.astype(BF16), wl_ref[...])
    ld = -math.exp(-0.5) * _sigmoid(w0_ref[...] + la[:, :WIDTH])
    a = _sigmoid(a0_ref[...] + la[:, WIDTH:])
    gate = _mm(_sigmoid(zg).astype(BF16), g2_ref[...])

    kk = k * kk_ref[...]
    kk = kk * jnp.minimum(lax.rsqrt(_seg_sum(kk * kk)), 1e12)
    kmod = k * (1.0 + (a - 1.0) * ka_ref[...])
    bvec = kk * a

    ti = lax.broadcasted_iota(I32, (rows, rows), 0)
    tj = lax.broadcasted_iota(I32, (rows, rows), 1)
    tri = ((ti >= tj) & (ti // CHUNK == tj // CHUNK)).astype(BF16)
    l1, l2 = _split2(ld)
    cum = _mm(tri, l1) + _mm(tri, l2)
    cum_x = cum - ld
    row_w = lax.broadcasted_iota(I32, (rows, WIDTH), 0)
    cref = cum[CHUNK // 2 - 1:CHUNK // 2, :]
    clast = cum[CHUNK - 1:CHUNK, :]
    for bb in range(1, nb):
        mid = bb * CHUNK + CHUNK // 2 - 1
        cref = jnp.where(row_w >= bb * CHUNK, cum[mid:mid + 1, :], cref)
        clast = jnp.where(row_w >= bb * CHUNK, cum[(bb + 1) * CHUNK - 1:(bb + 1) * CHUNK, :], clast)
    e_neg = jnp.exp(cref - cum)
    rt = r * jnp.exp(cum - cref)
    at = -kk * jnp.exp(cum_x - cref)
    kt = kmod * e_neg
    bt = bvec * e_neg
    r0 = r * jnp.exp(cum)
    a0v = -kk * jnp.exp(cum_x)
    e_end = jnp.exp(clast - cum)
    kc = kmod * e_end
    bc = bvec * e_end
    gw = GROUP * HEAD_DIM
    lane_h = lax.broadcasted_iota(I32, (CHUNK, gw), 1) // HEAD_DIM
    src = lax.broadcasted_iota(I32, (CHUNK, gw), 1) & (HEAD_DIM - 1)
    tok = lax.broadcasted_iota(I32, (CHUNK, gw), 0)
    strict = tok > src
    incl = tok >= src
    eye = (tok == src).astype(F32)
    diag_blk = (lax.broadcasted_iota(I32, (gw, gw), 0) // HEAD_DIM) == (lax.broadcasted_iota(I32, (gw, gw), 1) // HEAD_DIM)

    def bdiag(x):
        xb = x.astype(BF16)
        zero = jnp.zeros_like(xb)
        return jnp.concatenate([jnp.where(lane_h == h, xb, zero) for h in range(GROUP)], axis=0)

    all_units = [(bb, g, slice(bb * CHUNK, (bb + 1) * CHUNK), slice(g * gw, (g + 1) * gw))
                 for bb in range(nb) for g in range(HEADS // GROUP)]

    def run_wave(units):
        cut = lambda x: [x[rs, gs] for _, _, rs, gs in units]
        at_w, rt_w, bt_w, kt_w, v_w = cut(at), cut(rt), cut(bt), cut(kt), cut(v)
        lhs = [jnp.concatenate([a_, r_], axis=0).astype(BF16) for a_, r_ in zip(at_w, rt_w)]
        ab_rb = [_nt(l_, bdiag(b_)) for l_, b_ in zip(lhs, bt_w)]
        ak_rk = [_nt(l_, bdiag(k_)) for l_, k_ in zip(lhs, kt_w)]
        low = [jnp.where(strict, x[:CHUNK], 0.0) for x in ab_rb]
        a_ak = [jnp.where(strict, x[:CHUNK], 0.0).astype(BF16) for x in ak_rk]
        a_rb = [jnp.where(incl, x[CHUNK:], 0.0).astype(BF16) for x in ab_rb]
        a_rk = [jnp.where(incl, x[CHUNK:], 0.0).astype(BF16) for x in ak_rk]
        v_bd = [bdiag(x) for x in v_w]
        s_f = [st_s[bb, g] for bb, g, _, _ in units]
        s_b = [s.astype(BF16) for s in s_f]
        x0 = [_nt(x.astype(BF16), s) for x, s in zip(cut(a0v), s_b)]
        o_st = [_nt(x.astype(BF16), s) for x, s in zip(cut(r0), s_b)]
        yk = [_mm(a_, vb) for a_, vb in zip(a_ak, v_bd)]
        t = [eye + x for x in low]
        p = low
        p_bd = [bdiag(x) for x in p]
        for _ in range(int(math.log2(CHUNK)) - 1):
            p = [_mm(x.astype(BF16), xb) for x, xb in zip(p, p_bd)]
            p_bd = [bdiag(x) for x in p]
            t = [t_ + _mm(t_.astype(BF16), xb) for t_, xb in zip(t, p_bd)]
        u = [_mm(t_.astype(BF16), bdiag(a_ + b_)) for t_, a_, b_ in zip(t, x0, yk)]
        o_w = [a_ + _mm(rb, bdiag(u_)) + _mm(rk, vb) for a_, rb, u_, rk, vb in zip(o_st, a_rb, u, a_rk, v_bd)]
        uv = [jnp.concatenate([u_, v_], axis=0).astype(BF16) for u_, v_ in zip(u, v_w)]
        bk = [jnp.concatenate([b_, k_], axis=0).astype(BF16) for b_, k_ in zip(cut(bc), cut(kc))]
        for (bb, gi, _, gs), s, w, x in zip(units, s_f, uv, bk):
            pc = jnp.exp(cum[(bb + 1) * CHUNK - 1:(bb + 1) * CHUNK, gs])
            st_s[bb, gi] = s * pc + jnp.where(diag_blk, _tn(w, x), 0.0)
        return o_w

    o_w = []
    for w0 in range(0, len(all_units), WAVE):
        o_w += run_wave(all_units[w0:w0 + WAVE])

    groups = HEADS // GROUP
    o = jnp.concatenate([jnp.concatenate(o_w[bb * groups:(bb + 1) * groups], axis=1) for bb in range(nb)], axis=0)
    inv_n = 1.0 / HEAD_DIM
    mean = _seg_sum(o) * inv_n
    d = o - mean
    var = _seg_sum(d * d) * inv_n
    on = d * lax.rsqrt(var + GN_EPS) * lng_ref[...] + lnb_ref[...]
    bonus = _seg_sum(r * kmod * rk_ref[...]) * v
    y_ref[...] = ((on + bonus) * gate).astype(BF16).reshape(nb, CHUNK, WIDTH)


def _rwkv(zr, mu, w0, wl, a0, g2, k_k, k_a, r_k, ln_g, ln_b, nb_max=8):
    b, s, _ = zr.shape
    nb = max(n for n in range(1, nb_max + 1) if b % n == 0)
    full = lambda a: pl.BlockSpec(a.shape, lambda i, j: (0,) * a.ndim)
    params = (mu, w0, wl, a0, g2, k_k, k_a, r_k, ln_g, ln_b)
    return pl.pallas_call(
        functools.partial(_rwkv_kernel, nb=nb),
        grid=(b // nb, s // CHUNK),
        in_specs=[pl.BlockSpec((nb, CHUNK, RWKV_COLS), lambda i, j: (i, j, 0))] + [full(a) for a in params],
        out_specs=pl.BlockSpec((nb, CHUNK, WIDTH), lambda i, j: (i, j, 0)),
        out_shape=jax.ShapeDtypeStruct((b, s, WIDTH), BF16),
        scratch_shapes=[
            pltpu.VMEM((nb, 1, RWKV_COLS), F32),
            pltpu.VMEM((nb, HEADS // GROUP, GROUP * HEAD_DIM, GROUP * HEAD_DIM), F32),
        ],
        compiler_params=pltpu.CompilerParams(dimension_semantics=("arbitrary", "arbitrary"),
                                             vmem_limit_bytes=VMEM_LIMIT),
        name="rwkv",
    )(zr, *params)


def _key_to_float(key):
    return lax.bitcast_convert_type(key ^ ((key >> 31) & INT_MAX), F32)


def _dsa_kernel(qi_ref, q_ref, kiw_ref, k_ref, v_ref, band_ref, lng_ref, lnb_ref, o_ref,
                ki3_s, lq_s, wb_s, key_s, keyt_s, thr_s, jlim_s, lg_s, mx_s, mb_s, sm_s, pv_s, *, topk, nb):
    i = pl.program_id(1)
    sp = kiw_ref.shape[1]
    n_far = (i + 1) // 4
    n_tiles = n_far + 1
    far_lim = CHUNK * i - FRONT_PAD
    lane1 = lax.broadcasted_iota(I32, (CHUNK, LANES), 1)
    halves = [slice(0, LANES), slice(LANES, KEY_TILE)]
    n_pairs = (nb + 1) // 2
    kf = float(topk)

    def tile_pos0(c):
        return jnp.where(c < n_far, c * KEY_TILE, far_lim)

    def tile_end(c):
        return jnp.where(c < n_far, far_lim, CHUNK * (i + 1))

    def tile_row0(c):
        return pl.multiple_of(FRONT_PAD + tile_pos0(c), CHUNK)

    def half_pos(c, half):
        return tile_pos0(c) + half * LANES + lane1

    @pl.when(i == 0)
    def _():
        rows = 320 if sp % 320 == 0 else CHUNK

        def body(t, carry):
            bb = t // (sp // rows)
            r0 = pl.multiple_of((t % (sp // rows)) * rows, CHUNK)
            ki = kiw_ref[bb, pl.ds(r0, rows), :][:, :IDX_DIM]
            mean = jnp.mean(ki, axis=-1, keepdims=True)
            var = jnp.mean(jnp.square(ki - mean), axis=-1, keepdims=True)
            kn = (ki - mean) * lax.rsqrt(var + LN_EPS) * lng_ref[...] + lnb_ref[...]
            hi, lo = _split2(kn)
            ki3_s[bb, pl.ds(r0, rows), :] = jnp.concatenate([hi, lo, hi], axis=1)
            return carry

        lax.fori_loop(0, nb * (sp // rows), body, 0)

    def stage_queries(bb, carry):
        qi = qi_ref[:, bb].reshape(IDX_HEADS * CHUNK, IDX_DIM)
        qh, ql = _split2(qi)
        lq_s[bb] = jnp.concatenate([qh, qh, ql], axis=1)
        wq = kiw_ref[bb, pl.ds(pl.multiple_of(FRONT_PAD + CHUNK * i, CHUNK), CHUNK), :]
        wscale = IDX_HEADS ** -0.5 * IDX_DIM ** -0.5
        wcol = jnp.concatenate([wq[:, IDX_DIM + h:IDX_DIM + h + 1] for h in range(IDX_HEADS)], axis=0) * wscale
        wb_s[bb] = jnp.broadcast_to(wcol, (IDX_HEADS * CHUNK, LANES))
        return carry

    lax.fori_loop(0, nb, stage_queries, 0)

    def score_tile(c, carry):
        r0 = tile_row0(c)
        end = tile_end(c)
        xs = [_nt(lq_s[bb], ki3_s[bb, pl.ds(r0, KEY_TILE), :]) for bb in range(nb)]
        keys = []
        for bb in range(nb):
            wb = wb_s[bb]
            parts = []
            for half, hs_ in enumerate(halves):
                act = jnp.maximum(xs[bb][:, hs_], 0.0) * wb
                sc = act[0:CHUNK]
                for h in range(1, IDX_HEADS):
                    sc = sc + act[h * CHUNK:(h + 1) * CHUNK]
                pos = half_pos(c, half)
                valid = (pos >= 0) & (pos < end)
                parts.append(jnp.where(valid, sc, -jnp.inf))
            keys.append(jnp.concatenate(parts, axis=1))
            key_s[bb, c] = keys[bb]
        for pr in range(n_pairs):
            both = jnp.concatenate([keys[2 * pr], keys[min(2 * pr + 1, nb - 1)]], axis=0)
            keyt_s[pr, c] = both.T
        return carry

    lax.fori_loop(0, n_tiles, score_tile, 0)

    @pl.when(n_tiles % 2 == 1)
    def _():
        for pr in range(n_pairs):
            keyt_s[pr, n_tiles] = jnp.full((KEY_TILE, 2 * CHUNK), -jnp.inf, F32)

    thr_s[...] = jnp.full(thr_s.shape, -jnp.inf, F32)
    jlim_s[...] = jnp.full(jlim_s.shape, INT_MAX, I32)

    part = 32

    def count(cands, above):
        wide = [jnp.broadcast_to(cd, (part, LANES)) for cd in cands]

        def body(c2, accs):
            out = list(accs)
            for c in (2 * c2, 2 * c2 + 1):
                for pr in range(n_pairs):
                    sc = keyt_s[pr, c].reshape(KEY_TILE // part, part, LANES)
                    out[pr] = out[pr] + jnp.sum(jnp.where(above(sc, wide[pr]), 1.0, 0.0), axis=0)
            return tuple(out)

        accs = lax.fori_loop(0, (n_tiles + 1) // 2, body,
                             tuple(jnp.zeros((part, LANES), F32) for _ in range(n_pairs)))
        return [jnp.sum(a, axis=0, keepdims=True) for a in accs]

    def count_ge(keys):
        return count([_key_to_float(k) for k in keys], lambda s, c: s >= c)

    @pl.when(CHUNK * (i + 1) > topk)
    def _():
        n_all = float(KEY_TILE) * n_tiles.astype(F32)
        n_pos = count_ge([jnp.zeros((1, LANES), I32)] * n_pairs)
        ans0 = tuple(jnp.where(n >= kf, 0, INT_MIN).astype(I32) for n in n_pos)
        cnt0 = tuple(jnp.where(n >= kf, n, n_all) for n in n_pos)

        def one_bit(t, carry):
            ans, n_ans = carry
            bit = lax.shift_left(jnp.int32(1), 30 - t)
            cands = [a | bit for a in ans]
            cnts = count_ge(cands)
            take = [n >= kf for n in cnts]
            return (tuple(jnp.where(tk, cd, a) for tk, cd, a in zip(take, cands, ans)),
                    tuple(jnp.where(tk, n, na) for tk, n, na in zip(take, cnts, n_ans)))

        ans, n_ans = lax.fori_loop(0, 31, one_bit, (ans0, cnt0))
        def to_columns(rows_, dst):
            for pr in range(n_pairs):
                col = jnp.broadcast_to(rows_[pr], (LANES, LANES)).T
                dst[2 * pr] = col[:CHUNK]
                if 2 * pr + 1 < nb:
                    dst[2 * pr + 1] = col[CHUNK:]

        kth = [_key_to_float(a) for a in ans]
        to_columns(kth, thr_s)
        most = n_ans[0]
        for pr in range(1, n_pairs):
            most = jnp.maximum(most, n_ans[pr])

        @pl.when(jnp.max(most) > kf)
        def _():
            n_gt = count(kth, lambda s, c: s > c)
            need = [kf - n for n in n_gt]
            ans_w = [jnp.broadcast_to(a, (KEY_TILE, LANES)) for a in kth]
            krow = lax.broadcasted_iota(I32, (KEY_TILE, LANES), 0)
            tri = (lax.broadcasted_iota(I32, (KEY_TILE, KEY_TILE), 0)
                   >= lax.broadcasted_iota(I32, (KEY_TILE, KEY_TILE), 1)).astype(BF16)

            def body(c, carry):
                seen, last = carry
                pos = tile_pos0(c) + krow
                seen_out, last_out = [], []
                for pr in range(n_pairs):
                    tie = keyt_s[pr, c] == ans_w[pr]
                    rank = _mm(tri, jnp.where(tie, 1.0, 0.0).astype(BF16)) + seen[pr]
                    kept = jnp.where(tie & (rank <= need[pr]), pos, -1)
                    last_out.append(jnp.maximum(last[pr], jnp.max(kept, axis=0, keepdims=True)))
                    seen_out.append(rank[KEY_TILE - 1:KEY_TILE, :])
                return tuple(seen_out), tuple(last_out)

            init = (tuple(jnp.zeros((1, LANES), F32) for _ in range(n_pairs)),
                    tuple(jnp.full((1, LANES), -1, I32) for _ in range(n_pairs)))
            _, last = lax.fori_loop(0, n_tiles, body, init)
            to_columns(last, jlim_s)

    pair_lane_head = lane1 // HEAD_DIM

    cw = lg_s.shape[0]

    def phase_c(step, carry):
        bbs = [step * cw + j for j in range(cw)]
        zero = jnp.zeros((CHUNK, LANES), BF16)
        qm = []
        for bb in bbs:
            qs = q_ref[bb] * jnp.asarray(HEAD_DIM ** -0.5, BF16)
            qm.append([jnp.concatenate([jnp.where(pair_lane_head == 0, qs[:, g * LANES:(g + 1) * LANES], zero),
                                        jnp.where(pair_lane_head == 1, qs[:, g * LANES:(g + 1) * LANES], zero)],
                                       axis=0) for g in range(HEAD_PAIRS)])
        thr = [thr_s[bb] for bb in bbs]
        jlim = [jlim_s[bb] for bb in bbs]
        mx_s[...] = jnp.full(mx_s.shape, NEG_INF, F32)

        def logits(c, bias):
            r0 = tile_row0(c)
            end = tile_end(c)
            kt = [k_ref[bb, pl.ds(r0, KEY_TILE), :] for bb in bbs]
            s = [[_nt(qm[j][g], kt[j][:, g * LANES:(g + 1) * LANES]) for g in range(HEAD_PAIRS)]
                 for j in range(cw)]
            for j, bb in enumerate(bbs):
                sel = []
                for half, hs_ in enumerate(halves):
                    key = key_s[bb, c, :, hs_]
                    pos = half_pos(c, half)
                    sel.append((pos >= 0) & (pos < end) & ((key > thr[j]) | ((key == thr[j]) & (pos <= jlim[j]))))
                for g in range(HEAD_PAIRS):
                    sg = s[j][g]
                    if bias is not None:
                        sg = sg + bias[g * 2 * CHUNK:(g + 1) * 2 * CHUNK]
                    for hh in range(2):
                        rows = slice((2 * g + hh) * CHUNK, (2 * g + hh + 1) * CHUNK)
                        sub = sg[hh * CHUNK:(hh + 1) * CHUNK]
                        m0 = jnp.where(sel[0], sub[:, halves[0]], NEG_INF)
                        m1 = jnp.where(sel[1], sub[:, halves[1]], NEG_INF)
                        lg_s[j, c, rows, halves[0]] = m0
                        lg_s[j, c, rows, halves[1]] = m1
                        mx_s[j, rows] = jnp.maximum(mx_s[j, rows], jnp.maximum(m0, m1))

        def far_logits(c, carry):
            logits(c, None)
            return carry

        lax.fori_loop(0, n_far, far_logits, 0)
        logits(n_far, band_ref[...])

        for j in range(cw):
            mb_s[j] = jnp.broadcast_to(jnp.max(mx_s[j], axis=1, keepdims=True), mb_s.shape[1:])
        sm_s[...] = jnp.zeros_like(sm_s)
        pv_s[...] = jnp.zeros_like(pv_s)

        def weigh(c, carry):
            r0 = tile_row0(c)
            vt = [v_ref[bb, pl.ds(r0, KEY_TILE), :] for bb in bbs]
            for g in range(HEAD_PAIRS):
                rows = slice(g * 2 * CHUNK, (g + 1) * 2 * CHUNK)
                ps = []
                for j in range(cw):
                    m = mb_s[j, rows]
                    p0 = jnp.exp(lg_s[j, c, rows, halves[0]] - m)
                    p1 = jnp.exp(lg_s[j, c, rows, halves[1]] - m)
                    sm_s[j, rows] = sm_s[j, rows] + (p0 + p1)
                    ps.append(jnp.concatenate([p0, p1], axis=1).astype(BF16))
                for j in range(cw):
                    pv_s[j, rows] = pv_s[j, rows] + _mm(ps[j], vt[j][:, g * LANES:(g + 1) * LANES])
            return carry

        lax.fori_loop(0, n_tiles, weigh, 0)

        for j, bb in enumerate(bbs):
            outn = pv_s[j] / jnp.sum(sm_s[j], axis=1, keepdims=True)
            cols = []
            for g in range(HEAD_PAIRS):
                top = outn[(2 * g) * CHUNK:(2 * g + 1) * CHUNK]
                bot = outn[(2 * g + 1) * CHUNK:(2 * g + 2) * CHUNK]
                cols.append(jnp.where(pair_lane_head == 0, top, bot))
            o_ref[bb] = jnp.concatenate(cols, axis=1).astype(BF16)
        return carry

    lax.fori_loop(0, nb // cw, phase_c, 0)


def _dsa(qi_t, q, kiw_p, k_p, v_p, band, ln_g, ln_b):
    b, s, _ = q.shape
    sp = k_p.shape[1]
    topk = min(MAX_TOPK, s // 4)
    max_tiles = (s // CHUNK) // 4 + 1
    nb = 4 if b % 4 == 0 else (2 if b % 2 == 0 else 1)
    cw = nb
    rows = HEADS * CHUNK
    full = lambda a: pl.BlockSpec(a.shape, lambda i, j: (0,) * a.ndim)
    per_group = lambda n: pl.BlockSpec((nb, sp, n), lambda i, j: (i, 0, 0), pipeline_mode=pl.Buffered(1))
    return pl.pallas_call(
        functools.partial(_dsa_kernel, topk=topk, nb=nb),
        grid=(b // nb, s // CHUNK),
        in_specs=[
            pl.BlockSpec((IDX_HEADS, nb, CHUNK, IDX_DIM), lambda i, j: (0, i, j, 0)),
            pl.BlockSpec((nb, CHUNK, WIDTH), lambda i, j: (i, j, 0)),
            per_group(LANES), per_group(WIDTH), per_group(WIDTH),
            full(band), full(ln_g), full(ln_b),
        ],
        out_specs=pl.BlockSpec((nb, CHUNK, WIDTH), lambda i, j: (i, j, 0)),
        out_shape=jax.ShapeDtypeStruct((b, s, WIDTH), BF16),
        scratch_shapes=[
            pltpu.VMEM((nb, sp, 3 * IDX_DIM), BF16),
            pltpu.VMEM((nb, rows, 3 * IDX_DIM), BF16),
            pltpu.VMEM((nb, rows, LANES), F32),
            pltpu.VMEM((nb, max_tiles, CHUNK, KEY_TILE), F32),
            pltpu.VMEM(((nb + 1) // 2, max_tiles + 1, KEY_TILE, 2 * CHUNK), F32),
            pltpu.VMEM((nb, CHUNK, LANES), F32),
            pltpu.VMEM((nb, CHUNK, LANES), I32),
            pltpu.VMEM((cw, max_tiles, rows, KEY_TILE), F32),
            pltpu.VMEM((cw, rows, LANES), F32),
            pltpu.VMEM((cw, rows, LANES), F32),
            pltpu.VMEM((cw, rows, LANES), F32),
            pltpu.VMEM((cw, rows, LANES), F32),
        ],
        compiler_params=pltpu.CompilerParams(dimension_semantics=("arbitrary", "arbitrary"),
                                             vmem_limit_bytes=VMEM_LIMIT),
        name="dsa",
    )(qi_t, q, kiw_p, k_p, v_p, band, ln_g, ln_b)


def _post_kernel(y1_ref, y2_ref, x_ref, p_ref, woa_ref, wob_ref, w1_ref, w2_ref, wp_ref, wg_ref,
                 g_mix_ref, g_f1_ref, g_f2_ref, g_p1_ref, g_p2_ref, o_ref):
    mix = _mm(y1_ref[...], woa_ref[...]) + _mm(y2_ref[...], wob_ref[...])
    x1 = x_ref[...] + _rms(mix, g_mix_ref[...])
    h = _rms(x1, g_f1_ref[...]).astype(BF16)
    hid = jnp.square(jnp.maximum(_mm(h, w1_ref[...]), 0.0)).astype(BF16)
    x2 = x1 + _rms(_mm(hid, w2_ref[...]), g_f2_ref[...])
    gate = _sigmoid(_mm(_rms(x2, g_p1_ref[...]).astype(BF16), wg_ref[...]))
    pe = _mm(p_ref[...].astype(BF16), wp_ref[...]) * gate
    o_ref[...] = x2 + _rms(pe, g_p2_ref[...])


def _post(y1, y2, x2d, p2d, woa, wob, w1, w2, wp, wg, gains, tm=512):
    m, d = x2d.shape
    row = lambda n: pl.BlockSpec((tm, n), lambda i: (i, 0))
    const = lambda a: pl.BlockSpec(a.shape, lambda i: (0, 0), pipeline_mode=pl.Buffered(1))
    weights = (woa, wob, w1, w2, wp, wg) + tuple(gains)
    return pl.pallas_call(
        _post_kernel,
        grid=(m // tm,),
        in_specs=[row(WIDTH), row(WIDTH), row(d), row(p2d.shape[1])] + [const(a) for a in weights],
        out_specs=row(d),
        out_shape=jax.ShapeDtypeStruct((m, d), F32),
        compiler_params=pltpu.CompilerParams(dimension_semantics=("arbitrary",), vmem_limit_bytes=VMEM_LIMIT),
        name="post",
    )(y1, y2, x2d, p2d, *weights)


def _layer(x, p, w_in, shift_mu, w0, w2r, a0, a2, g2, k_k, k_a, r_k, ln_g, ln_b, idx_g, idx_b, rel_bias,
           w_out, w_f1, w_f2, w_ple, w_gate, n_mix_pre, n_mix_post, n_ffn_pre, n_ffn_post, n_ple_pre, n_ple_post):
    b, s, d = x.shape
    m = b * s
    row = lambda a: a.reshape(1, -1).astype(F32)

    c = [0, WIDTH, WIDTH + DECAY_RANK, 2 * WIDTH + DECAY_RANK, 3 * WIDTH + DECAY_RANK,
         3 * WIDTH + DECAY_RANK + A_RANK, RWKV_COLS]
    order = [(c[0], c[1]), (c[2], c[3]), (c[3], c[4]), (c[1], c[2]), (c[4], c[5]), (c[5], c[6])]
    reorder = lambda a: jnp.concatenate([a[..., lo:hi] for lo, hi in order], axis=-1)
    w_main = jnp.concatenate([reorder(w_in[:, :RWKV_COLS]), w_in[:, RWKV_COLS:RWKV_COLS + 3 * WIDTH]],
                             axis=1).astype(BF16)
    w_idx = w_in[:, RWKV_COLS + 3 * WIDTH:]
    w_idx = jnp.pad(w_idx, ((0, 0), (0, N_IDX - w_idx.shape[1])))
    w_idx_hi = w_idx.astype(BF16)
    w_idx_lo = (w_idx - w_idx_hi.astype(F32)).astype(BF16)

    zr, q, k, v, qi, kiw = _in_proj(x.reshape(m, d), row(n_mix_pre), w_main, w_idx_hi, w_idx_lo)

    zeros = jnp.zeros((DECAY_RANK, WIDTH), F32)
    w_lora = jnp.concatenate([jnp.concatenate([w2r, zeros], axis=1),
                              jnp.concatenate([zeros, a2], axis=1)], axis=0).astype(BF16)
    y_rwkv = _rwkv(zr.reshape(b, s, RWKV_COLS), row(reorder(shift_mu)), row(w0), w_lora, row(a0), g2.astype(BF16),
                   row(k_k), row(k_a), row(r_k), row(ln_g), row(ln_b))

    band = _bias_tiles(rel_bias.astype(F32))
    pad = lambda a: jnp.pad(a.reshape(b, s, -1), ((0, 0), (FRONT_PAD, BACK_PAD), (0, 0)))
    y_dsa = _dsa(qi.reshape(IDX_HEADS, b, s, IDX_DIM), q.reshape(b, s, WIDTH), pad(kiw), pad(k), pad(v), band,
                 row(idx_g), row(idx_b))

    gains = tuple(row(a) for a in (n_mix_post, n_ffn_pre, n_ffn_post, n_ple_pre, n_ple_post))
    out = _post(y_rwkv.reshape(m, WIDTH), y_dsa.reshape(m, WIDTH), x.reshape(m, d), p.reshape(m, -1),
                w_out[:WIDTH].astype(BF16), w_out[WIDTH:].astype(BF16), w_f1.astype(BF16), w_f2.astype(BF16),
                w_ple.astype(BF16), w_gate.astype(BF16), gains)
    return out.reshape(b, s, d)


def kernel(x, p, w_in, shift_mu, rwkv_w0, rwkv_w2, rwkv_a0, rwkv_a2, rwkv_g2, rwkv_k_k, rwkv_k_a, rwkv_r_k, rwkv_ln_g, rwkv_ln_b, idx_k_ln_g, idx_k_ln_b, rel_bias, w_out, w_ffn1, w_ffn2, w_ple, w_ple_gate, norm_mix_pre, norm_mix_post, norm_ffn_pre, norm_ffn_post, norm_ple_pre, norm_ple_post):
    for i in range(p.shape[0]):
        x = _layer(x, p[i], w_in[i], shift_mu[i], rwkv_w0[i], rwkv_w2[i], rwkv_a0[i], rwkv_a2[i], rwkv_g2[i],
                   rwkv_k_k[i], rwkv_k_a[i], rwkv_r_k[i], rwkv_ln_g[i], rwkv_ln_b[i], idx_k_ln_g[i],
                   idx_k_ln_b[i], rel_bias, w_out[i], w_ffn1[i], w_ffn2[i], w_ple[i], w_ple_gate[i],
                   norm_mix_pre[i], norm_mix_post[i], norm_ffn_pre[i], norm_ffn_post[i], norm_ple_pre[i],
                   norm_ple_post[i])
    return x
```

```python
import functools
import math

import jax
import jax.numpy as jnp
from jax import lax
from jax.experimental import pallas as pl
from jax.experimental.pallas import tpu as pltpu

F32 = jnp.float32
BF16 = jnp.bfloat16
I32 = jnp.int32

CHUNK = 64
HEADS = 8
HEAD_DIM = 64
WIDTH = HEADS * HEAD_DIM
DECAY_RANK = 64
A_RANK = 64
GATE_RANK = 128
GN_EPS = 64e-5
IDX_HEADS = 8
IDX_DIM = 64
MAX_TOPK = 256
REL_BUCKETS = 32
REL_MAX_DISTANCE = 128
NORM_EPS = 1e-6
LN_EPS = 1e-6
NEG_INF = -1e30
RWKV_COLS = 1792
N_MAIN = RWKV_COLS + 3 * WIDTH
N_IDX = 640

LANES = 128
KEY_TILE = 256
HEAD_PAIRS = WIDTH // LANES
GROUP = 4
WAVE = 8
FRONT_PAD = 2 * CHUNK
BACK_PAD = CHUNK
VMEM_LIMIT = 62 * 1024 * 1024

INT_MIN = -(2**31)
INT_MAX = 2**31 - 1


def _nt(a, b):
    return lax.dot_general(a, b, (((1,), (1,)), ((), ())), preferred_element_type=F32)


def _tn(a, b):
    return lax.dot_general(a, b, (((0,), (0,)), ((), ())), preferred_element_type=F32)


def _mm(a, b):
    return jnp.dot(a, b, preferred_element_type=F32)


def _split2(x):
    hi = x.astype(BF16)
    lo = (x - hi.astype(F32)).astype(BF16)
    return hi, lo


def _rms(x, g):
    return x * lax.rsqrt(jnp.mean(x * x, axis=-1, keepdims=True) + NORM_EPS) * g


def _sigmoid(x):
    return 1.0 / (1.0 + jnp.exp(-x))


def _in_proj_kernel(x_ref, g_ref, wm_ref, wih_ref, wil_ref, zr_ref, q_ref, k_ref, v_ref, qi_ref, kiw_ref):
    h = _rms(x_ref[...], g_ref[...])
    hh, hl = _split2(h)
    zm = _mm(hh, wm_ref[...])
    zr_ref[...] = zm[:, :RWKV_COLS]
    q_ref[...] = zm[:, RWKV_COLS:RWKV_COLS + WIDTH].astype(BF16)
    k_ref[...] = zm[:, RWKV_COLS + WIDTH:RWKV_COLS + 2 * WIDTH].astype(BF16)
    v_ref[...] = zm[:, RWKV_COLS + 2 * WIDTH:].astype(BF16)
    zi = _mm(hh, wih_ref[...]) + _mm(hh, wil_ref[...]) + _mm(hl, wih_ref[...])
    qh, ql = _split2(zi[:, :WIDTH])
    for h in range(IDX_HEADS):
        sl = slice(h * IDX_DIM, (h + 1) * IDX_DIM)
        qi_ref[h] = jnp.concatenate([qh[:, sl], qh[:, sl], ql[:, sl]], axis=1)
    kiw_ref[...] = zi[:, WIDTH:]


def _in_proj(x2, g, wm, wih, wil, tm=512):
    m, d = x2.shape
    row = lambda n: pl.BlockSpec((tm, n), lambda i: (i, 0))
    full = lambda a: pl.BlockSpec(a.shape, lambda i: (0, 0))
    return pl.pallas_call(
        _in_proj_kernel,
        grid=(m // tm,),
        in_specs=[row(d), full(g), full(wm), full(wih), full(wil)],
        out_specs=[row(RWKV_COLS), row(WIDTH), row(WIDTH), row(WIDTH),
                   pl.BlockSpec((IDX_HEADS, tm, 3 * IDX_DIM), lambda i: (0, i, 0)), row(LANES)],
        out_shape=[
            jax.ShapeDtypeStruct((m, RWKV_COLS), F32),
            jax.ShapeDtypeStruct((m, WIDTH), BF16),
            jax.ShapeDtypeStruct((m, WIDTH), BF16),
            jax.ShapeDtypeStruct((m, WIDTH), BF16),
            jax.ShapeDtypeStruct((IDX_HEADS, m, 3 * IDX_DIM), BF16),
            jax.ShapeDtypeStruct((m, LANES), F32),
        ],
        compiler_params=pltpu.CompilerParams(dimension_semantics=("arbitrary",), vmem_limit_bytes=VMEM_LIMIT),
        name="in_proj",
    )(x2, g, wm, wih, wil)


def _t5_bucket(rel):
    half = REL_BUCKETS // 2
    max_exact = half // 2
    ret = jnp.where(rel > 0, half, 0)
    n = jnp.abs(rel)
    nf = jnp.maximum(n, 1).astype(F32)
    steps = jnp.log(nf / max_exact) / math.log(REL_MAX_DISTANCE / max_exact) * (half - max_exact)
    large = max_exact
    for level in range(1, half - max_exact):
        large = large + jnp.where(steps >= level - 1e-4, 1, 0)
    return ret + jnp.where(n < max_exact, n, large)


def _bias_kernel(tab_ref, band_ref):
    r = lax.broadcasted_iota(I32, (CHUNK, KEY_TILE), 0)
    c = lax.broadcasted_iota(I32, (CHUNK, KEY_TILE), 1)
    near_bucket = _t5_bucket(c - FRONT_PAD - r)
    far_bucket = _t5_bucket(jnp.full((CHUNK, KEY_TILE), -(FRONT_PAD + 1), I32))
    for h in range(HEADS):
        near = jnp.zeros((CHUNK, KEY_TILE), F32)
        far = jnp.zeros((CHUNK, KEY_TILE), F32)
        for j in range(REL_BUCKETS):
            t = tab_ref[j, h]
            near = jnp.where(near_bucket == j, t, near)
            far = jnp.where(far_bucket == j, t, far)
        band_ref[h * CHUNK:(h + 1) * CHUNK, :] = near - far


def _bias_tiles(rel_bias):
    return pl.pallas_call(
        _bias_kernel,
        in_specs=[pl.BlockSpec(memory_space=pltpu.SMEM)],
        out_shape=jax.ShapeDtypeStruct((HEADS * CHUNK, KEY_TILE), F32),
        name="bias_tiles",
    )(rel_bias)


def _seg_sum(x):
    rows = x.shape[0]
    li = lax.broadcasted_iota(I32, (LANES, LANES), 0) // HEAD_DIM
    lj = lax.broadcasted_iota(I32, (LANES, LANES), 1) // HEAD_DIM
    ones = (li == lj).astype(BF16)
    xs = jnp.concatenate([x[:, g * LANES:(g + 1) * LANES] for g in range(HEAD_PAIRS)], axis=0)
    y = _mm(xs.astype(BF16), ones)
    return jnp.concatenate([y[g * rows:(g + 1) * rows] for g in range(HEAD_PAIRS)], axis=1)


def _rwkv_kernel(z_ref, mu_ref, w0_ref, wl_ref, a0_ref, g2_ref, kk_ref, ka_ref, rk_ref, lng_ref, lnb_ref,
                 y_ref, prev_s, st_s, *, nb):
    rows = nb * CHUNK

    @pl.when(pl.program_id(1) == 0)
    def _():
        prev_s[...] = jnp.zeros_like(prev_s)
        st_s[...] = jnp.zeros_like(st_s)

    z = z_ref[...].reshape(rows, RWKV_COLS)
    rolled = pltpu.roll(z, 1, axis=0)
    first = lax.broadcasted_iota(I32, (8, RWKV_COLS), 0) == 0
    pieces = []
    for bb in range(nb):
        base = bb * CHUNK
        pieces += [jnp.where(first, prev_s[bb], rolled[base:base + 8]), rolled[base + 8:base + CHUNK]]
        prev_s[bb] = z[base + CHUNK - 1:base + CHUNK, :]
    zp = jnp.concatenate(pieces, axis=0)
    zs = z + (zp - z) * mu_ref[...]
    r = zs[:, 0:WIDTH]
    k = zs[:, WIDTH:2 * WIDTH]
    v = zs[:, 2 * WIDTH:3 * WIDTH]
    zl = zs[:, 3 * WIDTH:3 * WIDTH + LANES]
    zg = zs[:, 3 * WIDTH + LANES:]

    lane = lax.broadcasted_iota(I32, zl.shape, 1)
    zl = jnp.where(lane < DECAY_RANK, jnp.tanh(zl), zl)
    la = _mm(zl.astype(BF16), wl_ref[...])
    ld = -math.exp(-0.5) * _sigmoid(w0_ref[...] + la[:, :WIDTH])
    a = _sigmoid(a0_ref[...] + la[:, WIDTH:])
    gate = _mm(_sigmoid(zg).astype(BF16), g2_ref[...])

    kk = k * kk_ref[...]
    kk = kk * jnp.minimum(lax.rsqrt(_seg_sum(kk * kk)), 1e12)
    kmod = k * (1.0 + (a - 1.0) * ka_ref[...])
    bvec = kk * a

    ti = lax.broadcasted_iota(I32, (rows, rows), 0)
    tj = lax.broadcasted_iota(I32, (rows, rows), 1)
    tri = ((ti >= tj) & (ti // CHUNK == tj // CHUNK)).astype(BF16)
    l1, l2 = _split2(ld)
    cum = _mm(tri, l1) + _mm(tri, l2)
    cum_x = cum - ld
    row_w = lax.broadcasted_iota(I32, (rows, WIDTH), 0)
    cref = cum[CHUNK // 2 - 1:CHUNK // 2, :]
    clast = cum[CHUNK - 1:CHUNK, :]
    for bb in range(1, nb):
        mid = bb * CHUNK + CHUNK // 2 - 1
        cref = jnp.where(row_w >= bb * CHUNK, cum[mid:mid + 1, :], cref)
        clast = jnp.where(row_w >= bb * CHUNK, cum[(bb + 1) * CHUNK - 1:(bb + 1) * CHUNK, :], clast)
    e_neg = jnp.exp(cref - cum)
    rt = r * jnp.exp(cum - cref)
    at = -kk * jnp.exp(cum_x - cref)
    kt = kmod * e_neg
    bt = bvec * e_neg
    r0 = r * jnp.exp(cum)
    a0v = -kk * jnp.exp(cum_x)
    e_end = jnp.exp(clast - cum)
    kc = kmod * e_end
    bc = bvec * e_end
    gw = GROUP * HEAD_DIM
    lane_h = lax.broadcasted_iota(I32, (CHUNK, gw), 1) // HEAD_DIM
    src = lax.broadcasted_iota(I32, (CHUNK, gw), 1) & (HEAD_DIM - 1)
    tok = lax.broadcasted_iota(I32, (CHUNK, gw), 0)
    strict = tok > src
    incl = tok >= src
    eye = (tok == src).astype(F32)
    diag_blk = (lax.broadcasted_iota(I32, (gw, gw), 0) // HEAD_DIM) == (lax.broadcasted_iota(I32, (gw, gw), 1) // HEAD_DIM)

    def bdiag(x):
        xb = x.astype(BF16)
        zero = jnp.zeros_like(xb)
        return jnp.concatenate([jnp.where(lane_h == h, xb, zero) for h in range(GROUP)], axis=0)

    all_units = [(bb, g, slice(bb * CHUNK, (bb + 1) * CHUNK), slice(g * gw, (g + 1) * gw))
                 for bb in range(nb) for g in range(HEADS // GROUP)]

    def run_wave(units):
        cut = lambda x: [x[rs, gs] for _, _, rs, gs in units]
        at_w, rt_w, bt_w, kt_w, v_w = cut(at), cut(rt), cut(bt), cut(kt), cut(v)
        lhs = [jnp.concatenate([a_, r_], axis=0).astype(BF16) for a_, r_ in zip(at_w, rt_w)]
        ab_rb = [_nt(l_, bdiag(b_)) for l_, b_ in zip(lhs, bt_w)]
        ak_rk = [_nt(l_, bdiag(k_)) for l_, k_ in zip(lhs, kt_w)]
        low = [jnp.where(strict, x[:CHUNK], 0.0) for x in ab_rb]
        a_ak = [jnp.where(strict, x[:CHUNK], 0.0).astype(BF16) for x in ak_rk]
        a_rb = [jnp.where(incl, x[CHUNK:], 0.0).astype(BF16) for x in ab_rb]
        a_rk = [jnp.where(incl, x[CHUNK:], 0.0).astype(BF16) for x in ak_rk]
        v_bd = [bdiag(x) for x in v_w]
        s_f = [st_s[bb, g] for bb, g, _, _ in units]
        s_b = [s.astype(BF16) for s in s_f]
        x0 = [_nt(x.astype(BF16), s) for x, s in zip(cut(a0v), s_b)]
        o_st = [_nt(x.astype(BF16), s) for x, s in zip(cut(r0), s_b)]
        yk = [_mm(a_, vb) for a_, vb in zip(a_ak, v_bd)]
        t = [eye + x for x in low]
        p = low
        p_bd = [bdiag(x) for x in p]
        for _ in range(int(math.log2(CHUNK)) - 1):
            p = [_mm(x.astype(BF16), xb) for x, xb in zip(p, p_bd)]
            p_bd = [bdiag(x) for x in p]
            t = [t_ + _mm(t_.astype(BF16), xb) for t_, xb in zip(t, p_bd)]
        u = [_mm(t_.astype(BF16), bdiag(a_ + b_)) for t_, a_, b_ in zip(t, x0, yk)]
        o_w = [a_ + _mm(rb, bdiag(u_)) + _mm(rk, vb) for a_, rb, u_, rk, vb in zip(o_st, a_rb, u, a_rk, v_bd)]
        uv = [jnp.concatenate([u_, v_], axis=0).astype(BF16) for u_, v_ in zip(u, v_w)]
        bk = [jnp.concatenate([b_, k_], axis=0).astype(BF16) for b_, k_ in zip(cut(bc), cut(kc))]
        for (bb, gi, _, gs), s, w, x in zip(units, s_f, uv, bk):
            pc = jnp.exp(cum[(bb + 1) * CHUNK - 1:(bb + 1) * CHUNK, gs])
            st_s[bb, gi] = s * pc + jnp.where(diag_blk, _tn(w, x), 0.0)
        return o_w

    o_w = []
    for w0 in range(0, len(all_units), WAVE):
        o_w += run_wave(all_units[w0:w0 + WAVE])

    groups = HEADS // GROUP
    o = jnp.concatenate([jnp.concatenate(o_w[bb * groups:(bb + 1) * groups], axis=1) for bb in range(nb)], axis=0)
    inv_n = 1.0 / HEAD_DIM
    mean = _seg_sum(o) * inv_n
    d = o - mean
    var = _seg_sum(d * d) * inv_n
    on = d * lax.rsqrt(var + GN_EPS) * lng_ref[...] + lnb_ref[...]
    bonus = _seg_sum(r * kmod * rk_ref[...]) * v
    y_ref[...] = ((on + bonus) * gate).astype(BF16).reshape(nb, CHUNK, WIDTH)


def _rwkv(zr, mu, w0, wl, a0, g2, k_k, k_a, r_k, ln_g, ln_b, nb_max=8):
    b, s, _ = zr.shape
    nb = max(n for n in range(1, nb_max + 1) if b % n == 0)
    full = lambda a: pl.BlockSpec(a.shape, lambda i, j: (0,) * a.ndim)
    params = (mu, w0, wl, a0, g2, k_k, k_a, r_k, ln_g, ln_b)
    return pl.pallas_call(
        functools.partial(_rwkv_kernel, nb=nb),
        grid=(b // nb, s // CHUNK),
        in_specs=[pl.BlockSpec((nb, CHUNK, RWKV_COLS), lambda i, j: (i, j, 0))] + [full(a) for a in params],
        out_specs=pl.BlockSpec((nb, CHUNK, WIDTH), lambda i, j: (i, j, 0)),
        out_shape=jax.ShapeDtypeStruct((b, s, WIDTH), BF16),
        scratch_shapes=[
            pltpu.VMEM((nb, 1, RWKV_COLS), F32),
            pltpu.VMEM((nb, HEADS // GROUP, GROUP * HEAD_DIM, GROUP * HEAD_DIM), F32),
        ],
        compiler_params=pltpu.CompilerParams(dimension_semantics=("arbitrary", "arbitrary"),
                                             vmem_limit_bytes=VMEM_LIMIT),
        name="rwkv",
    )(zr, *params)


def _key_to_float(key):
    return lax.bitcast_convert_type(key ^ ((key >> 31) & INT_MAX), F32)


def _dsa_kernel(qi_ref, q_ref, kiw_ref, k_ref, v_ref, band_ref, lng_ref, lnb_ref, o_ref,
                ki3_s, wb_s, key_s, keyt_s, thr_s, jlim_s, lg_s, mx_s, mb_s, sm_s, pv_s, *, topk, nb):
    i = pl.program_id(1)
    sp = kiw_ref.shape[1]
    n_far = (i + 1) // 4
    n_tiles = n_far + 1
    far_lim = CHUNK * i - FRONT_PAD
    lane1 = lax.broadcasted_iota(I32, (CHUNK, LANES), 1)
    halves = [slice(0, LANES), slice(LANES, KEY_TILE)]
    n_pairs = (nb + 1) // 2
    kf = float(topk)

    def tile_pos0(c):
        return jnp.where(c < n_far, c * KEY_TILE, far_lim)

    def tile_end(c):
        return jnp.where(c < n_far, far_lim, CHUNK * (i + 1))

    def tile_row0(c):
        return pl.multiple_of(FRONT_PAD + tile_pos0(c), CHUNK)

    def half_pos(c, half):
        return tile_pos0(c) + half * LANES + lane1

    @pl.when(i == 0)
    def _():
        rows = 320 if sp % 320 == 0 else CHUNK

        def body(t, carry):
            bb = t // (sp // rows)
            r0 = pl.multiple_of((t % (sp // rows)) * rows, CHUNK)
            ki = kiw_ref[bb, pl.ds(r0, rows), :][:, :IDX_DIM]
            mean = jnp.mean(ki, axis=-1, keepdims=True)
            var = jnp.mean(jnp.square(ki - mean), axis=-1, keepdims=True)
            kn = (ki - mean) * lax.rsqrt(var + LN_EPS) * lng_ref[...] + lnb_ref[...]
            hi, lo = _split2(kn)
            ki3_s[bb, pl.ds(r0, rows), :] = jnp.concatenate([hi, lo, hi], axis=1)
            return carry

        lax.fori_loop(0, nb * (sp // rows), body, 0)

    def stage_queries(bb, carry):
        wq = kiw_ref[bb, pl.ds(pl.multiple_of(FRONT_PAD + CHUNK * i, CHUNK), CHUNK), :]
        wscale = IDX_HEADS ** -0.5 * IDX_DIM ** -0.5
        wcol = jnp.concatenate([wq[:, IDX_DIM + h:IDX_DIM + h + 1] for h in range(IDX_HEADS)], axis=0) * wscale
        wb_s[bb] = jnp.broadcast_to(wcol, (IDX_HEADS * CHUNK, LANES))
        return carry

    lax.fori_loop(0, nb, stage_queries, 0)

    def score_tile(c, carry):
        r0 = tile_row0(c)
        end = tile_end(c)
        xs = [_nt(qi_ref[:, bb].reshape(IDX_HEADS * CHUNK, 3 * IDX_DIM), ki3_s[bb, pl.ds(r0, KEY_TILE), :])
              for bb in range(nb)]
        keys = []
        for bb in range(nb):
            wb = wb_s[bb]
            parts = []
            for half, hs_ in enumerate(halves):
                act = jnp.maximum(xs[bb][:, hs_], 0.0) * wb
                sc = act[0:CHUNK]
                for h in range(1, IDX_HEADS):
                    sc = sc + act[h * CHUNK:(h + 1) * CHUNK]
                pos = half_pos(c, half)
                valid = (pos >= 0) & (pos < end)
                parts.append(jnp.where(valid, sc, -jnp.inf))
            keys.append(jnp.concatenate(parts, axis=1))
            key_s[bb, c] = keys[bb]
        for pr in range(n_pairs):
            both = jnp.concatenate([keys[2 * pr], keys[min(2 * pr + 1, nb - 1)]], axis=0)
            keyt_s[pr, c] = both.T
        return carry

    lax.fori_loop(0, n_tiles, score_tile, 0)

    thr_s[...] = jnp.full(thr_s.shape, -jnp.inf, F32)
    jlim_s[...] = jnp.full(jlim_s.shape, INT_MAX, I32)

    part = 32

    def count(cands, above):
        wide = [jnp.broadcast_to(cd, (part, LANES)) for cd in cands]

        def body(c, accs):
            out = []
            for pr in range(n_pairs):
                sc = keyt_s[pr, c].reshape(KEY_TILE // part, part, LANES)
                out.append(accs[pr] + jnp.sum(jnp.where(above(sc, wide[pr]), 1.0, 0.0), axis=0))
            return tuple(out)

        accs = lax.fori_loop(0, n_tiles, body, tuple(jnp.zeros((part, LANES), F32) for _ in range(n_pairs)))
        return [jnp.sum(a, axis=0, keepdims=True) for a in accs]

    def count_ge(keys):
        return count([_key_to_float(k) for k in keys], lambda s, c: s >= c)

    @pl.when(CHUNK * (i + 1) > topk)
    def _():
        n_all = float(KEY_TILE) * n_tiles.astype(F32)
        n_pos = count_ge([jnp.zeros((1, LANES), I32)] * n_pairs)
        ans0 = tuple(jnp.where(n >= kf, 0, INT_MIN).astype(I32) for n in n_pos)
        cnt0 = tuple(jnp.where(n >= kf, n, n_all) for n in n_pos)

        def one_bit(t, carry):
            ans, n_ans = carry
            bit = lax.shift_left(jnp.int32(1), 30 - t)
            cands = [a | bit for a in ans]
            cnts = count_ge(cands)
            take = [n >= kf for n in cnts]
            return (tuple(jnp.where(tk, cd, a) for tk, cd, a in zip(take, cands, ans)),
                    tuple(jnp.where(tk, n, na) for tk, n, na in zip(take, cnts, n_ans)))

        ans, n_ans = lax.fori_loop(0, 31, one_bit, (ans0, cnt0))
        def to_columns(rows_, dst):
            for pr in range(n_pairs):
                col = jnp.broadcast_to(rows_[pr], (LANES, LANES)).T
                dst[2 * pr] = col[:CHUNK]
                if 2 * pr + 1 < nb:
                    dst[2 * pr + 1] = col[CHUNK:]

        kth = [_key_to_float(a) for a in ans]
        to_columns(kth, thr_s)
        most = n_ans[0]
        for pr in range(1, n_pairs):
            most = jnp.maximum(most, n_ans[pr])

        @pl.when(jnp.max(most) > kf)
        def _():
            n_gt = count(kth, lambda s, c: s > c)
            need = [kf - n for n in n_gt]
            ans_w = [jnp.broadcast_to(a, (KEY_TILE, LANES)) for a in kth]
            krow = lax.broadcasted_iota(I32, (KEY_TILE, LANES), 0)
            tri = (lax.broadcasted_iota(I32, (KEY_TILE, KEY_TILE), 0)
                   >= lax.broadcasted_iota(I32, (KEY_TILE, KEY_TILE), 1)).astype(BF16)

            def body(c, carry):
                seen, last = carry
                pos = tile_pos0(c) + krow
                seen_out, last_out = [], []
                for pr in range(n_pairs):
                    tie = keyt_s[pr, c] == ans_w[pr]
                    rank = _mm(tri, jnp.where(tie, 1.0, 0.0).astype(BF16)) + seen[pr]
                    kept = jnp.where(tie & (rank <= need[pr]), pos, -1)
                    last_out.append(jnp.maximum(last[pr], jnp.max(kept, axis=0, keepdims=True)))
                    seen_out.append(rank[KEY_TILE - 1:KEY_TILE, :])
                return tuple(seen_out), tuple(last_out)

            init = (tuple(jnp.zeros((1, LANES), F32) for _ in range(n_pairs)),
                    tuple(jnp.full((1, LANES), -1, I32) for _ in range(n_pairs)))
            _, last = lax.fori_loop(0, n_tiles, body, init)
            to_columns(last, jlim_s)

    pair_lane_head = lane1 // HEAD_DIM

    cw = lg_s.shape[0]

    def phase_c(step, carry):
        bbs = [step * cw + j for j in range(cw)]
        zero = jnp.zeros((CHUNK, LANES), BF16)
        qm = []
        for bb in bbs:
            qs = q_ref[bb] * jnp.asarray(HEAD_DIM ** -0.5, BF16)
            qm.append([jnp.concatenate([jnp.where(pair_lane_head == 0, qs[:, g * LANES:(g + 1) * LANES], zero),
                                        jnp.where(pair_lane_head == 1, qs[:, g * LANES:(g + 1) * LANES], zero)],
                                       axis=0) for g in range(HEAD_PAIRS)])
        thr = [thr_s[bb] for bb in bbs]
        jlim = [jlim_s[bb] for bb in bbs]
        mx_s[...] = jnp.full(mx_s.shape, NEG_INF, F32)

        def logits(c, bias):
            r0 = tile_row0(c)
            end = tile_end(c)
            kt = [k_ref[bb, pl.ds(r0, KEY_TILE), :] for bb in bbs]
            s = [[_nt(qm[j][g], kt[j][:, g * LANES:(g + 1) * LANES]) for g in range(HEAD_PAIRS)]
                 for j in range(cw)]
            for j, bb in enumerate(bbs):
                sel = []
                for half, hs_ in enumerate(halves):
                    key = key_s[bb, c, :, hs_]
                    pos = half_pos(c, half)
                    sel.append((pos >= 0) & (pos < end) & ((key > thr[j]) | ((key == thr[j]) & (pos <= jlim[j]))))
                for g in range(HEAD_PAIRS):
                    sg = s[j][g]
                    if bias is not None:
                        sg = sg + bias[g * 2 * CHUNK:(g + 1) * 2 * CHUNK]
                    for hh in range(2):
                        rows = slice((2 * g + hh) * CHUNK, (2 * g + hh + 1) * CHUNK)
                        sub = sg[hh * CHUNK:(hh + 1) * CHUNK]
                        m0 = jnp.where(sel[0], sub[:, halves[0]], NEG_INF)
                        m1 = jnp.where(sel[1], sub[:, halves[1]], NEG_INF)
                        lg_s[j, c, rows, halves[0]] = m0
                        lg_s[j, c, rows, halves[1]] = m1
                        mx_s[j, rows] = jnp.maximum(mx_s[j, rows], jnp.maximum(m0, m1))

        def far_logits(c, carry):
            logits(c, None)
            return carry

        lax.fori_loop(0, n_far, far_logits, 0)
        logits(n_far, band_ref[...])

        for j in range(cw):
            mb_s[j] = jnp.broadcast_to(jnp.max(mx_s[j], axis=1, keepdims=True), mb_s.shape[1:])
        sm_s[...] = jnp.zeros_like(sm_s)
        pv_s[...] = jnp.zeros_like(pv_s)

        def weigh(c, carry):
            r0 = tile_row0(c)
            vt = [v_ref[bb, pl.ds(r0, KEY_TILE), :] for bb in bbs]
            for g in range(HEAD_PAIRS):
                rows = slice(g * 2 * CHUNK, (g + 1) * 2 * CHUNK)
                ps = []
                for j in range(cw):
                    m = mb_s[j, rows]
                    p0 = jnp.exp(lg_s[j, c, rows, halves[0]] - m)
                    p1 = jnp.exp(lg_s[j, c, rows, halves[1]] - m)
                    sm_s[j, rows] = sm_s[j, rows] + (p0 + p1)
                    ps.append(jnp.concatenate([p0, p1], axis=1).astype(BF16))
                for j in range(cw):
                    pv_s[j, rows] = pv_s[j, rows] + _mm(ps[j], vt[j][:, g * LANES:(g + 1) * LANES])
            return carry

        lax.fori_loop(0, n_tiles, weigh, 0)

        for j, bb in enumerate(bbs):
            outn = pv_s[j] / jnp.sum(sm_s[j], axis=1, keepdims=True)
            cols = []
            for g in range(HEAD_PAIRS):
                top = outn[(2 * g) * CHUNK:(2 * g + 1) * CHUNK]
                bot = outn[(2 * g + 1) * CHUNK:(2 * g + 2) * CHUNK]
                cols.append(jnp.where(pair_lane_head == 0, top, bot))
            o_ref[bb] = jnp.concatenate(cols, axis=1).astype(BF16)
        return carry

    lax.fori_loop(0, nb // cw, phase_c, 0)


def _dsa(qi_t, q, kiw_p, k_p, v_p, band, ln_g, ln_b):
    b, s, _ = q.shape
    sp = k_p.shape[1]
    topk = min(MAX_TOPK, s // 4)
    max_tiles = (s // CHUNK) // 4 + 1
    nb = 4 if b % 4 == 0 else (2 if b % 2 == 0 else 1)
    cw = nb
    rows = HEADS * CHUNK
    full = lambda a: pl.BlockSpec(a.shape, lambda i, j: (0,) * a.ndim)
    per_group = lambda n: pl.BlockSpec((nb, sp, n), lambda i, j: (i, 0, 0), pipeline_mode=pl.Buffered(1))
    return pl.pallas_call(
        functools.partial(_dsa_kernel, topk=topk, nb=nb),
        grid=(b // nb, s // CHUNK),
        in_specs=[
            pl.BlockSpec((IDX_HEADS, nb, CHUNK, 3 * IDX_DIM), lambda i, j: (0, i, j, 0)),
            pl.BlockSpec((nb, CHUNK, WIDTH), lambda i, j: (i, j, 0)),
            per_group(LANES), per_group(WIDTH), per_group(WIDTH),
            full(band), full(ln_g), full(ln_b),
        ],
        out_specs=pl.BlockSpec((nb, CHUNK, WIDTH), lambda i, j: (i, j, 0)),
        out_shape=jax.ShapeDtypeStruct((b, s, WIDTH), BF16),
        scratch_shapes=[
            pltpu.VMEM((nb, sp, 3 * IDX_DIM), BF16),
            pltpu.VMEM((nb, rows, LANES), F32),
            pltpu.VMEM((nb, max_tiles, CHUNK, KEY_TILE), F32),
            pltpu.VMEM(((nb + 1) // 2, max_tiles, KEY_TILE, 2 * CHUNK), F32),
            pltpu.VMEM((nb, CHUNK, LANES), F32),
            pltpu.VMEM((nb, CHUNK, LANES), I32),
            pltpu.VMEM((cw, max_tiles, rows, KEY_TILE), F32),
            pltpu.VMEM((cw, rows, LANES), F32),
            pltpu.VMEM((cw, rows, LANES), F32),
            pltpu.VMEM((cw, rows, LANES), F32),
            pltpu.VMEM((cw, rows, LANES), F32),
        ],
        compiler_params=pltpu.CompilerParams(dimension_semantics=("arbitrary", "arbitrary"),
                                             vmem_limit_bytes=VMEM_LIMIT),
        name="dsa",
    )(qi_t, q, kiw_p, k_p, v_p, band, ln_g, ln_b)


def _post_kernel(y1_ref, y2_ref, x_ref, p_ref, woa_ref, wob_ref, w1_ref, w2_ref, wp_ref, wg_ref,
                 g_mix_ref, g_f1_ref, g_f2_ref, g_p1_ref, g_p2_ref, o_ref):
    mix = _mm(y1_ref[...], woa_ref[...]) + _mm(y2_ref[...], wob_ref[...])
    x1 = x_ref[...] + _rms(mix, g_mix_ref[...])
    h = _rms(x1, g_f1_ref[...]).astype(BF16)
    hid = jnp.square(jnp.maximum(_mm(h, w1_ref[...]), 0.0)).astype(BF16)
    x2 = x1 + _rms(_mm(hid, w2_ref[...]), g_f2_ref[...])
    gate = _sigmoid(_mm(_rms(x2, g_p1_ref[...]).astype(BF16), wg_ref[...]))
    pe = _mm(p_ref[...].astype(BF16), wp_ref[...]) * gate
    o_ref[...] = x2 + _rms(pe, g_p2_ref[...])


def _post(y1, y2, x2d, p2d, woa, wob, w1, w2, wp, wg, gains, tm=512):
    m, d = x2d.shape
    row = lambda n: pl.BlockSpec((tm, n), lambda i: (i, 0))
    const = lambda a: pl.BlockSpec(a.shape, lambda i: (0, 0), pipeline_mode=pl.Buffered(1))
    weights = (woa, wob, w1, w2, wp, wg) + tuple(gains)
    return pl.pallas_call(
        _post_kernel,
        grid=(m // tm,),
        in_specs=[row(WIDTH), row(WIDTH), row(d), row(p2d.shape[1])] + [const(a) for a in weights],
        out_specs=row(d),
        out_shape=jax.ShapeDtypeStruct((m, d), F32),
        compiler_params=pltpu.CompilerParams(dimension_semantics=("arbitrary",), vmem_limit_bytes=VMEM_LIMIT),
        name="post",
    )(y1, y2, x2d, p2d, *weights)


def _layer(x, p, w_in, shift_mu, w0, w2r, a0, a2, g2, k_k, k_a, r_k, ln_g, ln_b, idx_g, idx_b, rel_bias,
           w_out, w_f1, w_f2, w_ple, w_gate, n_mix_pre, n_mix_post, n_ffn_pre, n_ffn_post, n_ple_pre, n_ple_post):
    b, s, d = x.shape
    m = b * s
    row = lambda a: a.reshape(1, -1).astype(F32)

    c = [0, WIDTH, WIDTH + DECAY_RANK, 2 * WIDTH + DECAY_RANK, 3 * WIDTH + DECAY_RANK,
         3 * WIDTH + DECAY_RANK + A_RANK, RWKV_COLS]
    order = [(c[0], c[1]), (c[2], c[3]), (c[3], c[4]), (c[1], c[2]), (c[4], c[5]), (c[5], c[6])]
    reorder = lambda a: jnp.concatenate([a[..., lo:hi] for lo, hi in order], axis=-1)
    w_main = jnp.concatenate([reorder(w_in[:, :RWKV_COLS]), w_in[:, RWKV_COLS:RWKV_COLS + 3 * WIDTH]],
                             axis=1).astype(BF16)
    w_idx = w_in[:, RWKV_COLS + 3 * WIDTH:]
    w_idx = jnp.pad(w_idx, ((0, 0), (0, N_IDX - w_idx.shape[1])))
    w_idx_hi = w_idx.astype(BF16)
    w_idx_lo = (w_idx - w_idx_hi.astype(F32)).astype(BF16)

    zr, q, k, v, qi, kiw = _in_proj(x.reshape(m, d), row(n_mix_pre), w_main, w_idx_hi, w_idx_lo)

    zeros = jnp.zeros((DECAY_RANK, WIDTH), F32)
    w_lora = jnp.concatenate([jnp.concatenate([w2r, zeros], axis=1),
                              jnp.concatenate([zeros, a2], axis=1)], axis=0).astype(BF16)
    y_rwkv = _rwkv(zr.reshape(b, s, RWKV_COLS), row(reorder(shift_mu)), row(w0), w_lora, row(a0), g2.astype(BF16),
                   row(k_k), row(k_a), row(r_k), row(ln_g), row(ln_b))

    band = _bias_tiles(rel_bias.astype(F32))
    pad = lambda a: jnp.pad(a.reshape(b, s, -1), ((0, 0), (FRONT_PAD, BACK_PAD), (0, 0)))
    y_dsa = _dsa(qi.reshape(IDX_HEADS, b, s, 3 * IDX_DIM), q.reshape(b, s, WIDTH), pad(kiw), pad(k), pad(v), band,
                 row(idx_g), row(idx_b))

    gains = tuple(row(a) for a in (n_mix_post, n_ffn_pre, n_ffn_post, n_ple_pre, n_ple_post))
    out = _post(y_rwkv.reshape(m, WIDTH), y_dsa.reshape(m, WIDTH), x.reshape(m, d), p.reshape(m, -1),
                w_out[:WIDTH].astype(BF16), w_out[WIDTH:].astype(BF16), w_f1.astype(BF16), w_f2.astype(BF16),
                w_ple.astype(BF16), w_gate.astype(BF16), gains)
    return out.reshape(b, s, d)


def kernel(x, p, w_in, shift_mu, rwkv_w0, rwkv_w2, rwkv_a0, rwkv_a2, rwkv_g2, rwkv_k_k, rwkv_k_a, rwkv_r_k, rwkv_ln_g, rwkv_ln_b, idx_k_ln_g, idx_k_ln_b, rel_bias, w_out, w_ffn1, w_ffn2, w_ple, w_ple_gate, norm_mix_pre, norm_mix_post, norm_ffn_pre, norm_ffn_post, norm_ple_pre, norm_ple_post):
    for i in range(p.shape[0]):
        x = _layer(x, p[i], w_in[i], shift_mu[i], rwkv_w0[i], rwkv_w2[i], rwkv_a0[i], rwkv_a2[i], rwkv_g2[i],
                   rwkv_k_k[i], rwkv_k_a[i], rwkv_r_k[i], rwkv_ln_g[i], rwkv_ln_b[i], idx_k_ln_g[i],
                   idx_k_ln_b[i], rel_bias, w_out[i], w_ffn1[i], w_ffn2[i], w_ple[i], w_ple_gate[i],
                   norm_mix_pre[i], norm_mix_post[i], norm_ffn_pre[i], norm_ffn_post[i], norm_ple_pre[i],
                   norm_ple_post[i])
    return x
```

```python
import functools
import math

import jax
import jax.numpy as jnp
from jax import lax
from jax.experimental import pallas as pl
from jax.experimental.pallas import tpu as pltpu

F32 = jnp.float32
BF16 = jnp.bfloat16
I32 = jnp.int32

CHUNK = 64
HEADS = 8
HEAD_DIM = 64
WIDTH = HEADS * HEAD_DIM
DECAY_RANK = 64
A_RANK = 64
GATE_RANK = 128
GN_EPS = 64e-5
IDX_HEADS = 8
IDX_DIM = 64
MAX_TOPK = 256
REL_BUCKETS = 32
REL_MAX_DISTANCE = 128
NORM_EPS = 1e-6
LN_EPS = 1e-6
NEG_INF = -1e30
RWKV_COLS = 1792
N_MAIN = RWKV_COLS + 3 * WIDTH
N_IDX = 640

LANES = 128
KEY_TILE = 256
HEAD_PAIRS = WIDTH // LANES
GROUP = 4
ROW_GROUP = 4
FRONT_PAD = 2 * CHUNK
BACK_PAD = CHUNK
VMEM_LIMIT = 62 * 1024 * 1024

INT_MIN = -(2**31)
INT_MAX = 2**31 - 1


def _nt(a, b):
    return lax.dot_general(a, b, (((1,), (1,)), ((), ())), preferred_element_type=F32)


def _tn(a, b):
    return lax.dot_general(a, b, (((0,), (0,)), ((), ())), preferred_element_type=F32)


def _mm(a, b):
    return jnp.dot(a, b, preferred_element_type=F32)


def _split2(x):
    hi = x.astype(BF16)
    lo = (x - hi.astype(F32)).astype(BF16)
    return hi, lo


def _alternate(*gens):
    out = [None] * len(gens)
    live = list(range(len(gens)))
    while live:
        for n in list(live):
            try:
                next(gens[n])
            except StopIteration as stop:
                out[n] = stop.value
                live.remove(n)
    return out


def _rms(x, g):
    return x * lax.rsqrt(jnp.mean(x * x, axis=-1, keepdims=True) + NORM_EPS) * g


def _sigmoid(x):
    return 1.0 / (1.0 + jnp.exp(-x))


def _in_proj_kernel(x_ref, g_ref, wm_ref, wih_ref, wil_ref, zr_ref, q_ref, k_ref, v_ref, qi_ref, kiw_ref):
    h = _rms(x_ref[...], g_ref[...])
    hh, hl = _split2(h)
    zm = _mm(hh, wm_ref[...])
    zr_ref[...] = zm[:, :RWKV_COLS]
    q_ref[...] = zm[:, RWKV_COLS:RWKV_COLS + WIDTH].astype(BF16)
    k_ref[...] = zm[:, RWKV_COLS + WIDTH:RWKV_COLS + 2 * WIDTH].astype(BF16)
    v_ref[...] = zm[:, RWKV_COLS + 2 * WIDTH:].astype(BF16)
    zi = _mm(hh, wih_ref[...]) + _mm(hh, wil_ref[...]) + _mm(hl, wih_ref[...])
    qh, ql = _split2(zi[:, :WIDTH])
    for h in range(IDX_HEADS):
        sl = slice(h * IDX_DIM, (h + 1) * IDX_DIM)
        qi_ref[h] = jnp.concatenate([qh[:, sl], qh[:, sl], ql[:, sl]], axis=1)
    kiw_ref[...] = zi[:, WIDTH:]


def _in_proj(x2, g, wm, wih, wil, tm=512):
    m, d = x2.shape
    row = lambda n: pl.BlockSpec((tm, n), lambda i: (i, 0))
    full = lambda a: pl.BlockSpec(a.shape, lambda i: (0, 0))
    return pl.pallas_call(
        _in_proj_kernel,
        grid=(m // tm,),
        in_specs=[row(d), full(g), full(wm), full(wih), full(wil)],
        out_specs=[row(RWKV_COLS), row(WIDTH), row(WIDTH), row(WIDTH),
                   pl.BlockSpec((IDX_HEADS, tm, 3 * IDX_DIM), lambda i: (0, i, 0)), row(LANES)],
        out_shape=[
            jax.ShapeDtypeStruct((m, RWKV_COLS), F32),
            jax.ShapeDtypeStruct((m, WIDTH), BF16),
            jax.ShapeDtypeStruct((m, WIDTH), BF16),
            jax.ShapeDtypeStruct((m, WIDTH), BF16),
            jax.ShapeDtypeStruct((IDX_HEADS, m, 3 * IDX_DIM), BF16),
            jax.ShapeDtypeStruct((m, LANES), F32),
        ],
        compiler_params=pltpu.CompilerParams(dimension_semantics=("arbitrary",), vmem_limit_bytes=VMEM_LIMIT),
        name="in_proj",
    )(x2, g, wm, wih, wil)


def _t5_bucket(rel):
    half = REL_BUCKETS // 2
    max_exact = half // 2
    ret = jnp.where(rel > 0, half, 0)
    n = jnp.abs(rel)
    nf = jnp.maximum(n, 1).astype(F32)
    steps = jnp.log(nf / max_exact) / math.log(REL_MAX_DISTANCE / max_exact) * (half - max_exact)
    large = max_exact
    for level in range(1, half - max_exact):
        large = large + jnp.where(steps >= level - 1e-4, 1, 0)
    return ret + jnp.where(n < max_exact, n, large)


def _bias_kernel(tab_ref, band_ref):
    r = lax.broadcasted_iota(I32, (CHUNK, KEY_TILE), 0)
    c = lax.broadcasted_iota(I32, (CHUNK, KEY_TILE), 1)
    near_bucket = _t5_bucket(c - FRONT_PAD - r)
    far_bucket = _t5_bucket(jnp.full((CHUNK, KEY_TILE), -(FRONT_PAD + 1), I32))
    for h in range(HEADS):
        near = jnp.zeros((CHUNK, KEY_TILE), F32)
        far = jnp.zeros((CHUNK, KEY_TILE), F32)
        for j in range(REL_BUCKETS):
            t = tab_ref[j, h]
            near = jnp.where(near_bucket == j, t, near)
            far = jnp.where(far_bucket == j, t, far)
        band_ref[h * CHUNK:(h + 1) * CHUNK, :] = near - far


def _bias_tiles(rel_bias):
    return pl.pallas_call(
        _bias_kernel,
        in_specs=[pl.BlockSpec(memory_space=pltpu.SMEM)],
        out_shape=jax.ShapeDtypeStruct((HEADS * CHUNK, KEY_TILE), F32),
        name="bias_tiles",
    )(rel_bias)


def _seg_sum(x):
    rows = x.shape[0]
    li = lax.broadcasted_iota(I32, (LANES, LANES), 0) // HEAD_DIM
    lj = lax.broadcasted_iota(I32, (LANES, LANES), 1) // HEAD_DIM
    ones = (li == lj).astype(BF16)
    xs = jnp.concatenate([x[:, g * LANES:(g + 1) * LANES] for g in range(HEAD_PAIRS)], axis=0)
    y = _mm(xs.astype(BF16), ones)
    return jnp.concatenate([y[g * rows:(g + 1) * rows] for g in range(HEAD_PAIRS)], axis=1)


def _rwkv_kernel(z_ref, mu_ref, w0_ref, wl_ref, a0_ref, g2_ref, kk_ref, ka_ref, rk_ref, lng_ref, lnb_ref,
                 y_ref, prev_s, st_s, *, nb):
    @pl.when(pl.program_id(1) == 0)
    def _():
        prev_s[...] = jnp.zeros_like(prev_s)
        st_s[...] = jnp.zeros_like(st_s)

    gw = GROUP * HEAD_DIM
    lane_h = lax.broadcasted_iota(I32, (CHUNK, gw), 1) // HEAD_DIM
    src = lax.broadcasted_iota(I32, (CHUNK, gw), 1) & (HEAD_DIM - 1)
    tok = lax.broadcasted_iota(I32, (CHUNK, gw), 0)
    strict = tok > src
    incl = tok >= src
    eye = (tok == src).astype(F32)
    diag_blk = (lax.broadcasted_iota(I32, (gw, gw), 0) // HEAD_DIM) == (lax.broadcasted_iota(I32, (gw, gw), 1) // HEAD_DIM)

    def bdiag(x):
        xb = x.astype(BF16)
        zero = jnp.zeros_like(xb)
        return jnp.concatenate([jnp.where(lane_h == h, xb, zero) for h in range(GROUP)], axis=0)

    def prepare(rb0, nr):
        rows = nr * CHUNK
        z = z_ref[rb0:rb0 + nr].reshape(rows, RWKV_COLS)
        rolled = pltpu.roll(z, 1, axis=0)
        first = lax.broadcasted_iota(I32, (8, RWKV_COLS), 0) == 0
        pieces = []
        for j in range(nr):
            base = j * CHUNK
            pieces += [jnp.where(first, prev_s[rb0 + j], rolled[base:base + 8]), rolled[base + 8:base + CHUNK]]
            prev_s[rb0 + j] = z[base + CHUNK - 1:base + CHUNK, :]
        zp = jnp.concatenate(pieces, axis=0)
        zs = z + (zp - z) * mu_ref[...]
        yield
        r = zs[:, 0:WIDTH]
        k = zs[:, WIDTH:2 * WIDTH]
        v = zs[:, 2 * WIDTH:3 * WIDTH]
        zl = zs[:, 3 * WIDTH:3 * WIDTH + LANES]
        zg = zs[:, 3 * WIDTH + LANES:]
        lane = lax.broadcasted_iota(I32, zl.shape, 1)
        zl = jnp.where(lane < DECAY_RANK, jnp.tanh(zl), zl)
        la = _mm(zl.astype(BF16), wl_ref[...])
        ld = -math.exp(-0.5) * _sigmoid(w0_ref[...] + la[:, :WIDTH])
        yield
        a = _sigmoid(a0_ref[...] + la[:, WIDTH:])
        gate = _mm(_sigmoid(zg).astype(BF16), g2_ref[...])
        yield
        kk = k * kk_ref[...]
        kk = kk * jnp.minimum(lax.rsqrt(_seg_sum(kk * kk)), 1e12)
        yield
        kmod = k * (1.0 + (a - 1.0) * ka_ref[...])
        bvec = kk * a
        yield
        ti = lax.broadcasted_iota(I32, (rows, rows), 0)
        tj = lax.broadcasted_iota(I32, (rows, rows), 1)
        tri = ((ti >= tj) & (ti // CHUNK == tj // CHUNK)).astype(BF16)
        l1, l2 = _split2(ld)
        cum = _mm(tri, l1) + _mm(tri, l2)
        cum_x = cum - ld
        yield
        row_w = lax.broadcasted_iota(I32, (rows, WIDTH), 0)
        cref = cum[CHUNK // 2 - 1:CHUNK // 2, :]
        clast = cum[CHUNK - 1:CHUNK, :]
        for j in range(1, nr):
            mid = j * CHUNK + CHUNK // 2 - 1
            cref = jnp.where(row_w >= j * CHUNK, cum[mid:mid + 1, :], cref)
            clast = jnp.where(row_w >= j * CHUNK, cum[(j + 1) * CHUNK - 1:(j + 1) * CHUNK, :], clast)
        yield
        d = dict(r=r, v=v, kmod=kmod, gate=gate, cum=cum)
        e_neg = jnp.exp(cref - cum)
        d["rt"] = r * jnp.exp(cum - cref)
        yield
        d["at"] = -kk * jnp.exp(cum_x - cref)
        d["kt"] = kmod * e_neg
        yield
        d["bt"] = bvec * e_neg
        d["r0"] = r * jnp.exp(cum)
        yield
        d["a0v"] = -kk * jnp.exp(cum_x)
        e_end = jnp.exp(clast - cum)
        yield
        d["kc"] = kmod * e_end
        d["bc"] = bvec * e_end
        return d

    def chains(rb0, nr, d):
        units = [(j, g, slice(j * CHUNK, (j + 1) * CHUNK), slice(g * gw, (g + 1) * gw))
                 for j in range(nr) for g in range(HEADS // GROUP)]
        cut = lambda x: [x[rs, gs] for _, _, rs, gs in units]
        at_w, rt_w, bt_w, kt_w, v_w = cut(d["at"]), cut(d["rt"]), cut(d["bt"]), cut(d["kt"]), cut(d["v"])
        lhs = [jnp.concatenate([a_, r_], axis=0).astype(BF16) for a_, r_ in zip(at_w, rt_w)]
        ab_rb = [_nt(l_, bdiag(b_)) for l_, b_ in zip(lhs, bt_w)]
        yield
        ak_rk = [_nt(l_, bdiag(k_)) for l_, k_ in zip(lhs, kt_w)]
        low = [jnp.where(strict, x[:CHUNK], 0.0) for x in ab_rb]
        a_ak = [jnp.where(strict, x[:CHUNK], 0.0).astype(BF16) for x in ak_rk]
        a_rb = [jnp.where(incl, x[CHUNK:], 0.0).astype(BF16) for x in ab_rb]
        a_rk = [jnp.where(incl, x[CHUNK:], 0.0).astype(BF16) for x in ak_rk]
        yield
        v_bd = [bdiag(x) for x in v_w]
        s_f = [st_s[rb0 + j, g] for j, g, _, _ in units]
        s_b = [s.astype(BF16) for s in s_f]
        x0 = [_nt(x.astype(BF16), s) for x, s in zip(cut(d["a0v"]), s_b)]
        o_st = [_nt(x.astype(BF16), s) for x, s in zip(cut(d["r0"]), s_b)]
        yk = [_mm(a_, vb) for a_, vb in zip(a_ak, v_bd)]
        yield
        t = [eye + x for x in low]
        p = low
        p_bd = [bdiag(x) for x in p]
        for _ in range(int(math.log2(CHUNK)) - 1):
            p = [_mm(x.astype(BF16), xb) for x, xb in zip(p, p_bd)]
            p_bd = [bdiag(x) for x in p]
            yield
            t = [t_ + _mm(t_.astype(BF16), xb) for t_, xb in zip(t, p_bd)]
            yield
        u = [_mm(t_.astype(BF16), bdiag(a_ + b_)) for t_, a_, b_ in zip(t, x0, yk)]
        yield
        o_w = [a_ + _mm(rb, bdiag(u_)) + _mm(rk, vb) for a_, rb, u_, rk, vb in zip(o_st, a_rb, u, a_rk, v_bd)]
        uv = [jnp.concatenate([u_, v_], axis=0).astype(BF16) for u_, v_ in zip(u, v_w)]
        bk = [jnp.concatenate([b_, k_], axis=0).astype(BF16) for b_, k_ in zip(cut(d["bc"]), cut(d["kc"]))]
        yield
        for (j, gi, _, gs), s, w, x in zip(units, s_f, uv, bk):
            pc = jnp.exp(d["cum"][(j + 1) * CHUNK - 1:(j + 1) * CHUNK, gs])
            st_s[rb0 + j, gi] = s * pc + jnp.where(diag_blk, _tn(w, x), 0.0)
        groups = HEADS // GROUP
        return jnp.concatenate([jnp.concatenate(o_w[j * groups:(j + 1) * groups], axis=1) for j in range(nr)], axis=0)

    def finish(rb0, nr, d, o):
        inv_n = 1.0 / HEAD_DIM
        mean = _seg_sum(o) * inv_n
        dev = o - mean
        yield
        var = _seg_sum(dev * dev) * inv_n
        on = dev * lax.rsqrt(var + GN_EPS) * lng_ref[...] + lnb_ref[...]
        yield
        bonus = _seg_sum(d["r"] * d["kmod"] * rk_ref[...]) * d["v"]
        y_ref[rb0:rb0 + nr] = ((on + bonus) * d["gate"]).astype(BF16).reshape(nr, CHUNK, WIDTH)

    alternate = _alternate
    spans = [(rb0, min(ROW_GROUP, nb - rb0)) for rb0 in range(0, nb, ROW_GROUP)]
    data, = alternate(prepare(*spans[0]))
    done = None
    for n, span in enumerate(spans):
        stages = [chains(*span, data)]
        if n + 1 < len(spans):
            stages.append(prepare(*spans[n + 1]))
        if done is not None:
            stages.append(finish(*done))
        res = alternate(*stages)
        done = (*span, data, res[0])
        if n + 1 < len(spans):
            data = res[1]
    alternate(finish(*done))


def _rwkv(zr, mu, w0, wl, a0, g2, k_k, k_a, r_k, ln_g, ln_b, nb_max=8):
    b, s, _ = zr.shape
    nb = max(n for n in range(1, nb_max + 1) if b % n == 0)
    full = lambda a: pl.BlockSpec(a.shape, lambda i, j: (0,) * a.ndim)
    params = (mu, w0, wl, a0, g2, k_k, k_a, r_k, ln_g, ln_b)
    return pl.pallas_call(
        functools.partial(_rwkv_kernel, nb=nb),
        grid=(b // nb, s // CHUNK),
        in_specs=[pl.BlockSpec((nb, CHUNK, RWKV_COLS), lambda i, j: (i, j, 0))] + [full(a) for a in params],
        out_specs=pl.BlockSpec((nb, CHUNK, WIDTH), lambda i, j: (i, j, 0)),
        out_shape=jax.ShapeDtypeStruct((b, s, WIDTH), BF16),
        scratch_shapes=[
            pltpu.VMEM((nb, 1, RWKV_COLS), F32),
            pltpu.VMEM((nb, HEADS // GROUP, GROUP * HEAD_DIM, GROUP * HEAD_DIM), F32),
        ],
        compiler_params=pltpu.CompilerParams(dimension_semantics=("arbitrary", "arbitrary"),
                                             vmem_limit_bytes=VMEM_LIMIT),
        name="rwkv",
    )(zr, *params)


def _key_to_float(key):
    return lax.bitcast_convert_type(key ^ ((key >> 31) & INT_MAX), F32)


def _dsa_kernel(qi_ref, q_ref, kiw_ref, k_ref, v_ref, band_ref, lng_ref, lnb_ref, o_ref,
                ki3_s, wb_s, key_s, keyt_s, thr_s, jlim_s, lg_s, mx_s, mb_s, sm_s, pv_s, *, topk, nb):
    i = pl.program_id(1)
    sp = kiw_ref.shape[1]
    n_far = (i + 1) // 4
    n_tiles = n_far + 1
    far_lim = CHUNK * i - FRONT_PAD
    lane1 = lax.broadcasted_iota(I32, (CHUNK, LANES), 1)
    halves = [slice(0, LANES), slice(LANES, KEY_TILE)]
    n_pairs = (nb + 1) // 2
    kf = float(topk)

    def tile_pos0(c):
        return jnp.where(c < n_far, c * KEY_TILE, far_lim)

    def tile_end(c):
        return jnp.where(c < n_far, far_lim, CHUNK * (i + 1))

    def tile_row0(c):
        return pl.multiple_of(FRONT_PAD + tile_pos0(c), CHUNK)

    def half_pos(c, half):
        return tile_pos0(c) + half * LANES + lane1

    @pl.when(i == 0)
    def _():
        rows = 320 if sp % 320 == 0 else CHUNK

        def body(t, carry):
            bb = t // (sp // rows)
            r0 = pl.multiple_of((t % (sp // rows)) * rows, CHUNK)
            ki = kiw_ref[bb, pl.ds(r0, rows), :][:, :IDX_DIM]
            mean = jnp.mean(ki, axis=-1, keepdims=True)
            var = jnp.mean(jnp.square(ki - mean), axis=-1, keepdims=True)
            kn = (ki - mean) * lax.rsqrt(var + LN_EPS) * lng_ref[...] + lnb_ref[...]
            hi, lo = _split2(kn)
            ki3_s[bb, pl.ds(r0, rows), :] = jnp.concatenate([hi, lo, hi], axis=1)
            return carry

        lax.fori_loop(0, nb * (sp // rows), body, 0)

    def stage_queries(bb, carry):
        wq = kiw_ref[bb, pl.ds(pl.multiple_of(FRONT_PAD + CHUNK * i, CHUNK), CHUNK), :]
        wscale = IDX_HEADS ** -0.5 * IDX_DIM ** -0.5
        wcol = jnp.concatenate([wq[:, IDX_DIM + h:IDX_DIM + h + 1] for h in range(IDX_HEADS)], axis=0) * wscale
        wb_s[bb] = jnp.broadcast_to(wcol, (IDX_HEADS * CHUNK, LANES))
        return carry

    lax.fori_loop(0, nb, stage_queries, 0)

    def score_tile(c, carry):
        r0 = tile_row0(c)
        end = tile_end(c)
        xs = [_nt(qi_ref[:, bb].reshape(IDX_HEADS * CHUNK, 3 * IDX_DIM), ki3_s[bb, pl.ds(r0, KEY_TILE), :])
              for bb in range(nb)]
        keys = []
        for bb in range(nb):
            wb = wb_s[bb]
            parts = []
            for half, hs_ in enumerate(halves):
                act = jnp.maximum(xs[bb][:, hs_], 0.0) * wb
                sc = act[0:CHUNK]
                for h in range(1, IDX_HEADS):
                    sc = sc + act[h * CHUNK:(h + 1) * CHUNK]
                pos = half_pos(c, half)
                valid = (pos >= 0) & (pos < end)
                parts.append(jnp.where(valid, sc, -jnp.inf))
            keys.append(jnp.concatenate(parts, axis=1))
            key_s[bb, c] = keys[bb]
        for pr in range(n_pairs):
            both = jnp.concatenate([keys[2 * pr], keys[min(2 * pr + 1, nb - 1)]], axis=0)
            keyt_s[pr, c] = both.T
        return carry

    lax.fori_loop(0, n_tiles, score_tile, 0)

    thr_s[...] = jnp.full(thr_s.shape, -jnp.inf, F32)
    jlim_s[...] = jnp.full(jlim_s.shape, INT_MAX, I32)

    part = 32

    def count(cands, above):
        wide = [jnp.broadcast_to(cd, (part, LANES)) for cd in cands]

        def body(c, accs):
            out = []
            for pr in range(n_pairs):
                sc = keyt_s[pr, c].reshape(KEY_TILE // part, part, LANES)
                out.append(accs[pr] + jnp.sum(jnp.where(above(sc, wide[pr]), 1.0, 0.0), axis=0))
            return tuple(out)

        accs = lax.fori_loop(0, n_tiles, body, tuple(jnp.zeros((part, LANES), F32) for _ in range(n_pairs)))
        return [jnp.sum(a, axis=0, keepdims=True) for a in accs]

    def count_ge(keys):
        return count([_key_to_float(k) for k in keys], lambda s, c: s >= c)

    @pl.when(CHUNK * (i + 1) > topk)
    def _():
        n_all = float(KEY_TILE) * n_tiles.astype(F32)
        n_pos = count_ge([jnp.zeros((1, LANES), I32)] * n_pairs)
        ans0 = tuple(jnp.where(n >= kf, 0, INT_MIN).astype(I32) for n in n_pos)
        cnt0 = tuple(jnp.where(n >= kf, n, n_all) for n in n_pos)

        def one_bit(t, carry):
            ans, n_ans = carry
            bit = lax.shift_left(jnp.int32(1), 30 - t)
            cands = [a | bit for a in ans]
            cnts = count_ge(cands)
            take = [n >= kf for n in cnts]
            return (tuple(jnp.where(tk, cd, a) for tk, cd, a in zip(take, cands, ans)),
                    tuple(jnp.where(tk, n, na) for tk, n, na in zip(take, cnts, n_ans)))

        ans, n_ans = lax.fori_loop(0, 31, one_bit, (ans0, cnt0))
        def to_columns(rows_, dst):
            for pr in range(n_pairs):
                col = jnp.broadcast_to(rows_[pr], (LANES, LANES)).T
                dst[2 * pr] = col[:CHUNK]
                if 2 * pr + 1 < nb:
                    dst[2 * pr + 1] = col[CHUNK:]

        kth = [_key_to_float(a) for a in ans]
        to_columns(kth, thr_s)
        most = n_ans[0]
        for pr in range(1, n_pairs):
            most = jnp.maximum(most, n_ans[pr])

        @pl.when(jnp.max(most) > kf)
        def _():
            n_gt = count(kth, lambda s, c: s > c)
            need = [kf - n for n in n_gt]
            ans_w = [jnp.broadcast_to(a, (KEY_TILE, LANES)) for a in kth]
            krow = lax.broadcasted_iota(I32, (KEY_TILE, LANES), 0)
            tri = (lax.broadcasted_iota(I32, (KEY_TILE, KEY_TILE), 0)
                   >= lax.broadcasted_iota(I32, (KEY_TILE, KEY_TILE), 1)).astype(BF16)

            def body(c, carry):
                seen, last = carry
                pos = tile_pos0(c) + krow
                seen_out, last_out = [], []
                for pr in range(n_pairs):
                    tie = keyt_s[pr, c] == ans_w[pr]
                    rank = _mm(tri, jnp.where(tie, 1.0, 0.0).astype(BF16)) + seen[pr]
                    kept = jnp.where(tie & (rank <= need[pr]), pos, -1)
                    last_out.append(jnp.maximum(last[pr], jnp.max(kept, axis=0, keepdims=True)))
                    seen_out.append(rank[KEY_TILE - 1:KEY_TILE, :])
                return tuple(seen_out), tuple(last_out)

            init = (tuple(jnp.zeros((1, LANES), F32) for _ in range(n_pairs)),
                    tuple(jnp.full((1, LANES), -1, I32) for _ in range(n_pairs)))
            _, last = lax.fori_loop(0, n_tiles, body, init)
            to_columns(last, jlim_s)

    pair_lane_head = lane1 // HEAD_DIM

    cw = lg_s.shape[0]

    def phase_c(step, carry):
        bbs = [step * cw + j for j in range(cw)]
        zero = jnp.zeros((CHUNK, LANES), BF16)
        qm = []
        for bb in bbs:
            qs = q_ref[bb] * jnp.asarray(HEAD_DIM ** -0.5, BF16)
            qm.append([jnp.concatenate([jnp.where(pair_lane_head == 0, qs[:, g * LANES:(g + 1) * LANES], zero),
                                        jnp.where(pair_lane_head == 1, qs[:, g * LANES:(g + 1) * LANES], zero)],
                                       axis=0) for g in range(HEAD_PAIRS)])
        thr = [thr_s[bb] for bb in bbs]
        jlim = [jlim_s[bb] for bb in bbs]
        mx_s[...] = jnp.full(mx_s.shape, NEG_INF, F32)

        def logits(c, bias):
            r0 = tile_row0(c)
            end = tile_end(c)
            kt = [k_ref[bb, pl.ds(r0, KEY_TILE), :] for bb in bbs]
            s = [[_nt(qm[j][g], kt[j][:, g * LANES:(g + 1) * LANES]) for g in range(HEAD_PAIRS)]
                 for j in range(cw)]
            for j, bb in enumerate(bbs):
                sel = []
                for half, hs_ in enumerate(halves):
                    key = key_s[bb, c, :, hs_]
                    pos = half_pos(c, half)
                    sel.append((pos >= 0) & (pos < end) & ((key > thr[j]) | ((key == thr[j]) & (pos <= jlim[j]))))
                for g in range(HEAD_PAIRS):
                    sg = s[j][g]
                    if bias is not None:
                        sg = sg + bias[g * 2 * CHUNK:(g + 1) * 2 * CHUNK]
                    for hh in range(2):
                        rows = slice((2 * g + hh) * CHUNK, (2 * g + hh + 1) * CHUNK)
                        sub = sg[hh * CHUNK:(hh + 1) * CHUNK]
                        m0 = jnp.where(sel[0], sub[:, halves[0]], NEG_INF)
                        m1 = jnp.where(sel[1], sub[:, halves[1]], NEG_INF)
                        lg_s[j, c, rows, halves[0]] = m0
                        lg_s[j, c, rows, halves[1]] = m1
                        mx_s[j, rows] = jnp.maximum(mx_s[j, rows], jnp.maximum(m0, m1))

        def far_logits(c, carry):
            logits(c, None)
            return carry

        lax.fori_loop(0, n_far, far_logits, 0)
        logits(n_far, band_ref[...])

        for j in range(cw):
            mb_s[j] = jnp.broadcast_to(jnp.max(mx_s[j], axis=1, keepdims=True), mb_s.shape[1:])
        sm_s[...] = jnp.zeros_like(sm_s)
        pv_s[...] = jnp.zeros_like(pv_s)

        def weigh(c, carry):
            r0 = tile_row0(c)
            vt = [v_ref[bb, pl.ds(r0, KEY_TILE), :] for bb in bbs]
            for g in range(HEAD_PAIRS):
                rows = slice(g * 2 * CHUNK, (g + 1) * 2 * CHUNK)
                ps = []
                for j in range(cw):
                    m = mb_s[j, rows]
                    p0 = jnp.exp(lg_s[j, c, rows, halves[0]] - m)
                    p1 = jnp.exp(lg_s[j, c, rows, halves[1]] - m)
                    sm_s[j, rows] = sm_s[j, rows] + (p0 + p1)
                    ps.append(jnp.concatenate([p0, p1], axis=1).astype(BF16))
                for j in range(cw):
                    pv_s[j, rows] = pv_s[j, rows] + _mm(ps[j], vt[j][:, g * LANES:(g + 1) * LANES])
            return carry

        lax.fori_loop(0, n_tiles, weigh, 0)

        for j, bb in enumerate(bbs):
            outn = pv_s[j] / jnp.sum(sm_s[j], axis=1, keepdims=True)
            cols = []
            for g in range(HEAD_PAIRS):
                top = outn[(2 * g) * CHUNK:(2 * g + 1) * CHUNK]
                bot = outn[(2 * g + 1) * CHUNK:(2 * g + 2) * CHUNK]
                cols.append(jnp.where(pair_lane_head == 0, top, bot))
            o_ref[bb] = jnp.concatenate(cols, axis=1).astype(BF16)
        return carry

    lax.fori_loop(0, nb // cw, phase_c, 0)


def _dsa(qi_t, q, kiw_p, k_p, v_p, band, ln_g, ln_b):
    b, s, _ = q.shape
    sp = k_p.shape[1]
    topk = min(MAX_TOPK, s // 4)
    max_tiles = (s // CHUNK) // 4 + 1
    nb = 4 if b % 4 == 0 else (2 if b % 2 == 0 else 1)
    cw = nb
    rows = HEADS * CHUNK
    full = lambda a: pl.BlockSpec(a.shape, lambda i, j: (0,) * a.ndim)
    per_group = lambda n: pl.BlockSpec((nb, sp, n), lambda i, j: (i, 0, 0), pipeline_mode=pl.Buffered(1))
    return pl.pallas_call(
        functools.partial(_dsa_kernel, topk=topk, nb=nb),
        grid=(b // nb, s // CHUNK),
        in_specs=[
            pl.BlockSpec((IDX_HEADS, nb, CHUNK, 3 * IDX_DIM), lambda i, j: (0, i, j, 0)),
            pl.BlockSpec((nb, CHUNK, WIDTH), lambda i, j: (i, j, 0)),
            per_group(LANES), per_group(WIDTH), per_group(WIDTH),
            full(band), full(ln_g), full(ln_b),
        ],
        out_specs=pl.BlockSpec((nb, CHUNK, WIDTH), lambda i, j: (i, j, 0)),
        out_shape=jax.ShapeDtypeStruct((b, s, WIDTH), BF16),
        scratch_shapes=[
            pltpu.VMEM((nb, sp, 3 * IDX_DIM), BF16),
            pltpu.VMEM((nb, rows, LANES), F32),
            pltpu.VMEM((nb, max_tiles, CHUNK, KEY_TILE), F32),
            pltpu.VMEM(((nb + 1) // 2, max_tiles, KEY_TILE, 2 * CHUNK), F32),
            pltpu.VMEM((nb, CHUNK, LANES), F32),
            pltpu.VMEM((nb, CHUNK, LANES), I32),
            pltpu.VMEM((cw, max_tiles, rows, KEY_TILE), F32),
            pltpu.VMEM((cw, rows, LANES), F32),
            pltpu.VMEM((cw, rows, LANES), F32),
            pltpu.VMEM((cw, rows, LANES), F32),
            pltpu.VMEM((cw, rows, LANES), F32),
        ],
        compiler_params=pltpu.CompilerParams(dimension_semantics=("arbitrary", "arbitrary"),
                                             vmem_limit_bytes=VMEM_LIMIT),
        name="dsa",
    )(qi_t, q, kiw_p, k_p, v_p, band, ln_g, ln_b)


def _post_kernel(y1_ref, y2_ref, x_ref, p_ref, woa_ref, wob_ref, w1_ref, w2_ref, wp_ref, wg_ref,
                 g_mix_ref, g_f1_ref, g_f2_ref, g_p1_ref, g_p2_ref, o_ref):
    mix = _mm(y1_ref[...], woa_ref[...]) + _mm(y2_ref[...], wob_ref[...])
    x1 = x_ref[...] + _rms(mix, g_mix_ref[...])
    h = _rms(x1, g_f1_ref[...]).astype(BF16)
    hid = jnp.square(jnp.maximum(_mm(h, w1_ref[...]), 0.0)).astype(BF16)
    x2 = x1 + _rms(_mm(hid, w2_ref[...]), g_f2_ref[...])
    gate = _sigmoid(_mm(_rms(x2, g_p1_ref[...]).astype(BF16), wg_ref[...]))
    pe = _mm(p_ref[...].astype(BF16), wp_ref[...]) * gate
    o_ref[...] = x2 + _rms(pe, g_p2_ref[...])


def _post(y1, y2, x2d, p2d, woa, wob, w1, w2, wp, wg, gains, tm=512):
    m, d = x2d.shape
    row = lambda n: pl.BlockSpec((tm, n), lambda i: (i, 0))
    const = lambda a: pl.BlockSpec(a.shape, lambda i: (0, 0), pipeline_mode=pl.Buffered(1))
    weights = (woa, wob, w1, w2, wp, wg) + tuple(gains)
    return pl.pallas_call(
        _post_kernel,
        grid=(m // tm,),
        in_specs=[row(WIDTH), row(WIDTH), row(d), row(p2d.shape[1])] + [const(a) for a in weights],
        out_specs=row(d),
        out_shape=jax.ShapeDtypeStruct((m, d), F32),
        compiler_params=pltpu.CompilerParams(dimension_semantics=("arbitrary",), vmem_limit_bytes=VMEM_LIMIT),
        name="post",
    )(y1, y2, x2d, p2d, *weights)


def _layer(x, p, w_in, shift_mu, w0, w2r, a0, a2, g2, k_k, k_a, r_k, ln_g, ln_b, idx_g, idx_b, rel_bias,
           w_out, w_f1, w_f2, w_ple, w_gate, n_mix_pre, n_mix_post, n_ffn_pre, n_ffn_post, n_ple_pre, n_ple_post):
    b, s, d = x.shape
    m = b * s
    row = lambda a: a.reshape(1, -1).astype(F32)

    c = [0, WIDTH, WIDTH + DECAY_RANK, 2 * WIDTH + DECAY_RANK, 3 * WIDTH + DECAY_RANK,
         3 * WIDTH + DECAY_RANK + A_RANK, RWKV_COLS]
    order = [(c[0], c[1]), (c[2], c[3]), (c[3], c[4]), (c[1], c[2]), (c[4], c[5]), (c[5], c[6])]
    reorder = lambda a: jnp.concatenate([a[..., lo:hi] for lo, hi in order], axis=-1)
    w_main = jnp.concatenate([reorder(w_in[:, :RWKV_COLS]), w_in[:, RWKV_COLS:RWKV_COLS + 3 * WIDTH]],
                             axis=1).astype(BF16)
    w_idx = w_in[:, RWKV_COLS + 3 * WIDTH:]
    w_idx = jnp.pad(w_idx, ((0, 0), (0, N_IDX - w_idx.shape[1])))
    w_idx_hi = w_idx.astype(BF16)
    w_idx_lo = (w_idx - w_idx_hi.astype(F32)).astype(BF16)

    zr, q, k, v, qi, kiw = _in_proj(x.reshape(m, d), row(n_mix_pre), w_main, w_idx_hi, w_idx_lo)

    zeros = jnp.zeros((DECAY_RANK, WIDTH), F32)
    w_lora = jnp.concatenate([jnp.concatenate([w2r, zeros], axis=1),
                              jnp.concatenate([zeros, a2], axis=1)], axis=0).astype(BF16)
    y_rwkv = _rwkv(zr.reshape(b, s, RWKV_COLS), row(reorder(shift_mu)), row(w0), w_lora, row(a0), g2.astype(BF16),
                   row(k_k), row(k_a), row(r_k), row(ln_g), row(ln_b))

    band = _bias_tiles(rel_bias.astype(F32))
    pad = lambda a: jnp.pad(a.reshape(b, s, -1), ((0, 0), (FRONT_PAD, BACK_PAD), (0, 0)))
    y_dsa = _dsa(qi.reshape(IDX_HEADS, b, s, 3 * IDX_DIM), q.reshape(b, s, WIDTH), pad(kiw), pad(k), pad(v), band,
                 row(idx_g), row(idx_b))

    gains = tuple(row(a) for a in (n_mix_post, n_ffn_pre, n_ffn_post, n_ple_pre, n_ple_post))
    out = _post(y_rwkv.reshape(m, WIDTH), y_dsa.reshape(m, WIDTH), x.reshape(m, d), p.reshape(m, -1),
                w_out[:WIDTH].astype(BF16), w_out[WIDTH:].astype(BF16), w_f1.astype(BF16), w_f2.astype(BF16),
                w_ple.astype(BF16), w_gate.astype(BF16), gains)
    return out.reshape(b, s, d)


def kernel(x, p, w_in, shift_mu, rwkv_w0, rwkv_w2, rwkv_a0, rwkv_a2, rwkv_g2, rwkv_k_k, rwkv_k_a, rwkv_r_k, rwkv_ln_g, rwkv_ln_b, idx_k_ln_g, idx_k_ln_b, rel_bias, w_out, w_ffn1, w_ffn2, w_ple, w_ple_gate, norm_mix_pre, norm_mix_post, norm_ffn_pre, norm_ffn_post, norm_ple_pre, norm_ple_post):
    for i in range(p.shape[0]):
        x = _layer(x, p[i], w_in[i], shift_mu[i], rwkv_w0[i], rwkv_w2[i], rwkv_a0[i], rwkv_a2[i], rwkv_g2[i],
                   rwkv_k_k[i], rwkv_k_a[i], rwkv_r_k[i], rwkv_ln_g[i], rwkv_ln_b[i], idx_k_ln_g[i],
                   idx_k_ln_b[i], rel_bias, w_out[i], w_ffn1[i], w_ffn2[i], w_ple[i], w_ple_gate[i],
                   norm_mix_pre[i], norm_mix_post[i], norm_ffn_pre[i], norm_ffn_post[i], norm_ple_pre[i],
                   norm_ple_post[i])
    return x
```

```python
import functools
import math

import jax
import jax.numpy as jnp
from jax import lax
from jax.experimental import pallas as pl
from jax.experimental.pallas import tpu as pltpu

F32 = jnp.float32
BF16 = jnp.bfloat16
I32 = jnp.int32

CHUNK = 64
HEADS = 8
HEAD_DIM = 64
WIDTH = HEADS * HEAD_DIM
DECAY_RANK = 64
A_RANK = 64
GATE_RANK = 128
GN_EPS = 64e-5
IDX_HEADS = 8
IDX_DIM = 64
MAX_TOPK = 256
REL_BUCKETS = 32
REL_MAX_DISTANCE = 128
NORM_EPS = 1e-6
LN_EPS = 1e-6
NEG_INF = -1e30
RWKV_COLS = 1792
N_MAIN = RWKV_COLS + 3 * WIDTH
N_IDX = 640

LANES = 128
KEY_TILE = 256
HEAD_PAIRS = WIDTH // LANES
GROUP = 4
ROW_GROUP = 4
FAR_GAP = 2
NEAR_VARIANTS = KEY_TILE // CHUNK
VMEM_LIMIT = 62 * 1024 * 1024

INT_MIN = -(2**31)
INT_MAX = 2**31 - 1


def _nt(a, b):
    return lax.dot_general(a, b, (((1,), (1,)), ((), ())), preferred_element_type=F32)


def _tn(a, b):
    return lax.dot_general(a, b, (((0,), (0,)), ((), ())), preferred_element_type=F32)


def _mm(a, b):
    return jnp.dot(a, b, preferred_element_type=F32)


def _split2(x):
    hi = x.astype(BF16)
    lo = (x - hi.astype(F32)).astype(BF16)
    return hi, lo


def _alternate(*gens):
    out = [None] * len(gens)
    live = list(range(len(gens)))
    while live:
        for n in list(live):
            try:
                next(gens[n])
            except StopIteration as stop:
                out[n] = stop.value
                live.remove(n)
    return out


def _rms(x, g):
    return x * lax.rsqrt(jnp.mean(x * x, axis=-1, keepdims=True) + NORM_EPS) * g


def _sigmoid(x):
    return 1.0 / (1.0 + jnp.exp(-x))


def _in_proj_kernel(x_ref, g_ref, wm_ref, wih_ref, wil_ref, zr_ref, q_ref, k_ref, v_ref, qi_ref, kiw_ref):
    h = _rms(x_ref[...], g_ref[...])
    hh, hl = _split2(h)
    zm = _mm(hh, wm_ref[...])
    zr_ref[...] = zm[:, :RWKV_COLS]
    q_ref[...] = zm[:, RWKV_COLS:RWKV_COLS + WIDTH].astype(BF16)
    k_ref[...] = zm[:, RWKV_COLS + WIDTH:RWKV_COLS + 2 * WIDTH].astype(BF16)
    v_ref[...] = zm[:, RWKV_COLS + 2 * WIDTH:].astype(BF16)
    zi = _mm(hh, wih_ref[...]) + _mm(hh, wil_ref[...]) + _mm(hl, wih_ref[...])
    qh, ql = _split2(zi[:, :WIDTH])
    for h in range(IDX_HEADS):
        sl = slice(h * IDX_DIM, (h + 1) * IDX_DIM)
        qi_ref[h] = jnp.concatenate([qh[:, sl], qh[:, sl], ql[:, sl]], axis=1)
    kiw_ref[...] = zi[:, WIDTH:]


def _in_proj(x2, g, wm, wih, wil, tm=512):
    m, d = x2.shape
    row = lambda n: pl.BlockSpec((tm, n), lambda i: (i, 0))
    full = lambda a: pl.BlockSpec(a.shape, lambda i: (0, 0))
    return pl.pallas_call(
        _in_proj_kernel,
        grid=(m // tm,),
        in_specs=[row(d), full(g), full(wm), full(wih), full(wil)],
        out_specs=[row(RWKV_COLS), row(WIDTH), row(WIDTH), row(WIDTH),
                   pl.BlockSpec((IDX_HEADS, tm, 3 * IDX_DIM), lambda i: (0, i, 0)), row(LANES)],
        out_shape=[
            jax.ShapeDtypeStruct((m, RWKV_COLS), F32),
            jax.ShapeDtypeStruct((m, WIDTH), BF16),
            jax.ShapeDtypeStruct((m, WIDTH), BF16),
            jax.ShapeDtypeStruct((m, WIDTH), BF16),
            jax.ShapeDtypeStruct((IDX_HEADS, m, 3 * IDX_DIM), BF16),
            jax.ShapeDtypeStruct((m, LANES), F32),
        ],
        compiler_params=pltpu.CompilerParams(dimension_semantics=("arbitrary",), vmem_limit_bytes=VMEM_LIMIT),
        name="in_proj",
    )(x2, g, wm, wih, wil)


def _t5_bucket(rel):
    half = REL_BUCKETS // 2
    max_exact = half // 2
    ret = jnp.where(rel > 0, half, 0)
    n = jnp.abs(rel)
    nf = jnp.maximum(n, 1).astype(F32)
    steps = jnp.log(nf / max_exact) / math.log(REL_MAX_DISTANCE / max_exact) * (half - max_exact)
    large = max_exact
    for level in range(1, half - max_exact):
        large = large + jnp.where(steps >= level - 1e-4, 1, 0)
    return ret + jnp.where(n < max_exact, n, large)


def _bias_kernel(tab_ref, band_ref):
    r = lax.broadcasted_iota(I32, (CHUNK, KEY_TILE), 0)
    c = lax.broadcasted_iota(I32, (CHUNK, KEY_TILE), 1)
    far_bucket = _t5_bucket(jnp.full((CHUNK, KEY_TILE), -(FAR_GAP * CHUNK + 1), I32))
    for m in range(NEAR_VARIANTS):
        near_bucket = _t5_bucket(c - m * CHUNK - r)
        for h in range(HEADS):
            near = jnp.zeros((CHUNK, KEY_TILE), F32)
            far = jnp.zeros((CHUNK, KEY_TILE), F32)
            for j in range(REL_BUCKETS):
                t = tab_ref[j, h]
                near = jnp.where(near_bucket == j, t, near)
                far = jnp.where(far_bucket == j, t, far)
            band_ref[m, h * CHUNK:(h + 1) * CHUNK, :] = near - far


def _bias_tiles(rel_bias):
    return pl.pallas_call(
        _bias_kernel,
        in_specs=[pl.BlockSpec(memory_space=pltpu.SMEM)],
        out_shape=jax.ShapeDtypeStruct((NEAR_VARIANTS, HEADS * CHUNK, KEY_TILE), F32),
        name="bias_tiles",
    )(rel_bias)


def _seg_sum(x):
    rows = x.shape[0]
    li = lax.broadcasted_iota(I32, (LANES, LANES), 0) // HEAD_DIM
    lj = lax.broadcasted_iota(I32, (LANES, LANES), 1) // HEAD_DIM
    ones = (li == lj).astype(BF16)
    xs = jnp.concatenate([x[:, g * LANES:(g + 1) * LANES] for g in range(HEAD_PAIRS)], axis=0)
    y = _mm(xs.astype(BF16), ones)
    return jnp.concatenate([y[g * rows:(g + 1) * rows] for g in range(HEAD_PAIRS)], axis=1)


def _rwkv_kernel(z_ref, mu_ref, w0_ref, wl_ref, a0_ref, g2_ref, kk_ref, ka_ref, rk_ref, lng_ref, lnb_ref,
                 y_ref, prev_s, st_s, *, nb):
    @pl.when(pl.program_id(1) == 0)
    def _():
        prev_s[...] = jnp.zeros_like(prev_s)
        st_s[...] = jnp.zeros_like(st_s)

    gw = GROUP * HEAD_DIM
    lane_h = lax.broadcasted_iota(I32, (CHUNK, gw), 1) // HEAD_DIM
    src = lax.broadcasted_iota(I32, (CHUNK, gw), 1) & (HEAD_DIM - 1)
    tok = lax.broadcasted_iota(I32, (CHUNK, gw), 0)
    strict = tok > src
    incl = tok >= src
    eye = (tok == src).astype(F32)
    diag_blk = (lax.broadcasted_iota(I32, (gw, gw), 0) // HEAD_DIM) == (lax.broadcasted_iota(I32, (gw, gw), 1) // HEAD_DIM)

    def bdiag(x):
        xb = x.astype(BF16)
        zero = jnp.zeros_like(xb)
        return jnp.concatenate([jnp.where(lane_h == h, xb, zero) for h in range(GROUP)], axis=0)

    def prepare(rb0, nr):
        rows = nr * CHUNK
        z = z_ref[rb0:rb0 + nr].reshape(rows, RWKV_COLS)
        rolled = pltpu.roll(z, 1, axis=0)
        first = lax.broadcasted_iota(I32, (8, RWKV_COLS), 0) == 0
        pieces = []
        for j in range(nr):
            base = j * CHUNK
            pieces += [jnp.where(first, prev_s[rb0 + j], rolled[base:base + 8]), rolled[base + 8:base + CHUNK]]
            prev_s[rb0 + j] = z[base + CHUNK - 1:base + CHUNK, :]
        zp = jnp.concatenate(pieces, axis=0)
        zs = z + (zp - z) * mu_ref[...]
        yield
        r = zs[:, 0:WIDTH]
        k = zs[:, WIDTH:2 * WIDTH]
        v = zs[:, 2 * WIDTH:3 * WIDTH]
        zl = zs[:, 3 * WIDTH:3 * WIDTH + LANES]
        zg = zs[:, 3 * WIDTH + LANES:]
        lane = lax.broadcasted_iota(I32, zl.shape, 1)
        zl = jnp.where(lane < DECAY_RANK, jnp.tanh(zl), zl)
        la = _mm(zl.astype(BF16), wl_ref[...])
        ld = -math.exp(-0.5) * _sigmoid(w0_ref[...] + la[:, :WIDTH])
        yield
        a = _sigmoid(a0_ref[...] + la[:, WIDTH:])
        gate = _mm(_sigmoid(zg).astype(BF16), g2_ref[...])
        yield
        kk = k * kk_ref[...]
        kk = kk * jnp.minimum(lax.rsqrt(_seg_sum(kk * kk)), 1e12)
        yield
        kmod = k * (1.0 + (a - 1.0) * ka_ref[...])
        bvec = kk * a
        yield
        ti = lax.broadcasted_iota(I32, (rows, rows), 0)
        tj = lax.broadcasted_iota(I32, (rows, rows), 1)
        tri = ((ti >= tj) & (ti // CHUNK == tj // CHUNK)).astype(BF16)
        l1, l2 = _split2(ld)
        cum = _mm(tri, l1) + _mm(tri, l2)
        cum_x = cum - ld
        yield
        row_w = lax.broadcasted_iota(I32, (rows, WIDTH), 0)
        cref = cum[CHUNK // 2 - 1:CHUNK // 2, :]
        clast = cum[CHUNK - 1:CHUNK, :]
        for j in range(1, nr):
            mid = j * CHUNK + CHUNK // 2 - 1
            cref = jnp.where(row_w >= j * CHUNK, cum[mid:mid + 1, :], cref)
            clast = jnp.where(row_w >= j * CHUNK, cum[(j + 1) * CHUNK - 1:(j + 1) * CHUNK, :], clast)
        yield
        d = dict(r=r, v=v, kmod=kmod, gate=gate, cum=cum)
        e_neg = jnp.exp(cref - cum)
        d["rt"] = r * jnp.exp(cum - cref)
        yield
        d["at"] = -kk * jnp.exp(cum_x - cref)
        d["kt"] = kmod * e_neg
        yield
        d["bt"] = bvec * e_neg
        d["r0"] = r * jnp.exp(cum)
        yield
        d["a0v"] = -kk * jnp.exp(cum_x)
        e_end = jnp.exp(clast - cum)
        yield
        d["kc"] = kmod * e_end
        d["bc"] = bvec * e_end
        return d

    def chains(rb0, nr, d):
        units = [(j, g, slice(j * CHUNK, (j + 1) * CHUNK), slice(g * gw, (g + 1) * gw))
                 for j in range(nr) for g in range(HEADS // GROUP)]
        cut = lambda x: [x[rs, gs] for _, _, rs, gs in units]
        at_w, rt_w, bt_w, kt_w, v_w = cut(d["at"]), cut(d["rt"]), cut(d["bt"]), cut(d["kt"]), cut(d["v"])
        lhs = [jnp.concatenate([a_, r_], axis=0).astype(BF16) for a_, r_ in zip(at_w, rt_w)]
        ab_rb = [_nt(l_, bdiag(b_)) for l_, b_ in zip(lhs, bt_w)]
        yield
        ak_rk = [_nt(l_, bdiag(k_)) for l_, k_ in zip(lhs, kt_w)]
        low = [jnp.where(strict, x[:CHUNK], 0.0) for x in ab_rb]
        a_ak = [jnp.where(strict, x[:CHUNK], 0.0).astype(BF16) for x in ak_rk]
        a_rb = [jnp.where(incl, x[CHUNK:], 0.0).astype(BF16) for x in ab_rb]
        a_rk = [jnp.where(incl, x[CHUNK:], 0.0).astype(BF16) for x in ak_rk]
        yield
        v_bd = [bdiag(x) for x in v_w]
        s_f = [st_s[rb0 + j, g] for j, g, _, _ in units]
        s_b = [s.astype(BF16) for s in s_f]
        x0 = [_nt(x.astype(BF16), s) for x, s in zip(cut(d["a0v"]), s_b)]
        o_st = [_nt(x.astype(BF16), s) for x, s in zip(cut(d["r0"]), s_b)]
        yk = [_mm(a_, vb) for a_, vb in zip(a_ak, v_bd)]
        yield
        t = [eye + x for x in low]
        p = low
        p_bd = [bdiag(x) for x in p]
        for _ in range(int(math.log2(CHUNK)) - 1):
            p = [_mm(x.astype(BF16), xb) for x, xb in zip(p, p_bd)]
            p_bd = [bdiag(x) for x in p]
            yield
            t = [t_ + _mm(t_.astype(BF16), xb) for t_, xb in zip(t, p_bd)]
            yield
        u = [_mm(t_.astype(BF16), bdiag(a_ + b_)) for t_, a_, b_ in zip(t, x0, yk)]
        yield
        o_w = [a_ + _mm(rb, bdiag(u_)) + _mm(rk, vb) for a_, rb, u_, rk, vb in zip(o_st, a_rb, u, a_rk, v_bd)]
        uv = [jnp.concatenate([u_, v_], axis=0).astype(BF16) for u_, v_ in zip(u, v_w)]
        bk = [jnp.concatenate([b_, k_], axis=0).astype(BF16) for b_, k_ in zip(cut(d["bc"]), cut(d["kc"]))]
        yield
        for (j, gi, _, gs), s, w, x in zip(units, s_f, uv, bk):
            pc = jnp.exp(d["cum"][(j + 1) * CHUNK - 1:(j + 1) * CHUNK, gs])
            st_s[rb0 + j, gi] = s * pc + jnp.where(diag_blk, _tn(w, x), 0.0)
        groups = HEADS // GROUP
        return jnp.concatenate([jnp.concatenate(o_w[j * groups:(j + 1) * groups], axis=1) for j in range(nr)], axis=0)

    def finish(rb0, nr, d, o):
        inv_n = 1.0 / HEAD_DIM
        mean = _seg_sum(o) * inv_n
        dev = o - mean
        yield
        var = _seg_sum(dev * dev) * inv_n
        on = dev * lax.rsqrt(var + GN_EPS) * lng_ref[...] + lnb_ref[...]
        yield
        bonus = _seg_sum(d["r"] * d["kmod"] * rk_ref[...]) * d["v"]
        y_ref[rb0:rb0 + nr] = ((on + bonus) * d["gate"]).astype(BF16).reshape(nr, CHUNK, WIDTH)

    alternate = _alternate
    spans = [(rb0, min(ROW_GROUP, nb - rb0)) for rb0 in range(0, nb, ROW_GROUP)]
    data, = alternate(prepare(*spans[0]))
    done = None
    for n, span in enumerate(spans):
        stages = [chains(*span, data)]
        if n + 1 < len(spans):
            stages.append(prepare(*spans[n + 1]))
        if done is not None:
            stages.append(finish(*done))
        res = alternate(*stages)
        done = (*span, data, res[0])
        if n + 1 < len(spans):
            data = res[1]
    alternate(finish(*done))


def _rwkv(zr, mu, w0, wl, a0, g2, k_k, k_a, r_k, ln_g, ln_b, nb_max=8):
    b, s, _ = zr.shape
    nb = max(n for n in range(1, nb_max + 1) if b % n == 0)
    full = lambda a: pl.BlockSpec(a.shape, lambda i, j: (0,) * a.ndim)
    params = (mu, w0, wl, a0, g2, k_k, k_a, r_k, ln_g, ln_b)
    return pl.pallas_call(
        functools.partial(_rwkv_kernel, nb=nb),
        grid=(b // nb, s // CHUNK),
        in_specs=[pl.BlockSpec((nb, CHUNK, RWKV_COLS), lambda i, j: (i, j, 0))] + [full(a) for a in params],
        out_specs=pl.BlockSpec((nb, CHUNK, WIDTH), lambda i, j: (i, j, 0)),
        out_shape=jax.ShapeDtypeStruct((b, s, WIDTH), BF16),
        scratch_shapes=[
            pltpu.VMEM((nb, 1, RWKV_COLS), F32),
            pltpu.VMEM((nb, HEADS // GROUP, GROUP * HEAD_DIM, GROUP * HEAD_DIM), F32),
        ],
        compiler_params=pltpu.CompilerParams(dimension_semantics=("arbitrary", "arbitrary"),
                                             vmem_limit_bytes=VMEM_LIMIT),
        name="rwkv",
    )(zr, *params)


def _key_to_float(key):
    return lax.bitcast_convert_type(key ^ ((key >> 31) & INT_MAX), F32)


def _dsa_kernel(qi_ref, q_ref, kiw_ref, k_ref, v_ref, band_ref, lng_ref, lnb_ref, o_ref,
                ki3_s, wb_s, key_s, keyt_s, thr_s, jlim_s, lg_s, mx_s, mb_s, sm_s, pv_s, *, topk, nb):
    i = pl.program_id(1)
    sp = kiw_ref.shape[1]
    far_lim = jnp.clip(CHUNK * (i - FAR_GAP), 0, sp - KEY_TILE)
    n_far = (far_lim + KEY_TILE - 1) // KEY_TILE
    n_tiles = n_far + 1
    near_variant = i - far_lim // CHUNK
    lane1 = lax.broadcasted_iota(I32, (CHUNK, LANES), 1)
    halves = [slice(0, LANES), slice(LANES, KEY_TILE)]
    n_pairs = (nb + 1) // 2
    kf = float(topk)

    def tile_pos0(c):
        return jnp.where(c < n_far, c * KEY_TILE, far_lim)

    def tile_end(c):
        return jnp.where(c < n_far, far_lim, CHUNK * (i + 1))

    def tile_row0(c):
        return pl.multiple_of(tile_pos0(c), CHUNK)

    def half_pos(c, half):
        return tile_pos0(c) + half * LANES + lane1

    @pl.when(i == 0)
    def _():
        rows = KEY_TILE if sp % KEY_TILE == 0 else CHUNK

        def body(t, carry):
            bb = t // (sp // rows)
            r0 = pl.multiple_of((t % (sp // rows)) * rows, CHUNK)
            ki = kiw_ref[bb, pl.ds(r0, rows), :][:, :IDX_DIM]
            mean = jnp.mean(ki, axis=-1, keepdims=True)
            var = jnp.mean(jnp.square(ki - mean), axis=-1, keepdims=True)
            kn = (ki - mean) * lax.rsqrt(var + LN_EPS) * lng_ref[...] + lnb_ref[...]
            hi, lo = _split2(kn)
            ki3_s[bb, pl.ds(r0, rows), :] = jnp.concatenate([hi, lo, hi], axis=1)
            return carry

        lax.fori_loop(0, nb * (sp // rows), body, 0)

    def stage_queries(bb, carry):
        wq = kiw_ref[bb, pl.ds(pl.multiple_of(CHUNK * i, CHUNK), CHUNK), :]
        wscale = IDX_HEADS ** -0.5 * IDX_DIM ** -0.5
        wcol = jnp.concatenate([wq[:, IDX_DIM + h:IDX_DIM + h + 1] for h in range(IDX_HEADS)], axis=0) * wscale
        wb_s[bb] = jnp.broadcast_to(wcol, (IDX_HEADS * CHUNK, LANES))
        return carry

    lax.fori_loop(0, nb, stage_queries, 0)

    def score_tile(c, carry):
        r0 = tile_row0(c)
        end = tile_end(c)
        xs = [_nt(qi_ref[:, bb].reshape(IDX_HEADS * CHUNK, 3 * IDX_DIM), ki3_s[bb, pl.ds(r0, KEY_TILE), :])
              for bb in range(nb)]
        keys = []
        for bb in range(nb):
            wb = wb_s[bb]
            parts = []
            for half, hs_ in enumerate(halves):
                act = jnp.maximum(xs[bb][:, hs_], 0.0) * wb
                sc = act[0:CHUNK]
                for h in range(1, IDX_HEADS):
                    sc = sc + act[h * CHUNK:(h + 1) * CHUNK]
                pos = half_pos(c, half)
                valid = (pos >= 0) & (pos < end)
                parts.append(jnp.where(valid, sc, -jnp.inf))
            keys.append(jnp.concatenate(parts, axis=1))
            key_s[bb, c] = keys[bb]
        for pr in range(n_pairs):
            both = jnp.concatenate([keys[2 * pr], keys[min(2 * pr + 1, nb - 1)]], axis=0)
            keyt_s[pr, c] = both.T
        return carry

    lax.fori_loop(0, n_tiles, score_tile, 0)

    thr_s[...] = jnp.full(thr_s.shape, -jnp.inf, F32)
    jlim_s[...] = jnp.full(jlim_s.shape, INT_MAX, I32)

    part = 32

    def count(cands, above):
        wide = [jnp.broadcast_to(cd, (part, LANES)) for cd in cands]

        def body(c, accs):
            out = []
            for pr in range(n_pairs):
                sc = keyt_s[pr, c].reshape(KEY_TILE // part, part, LANES)
                out.append(accs[pr] + jnp.sum(jnp.where(above(sc, wide[pr]), 1.0, 0.0), axis=0))
            return tuple(out)

        accs = lax.fori_loop(0, n_tiles, body, tuple(jnp.zeros((part, LANES), F32) for _ in range(n_pairs)))
        return [jnp.sum(a, axis=0, keepdims=True) for a in accs]

    def count_ge(keys):
        return count([_key_to_float(k) for k in keys], lambda s, c: s >= c)

    @pl.when(CHUNK * (i + 1) > topk)
    def _():
        n_all = float(KEY_TILE) * n_tiles.astype(F32)
        n_pos = count_ge([jnp.zeros((1, LANES), I32)] * n_pairs)
        ans0 = tuple(jnp.where(n >= kf, 0, INT_MIN).astype(I32) for n in n_pos)
        cnt0 = tuple(jnp.where(n >= kf, n, n_all) for n in n_pos)

        def one_bit(t, carry):
            ans, n_ans = carry
            bit = lax.shift_left(jnp.int32(1), 30 - t)
            cands = [a | bit for a in ans]
            cnts = count_ge(cands)
            take = [n >= kf for n in cnts]
            return (tuple(jnp.where(tk, cd, a) for tk, cd, a in zip(take, cands, ans)),
                    tuple(jnp.where(tk, n, na) for tk, n, na in zip(take, cnts, n_ans)))

        ans, n_ans = lax.fori_loop(0, 31, one_bit, (ans0, cnt0))
        def to_columns(rows_, dst):
            for pr in range(n_pairs):
                col = jnp.broadcast_to(rows_[pr], (LANES, LANES)).T
                dst[2 * pr] = col[:CHUNK]
                if 2 * pr + 1 < nb:
                    dst[2 * pr + 1] = col[CHUNK:]

        kth = [_key_to_float(a) for a in ans]
        to_columns(kth, thr_s)
        most = n_ans[0]
        for pr in range(1, n_pairs):
            most = jnp.maximum(most, n_ans[pr])

        @pl.when(jnp.max(most) > kf)
        def _():
            n_gt = count(kth, lambda s, c: s > c)
            need = [kf - n for n in n_gt]
            ans_w = [jnp.broadcast_to(a, (KEY_TILE, LANES)) for a in kth]
            krow = lax.broadcasted_iota(I32, (KEY_TILE, LANES), 0)
            tri = (lax.broadcasted_iota(I32, (KEY_TILE, KEY_TILE), 0)
                   >= lax.broadcasted_iota(I32, (KEY_TILE, KEY_TILE), 1)).astype(BF16)

            def body(c, carry):
                seen, last = carry
                pos = tile_pos0(c) + krow
                seen_out, last_out = [], []
                for pr in range(n_pairs):
                    tie = keyt_s[pr, c] == ans_w[pr]
                    rank = _mm(tri, jnp.where(tie, 1.0, 0.0).astype(BF16)) + seen[pr]
                    kept = jnp.where(tie & (rank <= need[pr]), pos, -1)
                    last_out.append(jnp.maximum(last[pr], jnp.max(kept, axis=0, keepdims=True)))
                    seen_out.append(rank[KEY_TILE - 1:KEY_TILE, :])
                return tuple(seen_out), tuple(last_out)

            init = (tuple(jnp.zeros((1, LANES), F32) for _ in range(n_pairs)),
                    tuple(jnp.full((1, LANES), -1, I32) for _ in range(n_pairs)))
            _, last = lax.fori_loop(0, n_tiles, body, init)
            to_columns(last, jlim_s)

    pair_lane_head = lane1 // HEAD_DIM

    cw = lg_s.shape[0]

    def phase_c(step, carry):
        bbs = [step * cw + j for j in range(cw)]
        zero = jnp.zeros((CHUNK, LANES), BF16)
        qm = []
        for bb in bbs:
            qs = q_ref[bb] * jnp.asarray(HEAD_DIM ** -0.5, BF16)
            qm.append([jnp.concatenate([jnp.where(pair_lane_head == 0, qs[:, g * LANES:(g + 1) * LANES], zero),
                                        jnp.where(pair_lane_head == 1, qs[:, g * LANES:(g + 1) * LANES], zero)],
                                       axis=0) for g in range(HEAD_PAIRS)])
        thr = [thr_s[bb] for bb in bbs]
        jlim = [jlim_s[bb] for bb in bbs]
        mx_s[...] = jnp.full(mx_s.shape, NEG_INF, F32)

        def logits(c, bias):
            r0 = tile_row0(c)
            end = tile_end(c)
            kt = [k_ref[bb, pl.ds(r0, KEY_TILE), :] for bb in bbs]
            s = [[_nt(qm[j][g], kt[j][:, g * LANES:(g + 1) * LANES]) for g in range(HEAD_PAIRS)]
                 for j in range(cw)]
            for j, bb in enumerate(bbs):
                sel = []
                for half, hs_ in enumerate(halves):
                    key = key_s[bb, c, :, hs_]
                    pos = half_pos(c, half)
                    sel.append((pos >= 0) & (pos < end) & ((key > thr[j]) | ((key == thr[j]) & (pos <= jlim[j]))))
                for g in range(HEAD_PAIRS):
                    sg = s[j][g]
                    if bias is not None:
                        sg = sg + bias[g * 2 * CHUNK:(g + 1) * 2 * CHUNK]
                    for hh in range(2):
                        rows = slice((2 * g + hh) * CHUNK, (2 * g + hh + 1) * CHUNK)
                        sub = sg[hh * CHUNK:(hh + 1) * CHUNK]
                        m0 = jnp.where(sel[0], sub[:, halves[0]], NEG_INF)
                        m1 = jnp.where(sel[1], sub[:, halves[1]], NEG_INF)
                        lg_s[j, c, rows, halves[0]] = m0
                        lg_s[j, c, rows, halves[1]] = m1
                        mx_s[j, rows] = jnp.maximum(mx_s[j, rows], jnp.maximum(m0, m1))

        def far_logits(c, carry):
            logits(c, None)
            return carry

        lax.fori_loop(0, n_far, far_logits, 0)
        logits(n_far, band_ref[near_variant])

        for j in range(cw):
            mb_s[j] = jnp.broadcast_to(jnp.max(mx_s[j], axis=1, keepdims=True), mb_s.shape[1:])
        sm_s[...] = jnp.zeros_like(sm_s)
        pv_s[...] = jnp.zeros_like(pv_s)

        def weigh(c, carry):
            r0 = tile_row0(c)
            vt = [v_ref[bb, pl.ds(r0, KEY_TILE), :] for bb in bbs]
            for g in range(HEAD_PAIRS):
                rows = slice(g * 2 * CHUNK, (g + 1) * 2 * CHUNK)
                ps = []
                for j in range(cw):
                    m = mb_s[j, rows]
                    p0 = jnp.exp(lg_s[j, c, rows, halves[0]] - m)
                    p1 = jnp.exp(lg_s[j, c, rows, halves[1]] - m)
                    sm_s[j, rows] = sm_s[j, rows] + (p0 + p1)
                    ps.append(jnp.concatenate([p0, p1], axis=1).astype(BF16))
                for j in range(cw):
                    pv_s[j, rows] = pv_s[j, rows] + _mm(ps[j], vt[j][:, g * LANES:(g + 1) * LANES])
            return carry

        lax.fori_loop(0, n_tiles, weigh, 0)

        for j, bb in enumerate(bbs):
            outn = pv_s[j] / jnp.sum(sm_s[j], axis=1, keepdims=True)
            cols = []
            for g in range(HEAD_PAIRS):
                top = outn[(2 * g) * CHUNK:(2 * g + 1) * CHUNK]
                bot = outn[(2 * g + 1) * CHUNK:(2 * g + 2) * CHUNK]
                cols.append(jnp.where(pair_lane_head == 0, top, bot))
            o_ref[bb] = jnp.concatenate(cols, axis=1).astype(BF16)
        return carry

    lax.fori_loop(0, nb // cw, phase_c, 0)


def _dsa(qi_t, q, kiw_p, k_p, v_p, band, ln_g, ln_b):
    b, s, _ = q.shape
    sp = k_p.shape[1]
    topk = min(MAX_TOPK, s // 4)
    assert s % KEY_TILE == 0, "the near window needs at least one whole key tile"
    max_tiles = s // KEY_TILE
    nb = 4 if b % 4 == 0 else (2 if b % 2 == 0 else 1)
    cw = nb
    rows = HEADS * CHUNK
    full = lambda a: pl.BlockSpec(a.shape, lambda i, j: (0,) * a.ndim)
    per_group = lambda n: pl.BlockSpec((nb, sp, n), lambda i, j: (i, 0, 0), pipeline_mode=pl.Buffered(1))
    return pl.pallas_call(
        functools.partial(_dsa_kernel, topk=topk, nb=nb),
        grid=(b // nb, s // CHUNK),
        in_specs=[
            pl.BlockSpec((IDX_HEADS, nb, CHUNK, 3 * IDX_DIM), lambda i, j: (0, i, j, 0)),
            pl.BlockSpec((nb, CHUNK, WIDTH), lambda i, j: (i, j, 0)),
            per_group(LANES), per_group(WIDTH), per_group(WIDTH),
            full(band), full(ln_g), full(ln_b),
        ],
        out_specs=pl.BlockSpec((nb, CHUNK, WIDTH), lambda i, j: (i, j, 0)),
        out_shape=jax.ShapeDtypeStruct((b, s, WIDTH), BF16),
        scratch_shapes=[
            pltpu.VMEM((nb, sp, 3 * IDX_DIM), BF16),
            pltpu.VMEM((nb, rows, LANES), F32),
            pltpu.VMEM((nb, max_tiles, CHUNK, KEY_TILE), F32),
            pltpu.VMEM(((nb + 1) // 2, max_tiles, KEY_TILE, 2 * CHUNK), F32),
            pltpu.VMEM((nb, CHUNK, LANES), F32),
            pltpu.VMEM((nb, CHUNK, LANES), I32),
            pltpu.VMEM((cw, max_tiles, rows, KEY_TILE), F32),
            pltpu.VMEM((cw, rows, LANES), F32),
            pltpu.VMEM((cw, rows, LANES), F32),
            pltpu.VMEM((cw, rows, LANES), F32),
            pltpu.VMEM((cw, rows, LANES), F32),
        ],
        compiler_params=pltpu.CompilerParams(dimension_semantics=("arbitrary", "arbitrary"),
                                             vmem_limit_bytes=VMEM_LIMIT),
        name="dsa",
    )(qi_t, q, kiw_p, k_p, v_p, band, ln_g, ln_b)


def _post_kernel(y1_ref, y2_ref, x_ref, p_ref, woa_ref, wob_ref, w1_ref, w2_ref, wp_ref, wg_ref,
                 g_mix_ref, g_f1_ref, g_f2_ref, g_p1_ref, g_p2_ref, o_ref):
    mix = _mm(y1_ref[...], woa_ref[...]) + _mm(y2_ref[...], wob_ref[...])
    x1 = x_ref[...] + _rms(mix, g_mix_ref[...])
    h = _rms(x1, g_f1_ref[...]).astype(BF16)
    hid = jnp.square(jnp.maximum(_mm(h, w1_ref[...]), 0.0)).astype(BF16)
    x2 = x1 + _rms(_mm(hid, w2_ref[...]), g_f2_ref[...])
    gate = _sigmoid(_mm(_rms(x2, g_p1_ref[...]).astype(BF16), wg_ref[...]))
    pe = _mm(p_ref[...].astype(BF16), wp_ref[...]) * gate
    o_ref[...] = x2 + _rms(pe, g_p2_ref[...])


def _post(y1, y2, x2d, p2d, woa, wob, w1, w2, wp, wg, gains, tm=512):
    m, d = x2d.shape
    row = lambda n: pl.BlockSpec((tm, n), lambda i: (i, 0))
    const = lambda a: pl.BlockSpec(a.shape, lambda i: (0, 0), pipeline_mode=pl.Buffered(1))
    weights = (woa, wob, w1, w2, wp, wg) + tuple(gains)
    return pl.pallas_call(
        _post_kernel,
        grid=(m // tm,),
        in_specs=[row(WIDTH), row(WIDTH), row(d), row(p2d.shape[1])] + [const(a) for a in weights],
        out_specs=row(d),
        out_shape=jax.ShapeDtypeStruct((m, d), F32),
        compiler_params=pltpu.CompilerParams(dimension_semantics=("arbitrary",), vmem_limit_bytes=VMEM_LIMIT),
        name="post",
    )(y1, y2, x2d, p2d, *weights)


def _layer(x, p, w_in, shift_mu, w0, w2r, a0, a2, g2, k_k, k_a, r_k, ln_g, ln_b, idx_g, idx_b, rel_bias,
           w_out, w_f1, w_f2, w_ple, w_gate, n_mix_pre, n_mix_post, n_ffn_pre, n_ffn_post, n_ple_pre, n_ple_post):
    b, s, d = x.shape
    m = b * s
    row = lambda a: a.reshape(1, -1).astype(F32)

    c = [0, WIDTH, WIDTH + DECAY_RANK, 2 * WIDTH + DECAY_RANK, 3 * WIDTH + DECAY_RANK,
         3 * WIDTH + DECAY_RANK + A_RANK, RWKV_COLS]
    order = [(c[0], c[1]), (c[2], c[3]), (c[3], c[4]), (c[1], c[2]), (c[4], c[5]), (c[5], c[6])]
    reorder = lambda a: jnp.concatenate([a[..., lo:hi] for lo, hi in order], axis=-1)
    w_main = jnp.concatenate([reorder(w_in[:, :RWKV_COLS]), w_in[:, RWKV_COLS:RWKV_COLS + 3 * WIDTH]],
                             axis=1).astype(BF16)
    w_idx = w_in[:, RWKV_COLS + 3 * WIDTH:]
    w_idx = jnp.pad(w_idx, ((0, 0), (0, N_IDX - w_idx.shape[1])))
    w_idx_hi = w_idx.astype(BF16)
    w_idx_lo = (w_idx - w_idx_hi.astype(F32)).astype(BF16)

    zr, q, k, v, qi, kiw = _in_proj(x.reshape(m, d), row(n_mix_pre), w_main, w_idx_hi, w_idx_lo)

    zeros = jnp.zeros((DECAY_RANK, WIDTH), F32)
    w_lora = jnp.concatenate([jnp.concatenate([w2r, zeros], axis=1),
                              jnp.concatenate([zeros, a2], axis=1)], axis=0).astype(BF16)
    y_rwkv = _rwkv(zr.reshape(b, s, RWKV_COLS), row(reorder(shift_mu)), row(w0), w_lora, row(a0), g2.astype(BF16),
                   row(k_k), row(k_a), row(r_k), row(ln_g), row(ln_b))

    band = _bias_tiles(rel_bias.astype(F32))
    per_row = lambda a: a.reshape(b, s, -1)
    y_dsa = _dsa(qi.reshape(IDX_HEADS, b, s, 3 * IDX_DIM), per_row(q), per_row(kiw), per_row(k), per_row(v), band,
                 row(idx_g), row(idx_b))

    gains = tuple(row(a) for a in (n_mix_post, n_ffn_pre, n_ffn_post, n_ple_pre, n_ple_post))
    out = _post(y_rwkv.reshape(m, WIDTH), y_dsa.reshape(m, WIDTH), x.reshape(m, d), p.reshape(m, -1),
                w_out[:WIDTH].astype(BF16), w_out[WIDTH:].astype(BF16), w_f1.astype(BF16), w_f2.astype(BF16),
                w_ple.astype(BF16), w_gate.astype(BF16), gains)
    return out.reshape(b, s, d)


def kernel(x, p, w_in, shift_mu, rwkv_w0, rwkv_w2, rwkv_a0, rwkv_a2, rwkv_g2, rwkv_k_k, rwkv_k_a, rwkv_r_k, rwkv_ln_g, rwkv_ln_b, idx_k_ln_g, idx_k_ln_b, rel_bias, w_out, w_ffn1, w_ffn2, w_ple, w_ple_gate, norm_mix_pre, norm_mix_post, norm_ffn_pre, norm_ffn_post, norm_ple_pre, norm_ple_post):
    for i in range(p.shape[0]):
        x = _layer(x, p[i], w_in[i], shift_mu[i], rwkv_w0[i], rwkv_w2[i], rwkv_a0[i], rwkv_a2[i], rwkv_g2[i],
                   rwkv_k_k[i], rwkv_k_a[i], rwkv_r_k[i], rwkv_ln_g[i], rwkv_ln_b[i], idx_k_ln_g[i],
                   idx_k_ln_b[i], rel_bias, w_out[i], w_ffn1[i], w_ffn2[i], w_ple[i], w_ple_gate[i],
                   norm_mix_pre[i], norm_mix_post[i], norm_ffn_pre[i], norm_ffn_post[i], norm_ple_pre[i],
                   norm_ple_post[i])
    return x
```

```python
import functools
import math

import jax
import jax.numpy as jnp
from jax import lax
from jax.experimental import pallas as pl
from jax.experimental.pallas import tpu as pltpu

F32 = jnp.float32
BF16 = jnp.bfloat16
I32 = jnp.int32

CHUNK = 64
HEADS = 8
HEAD_DIM = 64
WIDTH = HEADS * HEAD_DIM
DECAY_RANK = 64
A_RANK = 64
GATE_RANK = 128
GN_EPS = 64e-5
IDX_HEADS = 8
IDX_DIM = 64
MAX_TOPK = 256
REL_BUCKETS = 32
REL_MAX_DISTANCE = 128
NORM_EPS = 1e-6
LN_EPS = 1e-6
NEG_INF = -1e30
RWKV_COLS = 1792
N_MAIN = RWKV_COLS + 3 * WIDTH
N_IDX = 640

LANES = 128
KEY_TILE = 256
HEAD_PAIRS = WIDTH // LANES
GROUP = 4
ROW_GROUP = 4
FAR_GAP = 2
NEAR_VARIANTS = KEY_TILE // CHUNK
VMEM_LIMIT = 62 * 1024 * 1024

INT_MIN = -(2**31)
INT_MAX = 2**31 - 1


def _nt(a, b):
    return lax.dot_general(a, b, (((1,), (1,)), ((), ())), preferred_element_type=F32)


def _tn(a, b):
    return lax.dot_general(a, b, (((0,), (0,)), ((), ())), preferred_element_type=F32)


def _mm(a, b):
    return jnp.dot(a, b, preferred_element_type=F32)


def _split2(x):
    hi = x.astype(BF16)
    lo = (x - hi.astype(F32)).astype(BF16)
    return hi, lo


def _alternate(*gens):
    out = [None] * len(gens)
    live = list(range(len(gens)))
    while live:
        for n in list(live):
            try:
                next(gens[n])
            except StopIteration as stop:
                out[n] = stop.value
                live.remove(n)
    return out


def _rms(x, g):
    return x * lax.rsqrt(jnp.mean(x * x, axis=-1, keepdims=True) + NORM_EPS) * g


def _sigmoid(x):
    return 1.0 / (1.0 + jnp.exp(-x))


def _in_proj_kernel(x_ref, g_ref, wm_ref, wih_ref, wil_ref, zr_ref, q_ref, k_ref, v_ref, qi_ref, kiw_ref):
    h = _rms(x_ref[...], g_ref[...])
    hh, hl = _split2(h)
    zm = _mm(hh, wm_ref[...])
    zr_ref[...] = zm[:, :RWKV_COLS]
    q_ref[...] = zm[:, RWKV_COLS:RWKV_COLS + WIDTH].astype(BF16)
    k_ref[...] = zm[:, RWKV_COLS + WIDTH:RWKV_COLS + 2 * WIDTH].astype(BF16)
    v_ref[...] = zm[:, RWKV_COLS + 2 * WIDTH:].astype(BF16)
    zi = _mm(hh, wih_ref[...]) + _mm(hh, wil_ref[...]) + _mm(hl, wih_ref[...])
    qh, ql = _split2(zi[:, :WIDTH])
    for h in range(IDX_HEADS):
        sl = slice(h * IDX_DIM, (h + 1) * IDX_DIM)
        qi_ref[h] = jnp.concatenate([qh[:, sl], qh[:, sl], ql[:, sl]], axis=1)
    kiw_ref[...] = zi[:, WIDTH:]


def _in_proj(x2, g, wm, wih, wil, tm=512):
    m, d = x2.shape
    row = lambda n: pl.BlockSpec((tm, n), lambda i: (i, 0))
    full = lambda a: pl.BlockSpec(a.shape, lambda i: (0, 0))
    return pl.pallas_call(
        _in_proj_kernel,
        grid=(m // tm,),
        in_specs=[row(d), full(g), full(wm), full(wih), full(wil)],
        out_specs=[row(RWKV_COLS), row(WIDTH), row(WIDTH), row(WIDTH),
                   pl.BlockSpec((IDX_HEADS, tm, 3 * IDX_DIM), lambda i: (0, i, 0)), row(LANES)],
        out_shape=[
            jax.ShapeDtypeStruct((m, RWKV_COLS), F32),
            jax.ShapeDtypeStruct((m, WIDTH), BF16),
            jax.ShapeDtypeStruct((m, WIDTH), BF16),
            jax.ShapeDtypeStruct((m, WIDTH), BF16),
            jax.ShapeDtypeStruct((IDX_HEADS, m, 3 * IDX_DIM), BF16),
            jax.ShapeDtypeStruct((m, LANES), F32),
        ],
        compiler_params=pltpu.CompilerParams(dimension_semantics=("arbitrary",), vmem_limit_bytes=VMEM_LIMIT),
        name="in_proj",
    )(x2, g, wm, wih, wil)


def _t5_bucket(rel):
    half = REL_BUCKETS // 2
    max_exact = half // 2
    ret = jnp.where(rel > 0, half, 0)
    n = jnp.abs(rel)
    nf = jnp.maximum(n, 1).astype(F32)
    steps = jnp.log(nf / max_exact) / math.log(REL_MAX_DISTANCE / max_exact) * (half - max_exact)
    large = max_exact
    for level in range(1, half - max_exact):
        large = large + jnp.where(steps >= level - 1e-4, 1, 0)
    return ret + jnp.where(n < max_exact, n, large)


def _bias_kernel(tab_ref, band_ref):
    r = lax.broadcasted_iota(I32, (CHUNK, KEY_TILE), 0)
    c = lax.broadcasted_iota(I32, (CHUNK, KEY_TILE), 1)
    far_bucket = _t5_bucket(jnp.full((CHUNK, KEY_TILE), -(FAR_GAP * CHUNK + 1), I32))
    for m in range(NEAR_VARIANTS):
        near_bucket = _t5_bucket(c - m * CHUNK - r)
        for h in range(HEADS):
            near = jnp.zeros((CHUNK, KEY_TILE), F32)
            far = jnp.zeros((CHUNK, KEY_TILE), F32)
            for j in range(REL_BUCKETS):
                t = tab_ref[j, h]
                near = jnp.where(near_bucket == j, t, near)
                far = jnp.where(far_bucket == j, t, far)
            band_ref[m, h * CHUNK:(h + 1) * CHUNK, :] = near - far


def _bias_tiles(rel_bias):
    return pl.pallas_call(
        _bias_kernel,
        in_specs=[pl.BlockSpec(memory_space=pltpu.SMEM)],
        out_shape=jax.ShapeDtypeStruct((NEAR_VARIANTS, HEADS * CHUNK, KEY_TILE), F32),
        name="bias_tiles",
    )(rel_bias)


def _seg_sum(x):
    rows = x.shape[0]
    li = lax.broadcasted_iota(I32, (LANES, LANES), 0) // HEAD_DIM
    lj = lax.broadcasted_iota(I32, (LANES, LANES), 1) // HEAD_DIM
    ones = (li == lj).astype(BF16)
    xs = jnp.concatenate([x[:, g * LANES:(g + 1) * LANES] for g in range(HEAD_PAIRS)], axis=0)
    y = _mm(xs.astype(BF16), ones)
    return jnp.concatenate([y[g * rows:(g + 1) * rows] for g in range(HEAD_PAIRS)], axis=1)


def _rwkv_kernel(z_ref, mu_ref, w0_ref, wl_ref, a0_ref, g2_ref, kk_ref, ka_ref, rk_ref, lng_ref, lnb_ref,
                 y_ref, prev_s, st_s, *, nb):
    @pl.when(pl.program_id(1) == 0)
    def _():
        prev_s[...] = jnp.zeros_like(prev_s)
        st_s[...] = jnp.zeros_like(st_s)

    gw = GROUP * HEAD_DIM
    lane_h = lax.broadcasted_iota(I32, (CHUNK, gw), 1) // HEAD_DIM
    src = lax.broadcasted_iota(I32, (CHUNK, gw), 1) & (HEAD_DIM - 1)
    tok = lax.broadcasted_iota(I32, (CHUNK, gw), 0)
    strict = tok > src
    incl = tok >= src
    eye = (tok == src).astype(F32)
    diag_blk = (lax.broadcasted_iota(I32, (gw, gw), 0) // HEAD_DIM) == (lax.broadcasted_iota(I32, (gw, gw), 1) // HEAD_DIM)

    def bdiag(x):
        xb = x.astype(BF16)
        zero = jnp.zeros_like(xb)
        return jnp.concatenate([jnp.where(lane_h == h, xb, zero) for h in range(GROUP)], axis=0)

    def prepare(rb0, nr):
        rows = nr * CHUNK
        z = z_ref[rb0:rb0 + nr].reshape(rows, RWKV_COLS)
        rolled = pltpu.roll(z, 1, axis=0)
        first = lax.broadcasted_iota(I32, (8, RWKV_COLS), 0) == 0
        pieces = []
        for j in range(nr):
            base = j * CHUNK
            pieces += [jnp.where(first, prev_s[rb0 + j], rolled[base:base + 8]), rolled[base + 8:base + CHUNK]]
            prev_s[rb0 + j] = z[base + CHUNK - 1:base + CHUNK, :]
        zp = jnp.concatenate(pieces, axis=0)
        zs = z + (zp - z) * mu_ref[...]
        yield
        r = zs[:, 0:WIDTH]
        k = zs[:, WIDTH:2 * WIDTH]
        v = zs[:, 2 * WIDTH:3 * WIDTH]
        zl = zs[:, 3 * WIDTH:3 * WIDTH + LANES]
        zg = zs[:, 3 * WIDTH + LANES:]
        lane = lax.broadcasted_iota(I32, zl.shape, 1)
        zl = jnp.where(lane < DECAY_RANK, jnp.tanh(zl), zl)
        la = _mm(zl.astype(BF16), wl_ref[...])
        ld = -math.exp(-0.5) * _sigmoid(w0_ref[...] + la[:, :WIDTH])
        yield
        a = _sigmoid(a0_ref[...] + la[:, WIDTH:])
        gate = _mm(_sigmoid(zg).astype(BF16), g2_ref[...])
        yield
        kk = k * kk_ref[...]
        kk = kk * jnp.minimum(lax.rsqrt(_seg_sum(kk * kk)), 1e12)
        yield
        kmod = k * (1.0 + (a - 1.0) * ka_ref[...])
        bvec = kk * a
        yield
        ti = lax.broadcasted_iota(I32, (rows, rows), 0)
        tj = lax.broadcasted_iota(I32, (rows, rows), 1)
        tri = ((ti >= tj) & (ti // CHUNK == tj // CHUNK)).astype(BF16)
        l1, l2 = _split2(ld)
        cum = _mm(tri, l1) + _mm(tri, l2)
        cum_x = cum - ld
        yield
        row_w = lax.broadcasted_iota(I32, (rows, WIDTH), 0)
        cref = cum[CHUNK // 2 - 1:CHUNK // 2, :]
        clast = cum[CHUNK - 1:CHUNK, :]
        for j in range(1, nr):
            mid = j * CHUNK + CHUNK // 2 - 1
            cref = jnp.where(row_w >= j * CHUNK, cum[mid:mid + 1, :], cref)
            clast = jnp.where(row_w >= j * CHUNK, cum[(j + 1) * CHUNK - 1:(j + 1) * CHUNK, :], clast)
        yield
        d = dict(r=r, v=v, kmod=kmod, gate=gate, cum=cum)
        e_neg = jnp.exp(cref - cum)
        d["rt"] = r * jnp.exp(cum - cref)
        yield
        d["at"] = -kk * jnp.exp(cum_x - cref)
        d["kt"] = kmod * e_neg
        yield
        d["bt"] = bvec * e_neg
        d["r0"] = r * jnp.exp(cum)
        yield
        d["a0v"] = -kk * jnp.exp(cum_x)
        e_end = jnp.exp(clast - cum)
        yield
        d["kc"] = kmod * e_end
        d["bc"] = bvec * e_end
        return d

    def chains(rb0, nr, d):
        units = [(j, g, slice(j * CHUNK, (j + 1) * CHUNK), slice(g * gw, (g + 1) * gw))
                 for j in range(nr) for g in range(HEADS // GROUP)]
        cut = lambda x: [x[rs, gs] for _, _, rs, gs in units]
        at_w, rt_w, bt_w, kt_w, v_w = cut(d["at"]), cut(d["rt"]), cut(d["bt"]), cut(d["kt"]), cut(d["v"])
        lhs = [jnp.concatenate([a_, r_], axis=0).astype(BF16) for a_, r_ in zip(at_w, rt_w)]
        ab_rb = [_nt(l_, bdiag(b_)) for l_, b_ in zip(lhs, bt_w)]
        yield
        ak_rk = [_nt(l_, bdiag(k_)) for l_, k_ in zip(lhs, kt_w)]
        low = [jnp.where(strict, x[:CHUNK], 0.0) for x in ab_rb]
        a_ak = [jnp.where(strict, x[:CHUNK], 0.0).astype(BF16) for x in ak_rk]
        a_rb = [jnp.where(incl, x[CHUNK:], 0.0).astype(BF16) for x in ab_rb]
        a_rk = [jnp.where(incl, x[CHUNK:], 0.0).astype(BF16) for x in ak_rk]
        yield
        v_bd = [bdiag(x) for x in v_w]
        s_f = [st_s[rb0 + j, g] for j, g, _, _ in units]
        s_b = [s.astype(BF16) for s in s_f]
        x0 = [_nt(x.astype(BF16), s) for x, s in zip(cut(d["a0v"]), s_b)]
        o_st = [_nt(x.astype(BF16), s) for x, s in zip(cut(d["r0"]), s_b)]
        yk = [_mm(a_, vb) for a_, vb in zip(a_ak, v_bd)]
        yield
        t = [eye + x for x in low]
        p = low
        p_bd = [bdiag(x) for x in p]
        for _ in range(int(math.log2(CHUNK)) - 1):
            p = [_mm(x.astype(BF16), xb) for x, xb in zip(p, p_bd)]
            p_bd = [bdiag(x) for x in p]
            yield
            t = [t_ + _mm(t_.astype(BF16), xb) for t_, xb in zip(t, p_bd)]
            yield
        u = [_mm(t_.astype(BF16), bdiag(a_ + b_)) for t_, a_, b_ in zip(t, x0, yk)]
        yield
        o_w = [a_ + _mm(rb, bdiag(u_)) + _mm(rk, vb) for a_, rb, u_, rk, vb in zip(o_st, a_rb, u, a_rk, v_bd)]
        uv = [jnp.concatenate([u_, v_], axis=0).astype(BF16) for u_, v_ in zip(u, v_w)]
        bk = [jnp.concatenate([b_, k_], axis=0).astype(BF16) for b_, k_ in zip(cut(d["bc"]), cut(d["kc"]))]
        yield
        for (j, gi, _, gs), s, w, x in zip(units, s_f, uv, bk):
            pc = jnp.exp(d["cum"][(j + 1) * CHUNK - 1:(j + 1) * CHUNK, gs])
            st_s[rb0 + j, gi] = s * pc + jnp.where(diag_blk, _tn(w, x), 0.0)
        groups = HEADS // GROUP
        return jnp.concatenate([jnp.concatenate(o_w[j * groups:(j + 1) * groups], axis=1) for j in range(nr)], axis=0)

    def finish(rb0, nr, d, o):
        inv_n = 1.0 / HEAD_DIM
        mean = _seg_sum(o) * inv_n
        dev = o - mean
        yield
        var = _seg_sum(dev * dev) * inv_n
        on = dev * lax.rsqrt(var + GN_EPS) * lng_ref[...] + lnb_ref[...]
        yield
        bonus = _seg_sum(d["r"] * d["kmod"] * rk_ref[...]) * d["v"]
        y_ref[rb0:rb0 + nr] = ((on + bonus) * d["gate"]).astype(BF16).reshape(nr, CHUNK, WIDTH)

    alternate = _alternate
    spans = [(rb0, min(ROW_GROUP, nb - rb0)) for rb0 in range(0, nb, ROW_GROUP)]
    data, = alternate(prepare(*spans[0]))
    done = None
    for n, span in enumerate(spans):
        stages = [chains(*span, data)]
        if n + 1 < len(spans):
            stages.append(prepare(*spans[n + 1]))
        if done is not None:
            stages.append(finish(*done))
        res = alternate(*stages)
        done = (*span, data, res[0])
        if n + 1 < len(spans):
            data = res[1]
    alternate(finish(*done))


def _rwkv(zr, mu, w0, wl, a0, g2, k_k, k_a, r_k, ln_g, ln_b, nb_max=8):
    b, s, _ = zr.shape
    nb = max(n for n in range(1, nb_max + 1) if b % n == 0)
    full = lambda a: pl.BlockSpec(a.shape, lambda i, j: (0,) * a.ndim)
    params = (mu, w0, wl, a0, g2, k_k, k_a, r_k, ln_g, ln_b)
    return pl.pallas_call(
        functools.partial(_rwkv_kernel, nb=nb),
        grid=(b // nb, s // CHUNK),
        in_specs=[pl.BlockSpec((nb, CHUNK, RWKV_COLS), lambda i, j: (i, j, 0))] + [full(a) for a in params],
        out_specs=pl.BlockSpec((nb, CHUNK, WIDTH), lambda i, j: (i, j, 0)),
        out_shape=jax.ShapeDtypeStruct((b, s, WIDTH), BF16),
        scratch_shapes=[
            pltpu.VMEM((nb, 1, RWKV_COLS), F32),
            pltpu.VMEM((nb, HEADS // GROUP, GROUP * HEAD_DIM, GROUP * HEAD_DIM), F32),
        ],
        compiler_params=pltpu.CompilerParams(dimension_semantics=("arbitrary", "arbitrary"),
                                             vmem_limit_bytes=VMEM_LIMIT),
        name="rwkv",
    )(zr, *params)


def _key_to_float(key):
    return lax.bitcast_convert_type(key ^ ((key >> 31) & INT_MAX), F32)


def _dsa_kernel(qi_ref, q_ref, kiw_ref, k_ref, v_ref, band_ref, lng_ref, lnb_ref, o_ref,
                ki3_s, wb_s, key_s, keyt_s, thr_s, jlim_s, lg_s, mx_s, mb_s, sm_s, pv_s, *, topk, nb):
    i = pl.program_id(1)
    sp = kiw_ref.shape[1]
    far_lim = jnp.clip(CHUNK * (i - FAR_GAP), 0, sp - KEY_TILE)
    n_far = (far_lim + KEY_TILE - 1) // KEY_TILE
    n_tiles = n_far + 1
    near_variant = i - far_lim // CHUNK
    lane1 = lax.broadcasted_iota(I32, (CHUNK, LANES), 1)
    halves = [slice(0, LANES), slice(LANES, KEY_TILE)]
    n_pairs = (nb + 1) // 2
    kf = float(topk)

    def tile_pos0(c):
        return jnp.where(c < n_far, c * KEY_TILE, far_lim)

    def tile_end(c):
        return jnp.where(c < n_far, far_lim, CHUNK * (i + 1))

    def tile_row0(c):
        return pl.multiple_of(tile_pos0(c), CHUNK)

    def half_pos(c, half):
        return tile_pos0(c) + half * LANES + lane1

    @pl.when(i == 0)
    def _():
        rows = KEY_TILE if sp % KEY_TILE == 0 else CHUNK

        def body(t, carry):
            bb = t // (sp // rows)
            r0 = pl.multiple_of((t % (sp // rows)) * rows, CHUNK)
            ki = kiw_ref[bb, pl.ds(r0, rows), :][:, :IDX_DIM]
            mean = jnp.mean(ki, axis=-1, keepdims=True)
            var = jnp.mean(jnp.square(ki - mean), axis=-1, keepdims=True)
            kn = (ki - mean) * lax.rsqrt(var + LN_EPS) * lng_ref[...] + lnb_ref[...]
            hi, lo = _split2(kn)
            ki3_s[bb, pl.ds(r0, rows), :] = jnp.concatenate([hi, lo, hi], axis=1)
            return carry

        lax.fori_loop(0, nb * (sp // rows), body, 0)

    def stage_queries(bb, carry):
        wq = kiw_ref[bb, pl.ds(pl.multiple_of(CHUNK * i, CHUNK), CHUNK), :]
        wscale = IDX_HEADS ** -0.5 * IDX_DIM ** -0.5
        wcol = jnp.concatenate([wq[:, IDX_DIM + h:IDX_DIM + h + 1] for h in range(IDX_HEADS)], axis=0) * wscale
        wb_s[bb] = jnp.broadcast_to(wcol, (IDX_HEADS * CHUNK, LANES))
        return carry

    lax.fori_loop(0, nb, stage_queries, 0)

    def score_tile(c, carry):
        r0 = tile_row0(c)
        end = tile_end(c)
        xs = [_nt(qi_ref[:, bb].reshape(IDX_HEADS * CHUNK, 3 * IDX_DIM), ki3_s[bb, pl.ds(r0, KEY_TILE), :])
              for bb in range(nb)]
        keys = []
        for bb in range(nb):
            wb = wb_s[bb]
            parts = []
            for half, hs_ in enumerate(halves):
                act = jnp.maximum(xs[bb][:, hs_], 0.0) * wb
                sc = act[0:CHUNK]
                for h in range(1, IDX_HEADS):
                    sc = sc + act[h * CHUNK:(h + 1) * CHUNK]
                pos = half_pos(c, half)
                valid = (pos >= 0) & (pos < end)
                parts.append(jnp.where(valid, sc, -jnp.inf))
            keys.append(jnp.concatenate(parts, axis=1))
            key_s[bb, c] = keys[bb]
        for pr in range(n_pairs):
            both = jnp.concatenate([keys[2 * pr], keys[min(2 * pr + 1, nb - 1)]], axis=0)
            keyt_s[pr, c] = both.T
        return carry

    lax.fori_loop(0, n_tiles, score_tile, 0)

    thr_s[...] = jnp.full(thr_s.shape, -jnp.inf, F32)
    jlim_s[...] = jnp.full(jlim_s.shape, INT_MAX, I32)

    part = 32

    def count(cands, above):
        wide = [jnp.broadcast_to(cd, (part, LANES)) for cd in cands]

        def body(c, accs):
            out = []
            for pr in range(n_pairs):
                acc = accs[pr]
                for k in range(KEY_TILE // part):
                    sc = keyt_s[pr, c, k * part:(k + 1) * part, :]
                    acc = jnp.where(above(sc, wide[pr]), acc + 1.0, acc)
                out.append(acc)
            return tuple(out)

        accs = lax.fori_loop(0, n_tiles, body, tuple(jnp.zeros((part, LANES), F32) for _ in range(n_pairs)))
        return [jnp.sum(a, axis=0, keepdims=True) for a in accs]

    def count_ge(keys):
        return count([_key_to_float(k) for k in keys], lambda s, c: s >= c)

    @pl.when(CHUNK * (i + 1) > topk)
    def _():
        n_all = float(KEY_TILE) * n_tiles.astype(F32)
        n_pos = count_ge([jnp.zeros((1, LANES), I32)] * n_pairs)
        ans0 = tuple(jnp.where(n >= kf, 0, INT_MIN).astype(I32) for n in n_pos)
        cnt0 = tuple(jnp.where(n >= kf, n, n_all) for n in n_pos)

        def one_bit(t, carry):
            ans, n_ans = carry
            bit = lax.shift_left(jnp.int32(1), 30 - t)
            cands = [a | bit for a in ans]
            cnts = count_ge(cands)
            take = [n >= kf for n in cnts]
            return (tuple(jnp.where(tk, cd, a) for tk, cd, a in zip(take, cands, ans)),
                    tuple(jnp.where(tk, n, na) for tk, n, na in zip(take, cnts, n_ans)))

        ans, n_ans = lax.fori_loop(0, 31, one_bit, (ans0, cnt0))
        def to_columns(rows_, dst):
            for pr in range(n_pairs):
                col = jnp.broadcast_to(rows_[pr], (LANES, LANES)).T
                dst[2 * pr] = col[:CHUNK]
                if 2 * pr + 1 < nb:
                    dst[2 * pr + 1] = col[CHUNK:]

        kth = [_key_to_float(a) for a in ans]
        to_columns(kth, thr_s)
        most = n_ans[0]
        for pr in range(1, n_pairs):
            most = jnp.maximum(most, n_ans[pr])

        @pl.when(jnp.max(most) > kf)
        def _():
            n_gt = count(kth, lambda s, c: s > c)
            need = [kf - n for n in n_gt]
            ans_w = [jnp.broadcast_to(a, (KEY_TILE, LANES)) for a in kth]
            krow = lax.broadcasted_iota(I32, (KEY_TILE, LANES), 0)
            tri = (lax.broadcasted_iota(I32, (KEY_TILE, KEY_TILE), 0)
                   >= lax.broadcasted_iota(I32, (KEY_TILE, KEY_TILE), 1)).astype(BF16)

            def body(c, carry):
                seen, last = carry
                pos = tile_pos0(c) + krow
                seen_out, last_out = [], []
                for pr in range(n_pairs):
                    tie = keyt_s[pr, c] == ans_w[pr]
                    rank = _mm(tri, jnp.where(tie, 1.0, 0.0).astype(BF16)) + seen[pr]
                    kept = jnp.where(tie & (rank <= need[pr]), pos, -1)
                    last_out.append(jnp.maximum(last[pr], jnp.max(kept, axis=0, keepdims=True)))
                    seen_out.append(rank[KEY_TILE - 1:KEY_TILE, :])
                return tuple(seen_out), tuple(last_out)

            init = (tuple(jnp.zeros((1, LANES), F32) for _ in range(n_pairs)),
                    tuple(jnp.full((1, LANES), -1, I32) for _ in range(n_pairs)))
            _, last = lax.fori_loop(0, n_tiles, body, init)
            to_columns(last, jlim_s)

    pair_lane_head = lane1 // HEAD_DIM

    cw = lg_s.shape[0]

    def phase_c(step, carry):
        bbs = [step * cw + j for j in range(cw)]
        zero = jnp.zeros((CHUNK, LANES), BF16)
        qm = []
        for bb in bbs:
            qs = q_ref[bb] * jnp.asarray(HEAD_DIM ** -0.5, BF16)
            qm.append([jnp.concatenate([jnp.where(pair_lane_head == 0, qs[:, g * LANES:(g + 1) * LANES], zero),
                                        jnp.where(pair_lane_head == 1, qs[:, g * LANES:(g + 1) * LANES], zero)],
                                       axis=0) for g in range(HEAD_PAIRS)])
        thr = [thr_s[bb] for bb in bbs]
        jlim = [jlim_s[bb] for bb in bbs]
        mx_s[...] = jnp.full(mx_s.shape, NEG_INF, F32)

        def logits(c, bias):
            r0 = tile_row0(c)
            end = tile_end(c)
            kt = [k_ref[bb, pl.ds(r0, KEY_TILE), :] for bb in bbs]
            s = [[_nt(qm[j][g], kt[j][:, g * LANES:(g + 1) * LANES]) for g in range(HEAD_PAIRS)]
                 for j in range(cw)]
            for j, bb in enumerate(bbs):
                sel = []
                for half, hs_ in enumerate(halves):
                    key = key_s[bb, c, :, hs_]
                    pos = half_pos(c, half)
                    sel.append((pos >= 0) & (pos < end) & ((key > thr[j]) | ((key == thr[j]) & (pos <= jlim[j]))))
                for g in range(HEAD_PAIRS):
                    sg = s[j][g]
                    if bias is not None:
                        sg = sg + bias[g * 2 * CHUNK:(g + 1) * 2 * CHUNK]
                    for hh in range(2):
                        rows = slice((2 * g + hh) * CHUNK, (2 * g + hh + 1) * CHUNK)
                        sub = sg[hh * CHUNK:(hh + 1) * CHUNK]
                        m0 = jnp.where(sel[0], sub[:, halves[0]], NEG_INF)
                        m1 = jnp.where(sel[1], sub[:, halves[1]], NEG_INF)
                        lg_s[j, c, rows, halves[0]] = m0
                        lg_s[j, c, rows, halves[1]] = m1
                        mx_s[j, rows] = jnp.maximum(mx_s[j, rows], jnp.maximum(m0, m1))

        def far_logits(c, carry):
            logits(c, None)
            return carry

        lax.fori_loop(0, n_far, far_logits, 0)
        logits(n_far, band_ref[near_variant])

        for j in range(cw):
            mb_s[j] = jnp.broadcast_to(jnp.max(mx_s[j], axis=1, keepdims=True), mb_s.shape[1:])
        sm_s[...] = jnp.zeros_like(sm_s)
        pv_s[...] = jnp.zeros_like(pv_s)

        def weigh(c, carry):
            r0 = tile_row0(c)
            vt = [v_ref[bb, pl.ds(r0, KEY_TILE), :] for bb in bbs]
            for g in range(HEAD_PAIRS):
                rows = slice(g * 2 * CHUNK, (g + 1) * 2 * CHUNK)
                ps = []
                for j in range(cw):
                    m = mb_s[j, rows]
                    p0 = jnp.exp(lg_s[j, c, rows, halves[0]] - m)
                    p1 = jnp.exp(lg_s[j, c, rows, halves[1]] - m)
                    sm_s[j, rows] = sm_s[j, rows] + (p0 + p1)
                    ps.append(jnp.concatenate([p0, p1], axis=1).astype(BF16))
                for j in range(cw):
                    pv_s[j, rows] = pv_s[j, rows] + _mm(ps[j], vt[j][:, g * LANES:(g + 1) * LANES])
            return carry

        lax.fori_loop(0, n_tiles, weigh, 0)

        for j, bb in enumerate(bbs):
            outn = pv_s[j] / jnp.sum(sm_s[j], axis=1, keepdims=True)
            cols = []
            for g in range(HEAD_PAIRS):
                top = outn[(2 * g) * CHUNK:(2 * g + 1) * CHUNK]
                bot = outn[(2 * g + 1) * CHUNK:(2 * g + 2) * CHUNK]
                cols.append(jnp.where(pair_lane_head == 0, top, bot))
            o_ref[bb] = jnp.concatenate(cols, axis=1).astype(BF16)
        return carry

    lax.fori_loop(0, nb // cw, phase_c, 0)


def _dsa(qi_t, q, kiw_p, k_p, v_p, band, ln_g, ln_b):
    b, s, _ = q.shape
    sp = k_p.shape[1]
    topk = min(MAX_TOPK, s // 4)
    assert s % KEY_TILE == 0, "the near window needs at least one whole key tile"
    max_tiles = s // KEY_TILE
    nb = 4 if b % 4 == 0 else (2 if b % 2 == 0 else 1)
    cw = nb
    rows = HEADS * CHUNK
    full = lambda a: pl.BlockSpec(a.shape, lambda i, j: (0,) * a.ndim)
    per_group = lambda n: pl.BlockSpec((nb, sp, n), lambda i, j: (i, 0, 0), pipeline_mode=pl.Buffered(1))
    return pl.pallas_call(
        functools.partial(_dsa_kernel, topk=topk, nb=nb),
        grid=(b // nb, s // CHUNK),
        in_specs=[
            pl.BlockSpec((IDX_HEADS, nb, CHUNK, 3 * IDX_DIM), lambda i, j: (0, i, j, 0)),
            pl.BlockSpec((nb, CHUNK, WIDTH), lambda i, j: (i, j, 0)),
            per_group(LANES), per_group(WIDTH), per_group(WIDTH),
            full(band), full(ln_g), full(ln_b),
        ],
        out_specs=pl.BlockSpec((nb, CHUNK, WIDTH), lambda i, j: (i, j, 0)),
        out_shape=jax.ShapeDtypeStruct((b, s, WIDTH), BF16),
        scratch_shapes=[
            pltpu.VMEM((nb, sp, 3 * IDX_DIM), BF16),
            pltpu.VMEM((nb, rows, LANES), F32),
            pltpu.VMEM((nb, max_tiles, CHUNK, KEY_TILE), F32),
            pltpu.VMEM(((nb + 1) // 2, max_tiles, KEY_TILE, 2 * CHUNK), F32),
            pltpu.VMEM((nb, CHUNK, LANES), F32),
            pltpu.VMEM((nb, CHUNK, LANES), I32),
            pltpu.VMEM((cw, max_tiles, rows, KEY_TILE), F32),
            pltpu.VMEM((cw, rows, LANES), F32),
            pltpu.VMEM((cw, rows, LANES), F32),
            pltpu.VMEM((cw, rows, LANES), F32),
            pltpu.VMEM((cw, rows, LANES), F32),
        ],
        compiler_params=pltpu.CompilerParams(dimension_semantics=("arbitrary", "arbitrary"),
                                             vmem_limit_bytes=VMEM_LIMIT),
        name="dsa",
    )(qi_t, q, kiw_p, k_p, v_p, band, ln_g, ln_b)


def _post_kernel(y1_ref, y2_ref, x_ref, p_ref, woa_ref, wob_ref, w1_ref, w2_ref, wp_ref, wg_ref,
                 g_mix_ref, g_f1_ref, g_f2_ref, g_p1_ref, g_p2_ref, o_ref):
    mix = _mm(y1_ref[...], woa_ref[...]) + _mm(y2_ref[...], wob_ref[...])
    x1 = x_ref[...] + _rms(mix, g_mix_ref[...])
    h = _rms(x1, g_f1_ref[...]).astype(BF16)
    hid = jnp.square(jnp.maximum(_mm(h, w1_ref[...]), 0.0)).astype(BF16)
    x2 = x1 + _rms(_mm(hid, w2_ref[...]), g_f2_ref[...])
    gate = _sigmoid(_mm(_rms(x2, g_p1_ref[...]).astype(BF16), wg_ref[...]))
    pe = _mm(p_ref[...].astype(BF16), wp_ref[...]) * gate
    o_ref[...] = x2 + _rms(pe, g_p2_ref[...])


def _post(y1, y2, x2d, p2d, woa, wob, w1, w2, wp, wg, gains, tm=512):
    m, d = x2d.shape
    row = lambda n: pl.BlockSpec((tm, n), lambda i: (i, 0))
    const = lambda a: pl.BlockSpec(a.shape, lambda i: (0, 0), pipeline_mode=pl.Buffered(1))
    weights = (woa, wob, w1, w2, wp, wg) + tuple(gains)
    return pl.pallas_call(
        _post_kernel,
        grid=(m // tm,),
        in_specs=[row(WIDTH), row(WIDTH), row(d), row(p2d.shape[1])] + [const(a) for a in weights],
        out_specs=row(d),
        out_shape=jax.ShapeDtypeStruct((m, d), F32),
        compiler_params=pltpu.CompilerParams(dimension_semantics=("arbitrary",), vmem_limit_bytes=VMEM_LIMIT),
        name="post",
    )(y1, y2, x2d, p2d, *weights)


def _layer(x, p, w_in, shift_mu, w0, w2r, a0, a2, g2, k_k, k_a, r_k, ln_g, ln_b, idx_g, idx_b, rel_bias,
           w_out, w_f1, w_f2, w_ple, w_gate, n_mix_pre, n_mix_post, n_ffn_pre, n_ffn_post, n_ple_pre, n_ple_post):
    b, s, d = x.shape
    m = b * s
    row = lambda a: a.reshape(1, -1).astype(F32)

    c = [0, WIDTH, WIDTH + DECAY_RANK, 2 * WIDTH + DECAY_RANK, 3 * WIDTH + DECAY_RANK,
         3 * WIDTH + DECAY_RANK + A_RANK, RWKV_COLS]
    order = [(c[0], c[1]), (c[2], c[3]), (c[3], c[4]), (c[1], c[2]), (c[4], c[5]), (c[5], c[6])]
    reorder = lambda a: jnp.concatenate([a[..., lo:hi] for lo, hi in order], axis=-1)
    w_main = jnp.concatenate([reorder(w_in[:, :RWKV_COLS]), w_in[:, RWKV_COLS:RWKV_COLS + 3 * WIDTH]],
                             axis=1).astype(BF16)
    w_idx = w_in[:, RWKV_COLS + 3 * WIDTH:]
    w_idx = jnp.pad(w_idx, ((0, 0), (0, N_IDX - w_idx.shape[1])))
    w_idx_hi = w_idx.astype(BF16)
    w_idx_lo = (w_idx - w_idx_hi.astype(F32)).astype(BF16)

    zr, q, k, v, qi, kiw = _in_proj(x.reshape(m, d), row(n_mix_pre), w_main, w_idx_hi, w_idx_lo)

    zeros = jnp.zeros((DECAY_RANK, WIDTH), F32)
    w_lora = jnp.concatenate([jnp.concatenate([w2r, zeros], axis=1),
                              jnp.concatenate([zeros, a2], axis=1)], axis=0).astype(BF16)
    y_rwkv = _rwkv(zr.reshape(b, s, RWKV_COLS), row(reorder(shift_mu)), row(w0), w_lora, row(a0), g2.astype(BF16),
                   row(k_k), row(k_a), row(r_k), row(ln_g), row(ln_b))

    band = _bias_tiles(rel_bias.astype(F32))
    per_row = lambda a: a.reshape(b, s, -1)
    y_dsa = _dsa(qi.reshape(IDX_HEADS, b, s, 3 * IDX_DIM), per_row(q), per_row(kiw), per_row(k), per_row(v), band,
                 row(idx_g), row(idx_b))

    gains = tuple(row(a) for a in (n_mix_post, n_ffn_pre, n_ffn_post, n_ple_pre, n_ple_post))
    out = _post(y_rwkv.reshape(m, WIDTH), y_dsa.reshape(m, WIDTH), x.reshape(m, d), p.reshape(m, -1),
                w_out[:WIDTH].astype(BF16), w_out[WIDTH:].astype(BF16), w_f1.astype(BF16), w_f2.astype(BF16),
                w_ple.astype(BF16), w_gate.astype(BF16), gains)
    return out.reshape(b, s, d)


def kernel(x, p, w_in, shift_mu, rwkv_w0, rwkv_w2, rwkv_a0, rwkv_a2, rwkv_g2, rwkv_k_k, rwkv_k_a, rwkv_r_k, rwkv_ln_g, rwkv_ln_b, idx_k_ln_g, idx_k_ln_b, rel_bias, w_out, w_ffn1, w_ffn2, w_ple, w_ple_gate, norm_mix_pre, norm_mix_post, norm_ffn_pre, norm_ffn_post, norm_ple_pre, norm_ple_post):
    for i in range(p.shape[0]):
        x = _layer(x, p[i], w_in[i], shift_mu[i], rwkv_w0[i], rwkv_w2[i], rwkv_a0[i], rwkv_a2[i], rwkv_g2[i],
                   rwkv_k_k[i], rwkv_k_a[i], rwkv_r_k[i], rwkv_ln_g[i], rwkv_ln_b[i], idx_k_ln_g[i],
                   idx_k_ln_b[i], rel_bias, w_out[i], w_ffn1[i], w_ffn2[i], w_ple[i], w_ple_gate[i],
                   norm_mix_pre[i], norm_mix_post[i], norm_ffn_pre[i], norm_ffn_post[i], norm_ple_pre[i],
                   norm_ple_post[i])
    return x
```

```python
import functools
import math

import jax
import jax.numpy as jnp
from jax import lax
from jax.experimental import pallas as pl
from jax.experimental.pallas import tpu as pltpu

F32 = jnp.float32
BF16 = jnp.bfloat16
I32 = jnp.int32

CHUNK = 64
HEADS = 8
HEAD_DIM = 64
WIDTH = HEADS * HEAD_DIM
DECAY_RANK = 64
A_RANK = 64
GATE_RANK = 128
GN_EPS = 64e-5
IDX_HEADS = 8
IDX_DIM = 64
MAX_TOPK = 256
REL_BUCKETS = 32
REL_MAX_DISTANCE = 128
NORM_EPS = 1e-6
LN_EPS = 1e-6
NEG_INF = -1e30
RWKV_COLS = 1792
N_MAIN = RWKV_COLS + 3 * WIDTH
N_IDX = 640

LANES = 128
KEY_TILE = 256
HEAD_PAIRS = WIDTH // LANES
GROUP = 4
ROW_GROUP = 4
FAR_GAP = 2
NEAR_VARIANTS = KEY_TILE // CHUNK
VMEM_LIMIT = 62 * 1024 * 1024

INT_MIN = -(2**31)
INT_MAX = 2**31 - 1


def _nt(a, b):
    return lax.dot_general(a, b, (((1,), (1,)), ((), ())), preferred_element_type=F32)


def _tn(a, b):
    return lax.dot_general(a, b, (((0,), (0,)), ((), ())), preferred_element_type=F32)


def _mm(a, b):
    return jnp.dot(a, b, preferred_element_type=F32)


def _split2(x):
    hi = x.astype(BF16)
    lo = (x - hi.astype(F32)).astype(BF16)
    return hi, lo


def _alternate(*gens):
    out = [None] * len(gens)
    live = list(range(len(gens)))
    while live:
        for n in list(live):
            try:
                next(gens[n])
            except StopIteration as stop:
                out[n] = stop.value
                live.remove(n)
    return out


def _rms(x, g):
    return x * lax.rsqrt(jnp.mean(x * x, axis=-1, keepdims=True) + NORM_EPS) * g


def _sigmoid(x):
    return 1.0 / (1.0 + jnp.exp(-x))


def _in_proj_kernel(x_ref, g_ref, wm_ref, wih_ref, wil_ref, zr_ref, q_ref, k_ref, v_ref, qi_ref, kiw_ref):
    h = _rms(x_ref[...], g_ref[...])
    hh, hl = _split2(h)
    zm = _mm(hh, wm_ref[...])
    zr_ref[...] = zm[:, :RWKV_COLS]
    q_ref[...] = zm[:, RWKV_COLS:RWKV_COLS + WIDTH].astype(BF16)
    k_ref[...] = zm[:, RWKV_COLS + WIDTH:RWKV_COLS + 2 * WIDTH].astype(BF16)
    v_ref[...] = zm[:, RWKV_COLS + 2 * WIDTH:].astype(BF16)
    zi = _mm(hh, wih_ref[...]) + _mm(hh, wil_ref[...]) + _mm(hl, wih_ref[...])
    qh, ql = _split2(zi[:, :WIDTH])
    for h in range(IDX_HEADS):
        sl = slice(h * IDX_DIM, (h + 1) * IDX_DIM)
        qi_ref[h] = jnp.concatenate([qh[:, sl], qh[:, sl], ql[:, sl]], axis=1)
    kiw_ref[...] = zi[:, WIDTH:]


def _in_proj(x2, g, wm, wih, wil, tm=512):
    m, d = x2.shape
    row = lambda n: pl.BlockSpec((tm, n), lambda i: (i, 0))
    full = lambda a: pl.BlockSpec(a.shape, lambda i: (0, 0))
    return pl.pallas_call(
        _in_proj_kernel,
        grid=(m // tm,),
        in_specs=[row(d), full(g), full(wm), full(wih), full(wil)],
        out_specs=[row(RWKV_COLS), row(WIDTH), row(WIDTH), row(WIDTH),
                   pl.BlockSpec((IDX_HEADS, tm, 3 * IDX_DIM), lambda i: (0, i, 0)), row(LANES)],
        out_shape=[
            jax.ShapeDtypeStruct((m, RWKV_COLS), F32),
            jax.ShapeDtypeStruct((m, WIDTH), BF16),
            jax.ShapeDtypeStruct((m, WIDTH), BF16),
            jax.ShapeDtypeStruct((m, WIDTH), BF16),
            jax.ShapeDtypeStruct((IDX_HEADS, m, 3 * IDX_DIM), BF16),
            jax.ShapeDtypeStruct((m, LANES), F32),
        ],
        compiler_params=pltpu.CompilerParams(dimension_semantics=("arbitrary",), vmem_limit_bytes=VMEM_LIMIT),
        name="in_proj",
    )(x2, g, wm, wih, wil)


def _t5_bucket(rel):
    half = REL_BUCKETS // 2
    max_exact = half // 2
    ret = jnp.where(rel > 0, half, 0)
    n = jnp.abs(rel)
    nf = jnp.maximum(n, 1).astype(F32)
    steps = jnp.log(nf / max_exact) / math.log(REL_MAX_DISTANCE / max_exact) * (half - max_exact)
    large = max_exact
    for level in range(1, half - max_exact):
        large = large + jnp.where(steps >= level - 1e-4, 1, 0)
    return ret + jnp.where(n < max_exact, n, large)


def _bias_kernel(tab_ref, band_ref):
    r = lax.broadcasted_iota(I32, (CHUNK, KEY_TILE), 0)
    c = lax.broadcasted_iota(I32, (CHUNK, KEY_TILE), 1)
    far_bucket = _t5_bucket(jnp.full((CHUNK, KEY_TILE), -(FAR_GAP * CHUNK + 1), I32))
    for m in range(NEAR_VARIANTS):
        near_bucket = _t5_bucket(c - m * CHUNK - r)
        for h in range(HEADS):
            near = jnp.zeros((CHUNK, KEY_TILE), F32)
            far = jnp.zeros((CHUNK, KEY_TILE), F32)
            for j in range(REL_BUCKETS):
                t = tab_ref[j, h]
                near = jnp.where(near_bucket == j, t, near)
                far = jnp.where(far_bucket == j, t, far)
            band_ref[m, h * CHUNK:(h + 1) * CHUNK, :] = near - far


def _bias_tiles(rel_bias):
    return pl.pallas_call(
        _bias_kernel,
        in_specs=[pl.BlockSpec(memory_space=pltpu.SMEM)],
        out_shape=jax.ShapeDtypeStruct((NEAR_VARIANTS, HEADS * CHUNK, KEY_TILE), F32),
        name="bias_tiles",
    )(rel_bias)


def _seg_sum(x):
    rows = x.shape[0]
    li = lax.broadcasted_iota(I32, (LANES, LANES), 0) // HEAD_DIM
    lj = lax.broadcasted_iota(I32, (LANES, LANES), 1) // HEAD_DIM
    ones = (li == lj).astype(BF16)
    xs = jnp.concatenate([x[:, g * LANES:(g + 1) * LANES] for g in range(HEAD_PAIRS)], axis=0)
    y = _mm(xs.astype(BF16), ones)
    return jnp.concatenate([y[g * rows:(g + 1) * rows] for g in range(HEAD_PAIRS)], axis=1)


def _rwkv_kernel(z_ref, mu_ref, w0_ref, wl_ref, a0_ref, g2_ref, kk_ref, ka_ref, rk_ref, lng_ref, lnb_ref,
                 y_ref, prev_s, st_s, *, nb):
    @pl.when(pl.program_id(1) == 0)
    def _():
        prev_s[...] = jnp.zeros_like(prev_s)
        st_s[...] = jnp.zeros_like(st_s)

    gw = GROUP * HEAD_DIM
    lane_h = lax.broadcasted_iota(I32, (CHUNK, gw), 1) // HEAD_DIM
    src = lax.broadcasted_iota(I32, (CHUNK, gw), 1) & (HEAD_DIM - 1)
    tok = lax.broadcasted_iota(I32, (CHUNK, gw), 0)
    strict = tok > src
    incl = tok >= src
    eye = (tok == src).astype(F32)
    diag_blk = (lax.broadcasted_iota(I32, (gw, gw), 0) // HEAD_DIM) == (lax.broadcasted_iota(I32, (gw, gw), 1) // HEAD_DIM)

    def bdiag(x):
        xb = x.astype(BF16)
        zero = jnp.zeros_like(xb)
        return jnp.concatenate([jnp.where(lane_h == h, xb, zero) for h in range(GROUP)], axis=0)

    def prepare(rb0, nr):
        rows = nr * CHUNK
        z = z_ref[rb0:rb0 + nr].reshape(rows, RWKV_COLS)
        rolled = pltpu.roll(z, 1, axis=0)
        first = lax.broadcasted_iota(I32, (8, RWKV_COLS), 0) == 0
        pieces = []
        for j in range(nr):
            base = j * CHUNK
            pieces += [jnp.where(first, prev_s[rb0 + j], rolled[base:base + 8]), rolled[base + 8:base + CHUNK]]
            prev_s[rb0 + j] = z[base + CHUNK - 1:base + CHUNK, :]
        zp = jnp.concatenate(pieces, axis=0)
        zs = z + (zp - z) * mu_ref[...]
        yield
        r = zs[:, 0:WIDTH]
        k = zs[:, WIDTH:2 * WIDTH]
        v = zs[:, 2 * WIDTH:3 * WIDTH]
        zl = zs[:, 3 * WIDTH:3 * WIDTH + LANES]
        zg = zs[:, 3 * WIDTH + LANES:]
        lane = lax.broadcasted_iota(I32, zl.shape, 1)
        zl = jnp.where(lane < DECAY_RANK, jnp.tanh(zl), zl)
        la = _mm(zl.astype(BF16), wl_ref[...])
        ld = -math.exp(-0.5) * _sigmoid(w0_ref[...] + la[:, :WIDTH])
        yield
        a = _sigmoid(a0_ref[...] + la[:, WIDTH:])
        gate = _mm(_sigmoid(zg).astype(BF16), g2_ref[...])
        yield
        kk = k * kk_ref[...]
        kk = kk * jnp.minimum(lax.rsqrt(_seg_sum(kk * kk)), 1e12)
        yield
        kmod = k * (1.0 + (a - 1.0) * ka_ref[...])
        bvec = kk * a
        yield
        ti = lax.broadcasted_iota(I32, (rows, rows), 0)
        tj = lax.broadcasted_iota(I32, (rows, rows), 1)
        tri = ((ti >= tj) & (ti // CHUNK == tj // CHUNK)).astype(BF16)
        l1, l2 = _split2(ld)
        cum = _mm(tri, l1) + _mm(tri, l2)
        cum_x = cum - ld
        yield
        row_w = lax.broadcasted_iota(I32, (rows, WIDTH), 0)
        cref = cum[CHUNK // 2 - 1:CHUNK // 2, :]
        clast = cum[CHUNK - 1:CHUNK, :]
        for j in range(1, nr):
            mid = j * CHUNK + CHUNK // 2 - 1
            cref = jnp.where(row_w >= j * CHUNK, cum[mid:mid + 1, :], cref)
            clast = jnp.where(row_w >= j * CHUNK, cum[(j + 1) * CHUNK - 1:(j + 1) * CHUNK, :], clast)
        yield
        d = dict(r=r, v=v, kmod=kmod, gate=gate, cum=cum)
        e_neg = jnp.exp(cref - cum)
        d["rt"] = r * jnp.exp(cum - cref)
        yield
        d["at"] = -kk * jnp.exp(cum_x - cref)
        d["kt"] = kmod * e_neg
        yield
        d["bt"] = bvec * e_neg
        d["r0"] = r * jnp.exp(cum)
        yield
        d["a0v"] = -kk * jnp.exp(cum_x)
        e_end = jnp.exp(clast - cum)
        yield
        d["kc"] = kmod * e_end
        d["bc"] = bvec * e_end
        return d

    def chains(rb0, nr, d):
        units = [(j, g, slice(j * CHUNK, (j + 1) * CHUNK), slice(g * gw, (g + 1) * gw))
                 for j in range(nr) for g in range(HEADS // GROUP)]
        cut = lambda x: [x[rs, gs] for _, _, rs, gs in units]
        at_w, rt_w, bt_w, kt_w, v_w = cut(d["at"]), cut(d["rt"]), cut(d["bt"]), cut(d["kt"]), cut(d["v"])
        lhs = [jnp.concatenate([a_, r_], axis=0).astype(BF16) for a_, r_ in zip(at_w, rt_w)]
        ab_rb = [_nt(l_, bdiag(b_)) for l_, b_ in zip(lhs, bt_w)]
        yield
        ak_rk = [_nt(l_, bdiag(k_)) for l_, k_ in zip(lhs, kt_w)]
        low = [jnp.where(strict, x[:CHUNK], 0.0) for x in ab_rb]
        a_ak = [jnp.where(strict, x[:CHUNK], 0.0).astype(BF16) for x in ak_rk]
        a_rb = [jnp.where(incl, x[CHUNK:], 0.0).astype(BF16) for x in ab_rb]
        a_rk = [jnp.where(incl, x[CHUNK:], 0.0).astype(BF16) for x in ak_rk]
        yield
        v_bd = [bdiag(x) for x in v_w]
        s_f = [st_s[rb0 + j, g] for j, g, _, _ in units]
        s_b = [s.astype(BF16) for s in s_f]
        x0 = [_nt(x.astype(BF16), s) for x, s in zip(cut(d["a0v"]), s_b)]
        o_st = [_nt(x.astype(BF16), s) for x, s in zip(cut(d["r0"]), s_b)]
        yk = [_mm(a_, vb) for a_, vb in zip(a_ak, v_bd)]
        yield
        t = [eye + x for x in low]
        p = low
        p_bd = [bdiag(x) for x in p]
        for _ in range(int(math.log2(CHUNK)) - 1):
            p = [_mm(x.astype(BF16), xb) for x, xb in zip(p, p_bd)]
            p_bd = [bdiag(x) for x in p]
            yield
            t = [t_ + _mm(t_.astype(BF16), xb) for t_, xb in zip(t, p_bd)]
            yield
        u = [_mm(t_.astype(BF16), bdiag(a_ + b_)) for t_, a_, b_ in zip(t, x0, yk)]
        yield
        o_w = [a_ + _mm(rb, bdiag(u_)) + _mm(rk, vb) for a_, rb, u_, rk, vb in zip(o_st, a_rb, u, a_rk, v_bd)]
        uv = [jnp.concatenate([u_, v_], axis=0).astype(BF16) for u_, v_ in zip(u, v_w)]
        bk = [jnp.concatenate([b_, k_], axis=0).astype(BF16) for b_, k_ in zip(cut(d["bc"]), cut(d["kc"]))]
        yield
        for (j, gi, _, gs), s, w, x in zip(units, s_f, uv, bk):
            pc = jnp.exp(d["cum"][(j + 1) * CHUNK - 1:(j + 1) * CHUNK, gs])
            st_s[rb0 + j, gi] = s * pc + jnp.where(diag_blk, _tn(w, x), 0.0)
        groups = HEADS // GROUP
        return jnp.concatenate([jnp.concatenate(o_w[j * groups:(j + 1) * groups], axis=1) for j in range(nr)], axis=0)

    def finish(rb0, nr, d, o):
        inv_n = 1.0 / HEAD_DIM
        mean = _seg_sum(o) * inv_n
        dev = o - mean
        yield
        var = _seg_sum(dev * dev) * inv_n
        on = dev * lax.rsqrt(var + GN_EPS) * lng_ref[...] + lnb_ref[...]
        yield
        bonus = _seg_sum(d["r"] * d["kmod"] * rk_ref[...]) * d["v"]
        y_ref[rb0:rb0 + nr] = ((on + bonus) * d["gate"]).astype(BF16).reshape(nr, CHUNK, WIDTH)

    alternate = _alternate
    spans = [(rb0, min(ROW_GROUP, nb - rb0)) for rb0 in range(0, nb, ROW_GROUP)]
    data, = alternate(prepare(*spans[0]))
    done = None
    for n, span in enumerate(spans):
        stages = [chains(*span, data)]
        if n + 1 < len(spans):
            stages.append(prepare(*spans[n + 1]))
        if done is not None:
            stages.append(finish(*done))
        res = alternate(*stages)
        done = (*span, data, res[0])
        if n + 1 < len(spans):
            data = res[1]
    alternate(finish(*done))


def _rwkv(zr, mu, w0, wl, a0, g2, k_k, k_a, r_k, ln_g, ln_b, nb_max=8):
    b, s, _ = zr.shape
    nb = max(n for n in range(1, nb_max + 1) if b % n == 0)
    full = lambda a: pl.BlockSpec(a.shape, lambda i, j: (0,) * a.ndim)
    params = (mu, w0, wl, a0, g2, k_k, k_a, r_k, ln_g, ln_b)
    return pl.pallas_call(
        functools.partial(_rwkv_kernel, nb=nb),
        grid=(b // nb, s // CHUNK),
        in_specs=[pl.BlockSpec((nb, CHUNK, RWKV_COLS), lambda i, j: (i, j, 0))] + [full(a) for a in params],
        out_specs=pl.BlockSpec((nb, CHUNK, WIDTH), lambda i, j: (i, j, 0)),
        out_shape=jax.ShapeDtypeStruct((b, s, WIDTH), BF16),
        scratch_shapes=[
            pltpu.VMEM((nb, 1, RWKV_COLS), F32),
            pltpu.VMEM((nb, HEADS // GROUP, GROUP * HEAD_DIM, GROUP * HEAD_DIM), F32),
        ],
        compiler_params=pltpu.CompilerParams(dimension_semantics=("arbitrary", "arbitrary"),
                                             vmem_limit_bytes=VMEM_LIMIT),
        name="rwkv",
    )(zr, *params)


def _key_to_float(key):
    return lax.bitcast_convert_type(key ^ ((key >> 31) & INT_MAX), F32)


def _dsa_kernel(qi_ref, q_ref, kiw_ref, k_ref, v_ref, band_ref, lng_ref, lnb_ref, o_ref,
                ki3_s, wb_s, key_s, keyt_s, thr_s, jlim_s, lg_s, mx_s, mb_s, sm_s, pv_s, *, topk, nb):
    i = pl.program_id(1)
    sp = kiw_ref.shape[1]
    far_lim = jnp.clip(CHUNK * (i - FAR_GAP), 0, sp - KEY_TILE)
    n_far = (far_lim + KEY_TILE - 1) // KEY_TILE
    n_tiles = n_far + 1
    near_variant = i - far_lim // CHUNK
    lane1 = lax.broadcasted_iota(I32, (CHUNK, LANES), 1)
    halves = [slice(0, LANES), slice(LANES, KEY_TILE)]
    n_pairs = (nb + 1) // 2
    kf = float(topk)

    def tile_pos0(c):
        return jnp.where(c < n_far, c * KEY_TILE, far_lim)

    def tile_end(c):
        return jnp.where(c < n_far, far_lim, CHUNK * (i + 1))

    def tile_row0(c):
        return pl.multiple_of(tile_pos0(c), CHUNK)

    def half_pos(c, half):
        return tile_pos0(c) + half * LANES + lane1

    @pl.when(i == 0)
    def _():
        rows = KEY_TILE if sp % KEY_TILE == 0 else CHUNK

        def body(t, carry):
            bb = t // (sp // rows)
            r0 = pl.multiple_of((t % (sp // rows)) * rows, CHUNK)
            ki = kiw_ref[bb, pl.ds(r0, rows), :][:, :IDX_DIM]
            mean = jnp.mean(ki, axis=-1, keepdims=True)
            var = jnp.mean(jnp.square(ki - mean), axis=-1, keepdims=True)
            kn = (ki - mean) * lax.rsqrt(var + LN_EPS) * lng_ref[...] + lnb_ref[...]
            hi, lo = _split2(kn)
            ki3_s[bb, pl.ds(r0, rows), :] = jnp.concatenate([hi, lo, hi], axis=1)
            return carry

        lax.fori_loop(0, nb * (sp // rows), body, 0)

    def stage_queries(bb, carry):
        wq = kiw_ref[bb, pl.ds(pl.multiple_of(CHUNK * i, CHUNK), CHUNK), :]
        wscale = IDX_HEADS ** -0.5 * IDX_DIM ** -0.5
        wcol = jnp.concatenate([wq[:, IDX_DIM + h:IDX_DIM + h + 1] for h in range(IDX_HEADS)], axis=0) * wscale
        wb_s[bb] = jnp.broadcast_to(wcol, (IDX_HEADS * CHUNK, LANES))
        return carry

    lax.fori_loop(0, nb, stage_queries, 0)

    def score_tile(c, carry):
        r0 = tile_row0(c)
        end = tile_end(c)
        xs = [_nt(qi_ref[:, bb].reshape(IDX_HEADS * CHUNK, 3 * IDX_DIM), ki3_s[bb, pl.ds(r0, KEY_TILE), :])
              for bb in range(nb)]
        keys = []
        for bb in range(nb):
            wb = wb_s[bb]
            parts = []
            for half, hs_ in enumerate(halves):
                act = jnp.maximum(xs[bb][:, hs_], 0.0) * wb
                sc = act[0:CHUNK]
                for h in range(1, IDX_HEADS):
                    sc = sc + act[h * CHUNK:(h + 1) * CHUNK]
                pos = half_pos(c, half)
                valid = (pos >= 0) & (pos < end)
                parts.append(jnp.where(valid, sc, -jnp.inf))
            keys.append(jnp.concatenate(parts, axis=1))
            key_s[bb, c] = keys[bb]
        for pr in range(n_pairs):
            both = jnp.concatenate([keys[2 * pr], keys[min(2 * pr + 1, nb - 1)]], axis=0)
            keyt_s[pr, c] = both.T
        return carry

    lax.fori_loop(0, n_tiles, score_tile, 0)

    thr_s[...] = jnp.full(thr_s.shape, -jnp.inf, F32)
    jlim_s[...] = jnp.full(jlim_s.shape, INT_MAX, I32)

    part = 32

    def count(cands, above):
        wide = [jnp.broadcast_to(cd, (part, LANES)) for cd in cands]

        def body(c, accs):
            out = []
            for pr in range(n_pairs):
                acc = accs[pr]
                for k in range(KEY_TILE // part):
                    sc = keyt_s[pr, c, k * part:(k + 1) * part, :]
                    acc = jnp.where(above(sc, wide[pr]), acc + 1.0, acc)
                out.append(acc)
            return tuple(out)

        accs = lax.fori_loop(0, n_tiles, body, tuple(jnp.zeros((part, LANES), F32) for _ in range(n_pairs)))
        return [jnp.sum(a, axis=0, keepdims=True) for a in accs]

    def count_ge(keys):
        return count([_key_to_float(k) for k in keys], lambda s, c: s >= c)

    @pl.when(CHUNK * (i + 1) > topk)
    def _():
        n_all = float(KEY_TILE) * n_tiles.astype(F32)
        n_pos = count_ge([jnp.zeros((1, LANES), I32)] * n_pairs)
        ans0 = tuple(jnp.where(n >= kf, 0, INT_MIN).astype(I32) for n in n_pos)
        cnt0 = tuple(jnp.where(n >= kf, n, n_all) for n in n_pos)

        def one_bit(t, carry):
            ans, n_ans = carry
            bit = lax.shift_left(jnp.int32(1), 30 - t)
            cands = [a | bit for a in ans]
            cnts = count_ge(cands)
            take = [n >= kf for n in cnts]
            return (tuple(jnp.where(tk, cd, a) for tk, cd, a in zip(take, cands, ans)),
                    tuple(jnp.where(tk, n, na) for tk, n, na in zip(take, cnts, n_ans)))

        ans, n_ans = lax.fori_loop(0, 31, one_bit, (ans0, cnt0))
        def to_columns(rows_, dst):
            for pr in range(n_pairs):
                col = jnp.broadcast_to(rows_[pr], (LANES, LANES)).T
                dst[2 * pr] = col[:CHUNK]
                if 2 * pr + 1 < nb:
                    dst[2 * pr + 1] = col[CHUNK:]

        kth = [_key_to_float(a) for a in ans]
        to_columns(kth, thr_s)
        most = n_ans[0]
        for pr in range(1, n_pairs):
            most = jnp.maximum(most, n_ans[pr])

        @pl.when(jnp.max(most) > kf)
        def _():
            n_gt = count(kth, lambda s, c: s > c)
            need = [kf - n for n in n_gt]
            ans_w = [jnp.broadcast_to(a, (KEY_TILE, LANES)) for a in kth]
            krow = lax.broadcasted_iota(I32, (KEY_TILE, LANES), 0)
            tri = (lax.broadcasted_iota(I32, (KEY_TILE, KEY_TILE), 0)
                   >= lax.broadcasted_iota(I32, (KEY_TILE, KEY_TILE), 1)).astype(BF16)

            def body(c, carry):
                seen, last = carry
                pos = tile_pos0(c) + krow
                seen_out, last_out = [], []
                for pr in range(n_pairs):
                    tie = keyt_s[pr, c] == ans_w[pr]
                    rank = _mm(tri, jnp.where(tie, 1.0, 0.0).astype(BF16)) + seen[pr]
                    kept = jnp.where(tie & (rank <= need[pr]), pos, -1)
                    last_out.append(jnp.maximum(last[pr], jnp.max(kept, axis=0, keepdims=True)))
                    seen_out.append(rank[KEY_TILE - 1:KEY_TILE, :])
                return tuple(seen_out), tuple(last_out)

            init = (tuple(jnp.zeros((1, LANES), F32) for _ in range(n_pairs)),
                    tuple(jnp.full((1, LANES), -1, I32) for _ in range(n_pairs)))
            _, last = lax.fori_loop(0, n_tiles, body, init)
            to_columns(last, jlim_s)

    pair_lane_head = lane1 // HEAD_DIM

    cw = lg_s.shape[0]

    def phase_c(step, carry):
        bbs = [step * cw + j for j in range(cw)]
        zero = jnp.zeros((CHUNK, LANES), BF16)
        qm = []
        for bb in bbs:
            qs = q_ref[bb] * jnp.asarray(HEAD_DIM ** -0.5, BF16)
            qm.append([jnp.concatenate([jnp.where(pair_lane_head == 0, qs[:, g * LANES:(g + 1) * LANES], zero),
                                        jnp.where(pair_lane_head == 1, qs[:, g * LANES:(g + 1) * LANES], zero)],
                                       axis=0) for g in range(HEAD_PAIRS)])
        thr = [thr_s[bb] for bb in bbs]
        jlim = [jlim_s[bb] for bb in bbs]
        mx_s[...] = jnp.full(mx_s.shape, NEG_INF, F32)

        def logits(c, bias):
            r0 = tile_row0(c)
            end = tile_end(c)
            kt = [k_ref[bb, pl.ds(r0, KEY_TILE), :] for bb in bbs]
            s = [[_nt(qm[j][g], kt[j][:, g * LANES:(g + 1) * LANES]) for g in range(HEAD_PAIRS)]
                 for j in range(cw)]
            for j, bb in enumerate(bbs):
                sel = []
                for half, hs_ in enumerate(halves):
                    key = key_s[bb, c, :, hs_]
                    pos = half_pos(c, half)
                    sel.append((pos >= 0) & (pos < end) & ((key > thr[j]) | ((key == thr[j]) & (pos <= jlim[j]))))
                for g in range(HEAD_PAIRS):
                    sg = s[j][g]
                    if bias is not None:
                        sg = sg + bias[g * 2 * CHUNK:(g + 1) * 2 * CHUNK]
                    for hh in range(2):
                        rows = slice((2 * g + hh) * CHUNK, (2 * g + hh + 1) * CHUNK)
                        sub = sg[hh * CHUNK:(hh + 1) * CHUNK]
                        m0 = jnp.where(sel[0], sub[:, halves[0]], NEG_INF)
                        m1 = jnp.where(sel[1], sub[:, halves[1]], NEG_INF)
                        lg_s[j, c, rows, halves[0]] = m0
                        lg_s[j, c, rows, halves[1]] = m1
                        mx_s[j, rows] = jnp.maximum(mx_s[j, rows], jnp.maximum(m0, m1))

        def far_logits(c, carry):
            logits(c, None)
            return carry

        lax.fori_loop(0, n_far, far_logits, 0)
        logits(n_far, band_ref[near_variant])

        for j in range(cw):
            mb_s[j] = jnp.broadcast_to(jnp.max(mx_s[j], axis=1, keepdims=True), mb_s.shape[1:])
        sm_s[...] = jnp.zeros_like(sm_s)
        pv_s[...] = jnp.zeros_like(pv_s)

        ones = jnp.ones((KEY_TILE, LANES), BF16)

        def weigh(c, carry):
            r0 = tile_row0(c)
            vt = [v_ref[bb, pl.ds(r0, KEY_TILE), :] for bb in bbs]
            for g in range(HEAD_PAIRS):
                rows = slice(g * 2 * CHUNK, (g + 1) * 2 * CHUNK)
                ps = []
                for j in range(cw):
                    m = mb_s[j, rows]
                    p0 = jnp.exp(lg_s[j, c, rows, halves[0]] - m)
                    p1 = jnp.exp(lg_s[j, c, rows, halves[1]] - m)
                    ps.append(jnp.concatenate([p0, p1], axis=1).astype(BF16))
                for j in range(cw):
                    rhs = jnp.concatenate([vt[j][:, g * LANES:(g + 1) * LANES], ones], axis=1)
                    both = _mm(ps[j], rhs)
                    pv_s[j, rows] = pv_s[j, rows] + both[:, :LANES]
                    sm_s[j, rows] = sm_s[j, rows] + both[:, LANES:]
            return carry

        lax.fori_loop(0, n_tiles, weigh, 0)

        for j, bb in enumerate(bbs):
            outn = pv_s[j] / sm_s[j]
            cols = []
            for g in range(HEAD_PAIRS):
                top = outn[(2 * g) * CHUNK:(2 * g + 1) * CHUNK]
                bot = outn[(2 * g + 1) * CHUNK:(2 * g + 2) * CHUNK]
                cols.append(jnp.where(pair_lane_head == 0, top, bot))
            o_ref[bb] = jnp.concatenate(cols, axis=1).astype(BF16)
        return carry

    lax.fori_loop(0, nb // cw, phase_c, 0)


def _dsa(qi_t, q, kiw_p, k_p, v_p, band, ln_g, ln_b):
    b, s, _ = q.shape
    sp = k_p.shape[1]
    topk = min(MAX_TOPK, s // 4)
    assert s % KEY_TILE == 0, "the near window needs at least one whole key tile"
    max_tiles = s // KEY_TILE
    nb = 4 if b % 4 == 0 else (2 if b % 2 == 0 else 1)
    cw = nb
    rows = HEADS * CHUNK
    full = lambda a: pl.BlockSpec(a.shape, lambda i, j: (0,) * a.ndim)
    per_group = lambda n: pl.BlockSpec((nb, sp, n), lambda i, j: (i, 0, 0), pipeline_mode=pl.Buffered(1))
    return pl.pallas_call(
        functools.partial(_dsa_kernel, topk=topk, nb=nb),
        grid=(b // nb, s // CHUNK),
        in_specs=[
            pl.BlockSpec((IDX_HEADS, nb, CHUNK, 3 * IDX_DIM), lambda i, j: (0, i, j, 0)),
            pl.BlockSpec((nb, CHUNK, WIDTH), lambda i, j: (i, j, 0)),
            per_group(LANES), per_group(WIDTH), per_group(WIDTH),
            full(band), full(ln_g), full(ln_b),
        ],
        out_specs=pl.BlockSpec((nb, CHUNK, WIDTH), lambda i, j: (i, j, 0)),
        out_shape=jax.ShapeDtypeStruct((b, s, WIDTH), BF16),
        scratch_shapes=[
            pltpu.VMEM((nb, sp, 3 * IDX_DIM), BF16),
            pltpu.VMEM((nb, rows, LANES), F32),
            pltpu.VMEM((nb, max_tiles, CHUNK, KEY_TILE), F32),
            pltpu.VMEM(((nb + 1) // 2, max_tiles, KEY_TILE, 2 * CHUNK), F32),
            pltpu.VMEM((nb, CHUNK, LANES), F32),
            pltpu.VMEM((nb, CHUNK, LANES), I32),
            pltpu.VMEM((cw, max_tiles, rows, KEY_TILE), F32),
            pltpu.VMEM((cw, rows, LANES), F32),
            pltpu.VMEM((cw, rows, LANES), F32),
            pltpu.VMEM((cw, rows, LANES), F32),
            pltpu.VMEM((cw, rows, LANES), F32),
        ],
        compiler_params=pltpu.CompilerParams(dimension_semantics=("arbitrary", "arbitrary"),
                                             vmem_limit_bytes=VMEM_LIMIT),
        name="dsa",
    )(qi_t, q, kiw_p, k_p, v_p, band, ln_g, ln_b)


def _post_kernel(y1_ref, y2_ref, x_ref, p_ref, woa_ref, wob_ref, w1_ref, w2_ref, wp_ref, wg_ref,
                 g_mix_ref, g_f1_ref, g_f2_ref, g_p1_ref, g_p2_ref, o_ref):
    mix = _mm(y1_ref[...], woa_ref[...]) + _mm(y2_ref[...], wob_ref[...])
    x1 = x_ref[...] + _rms(mix, g_mix_ref[...])
    h = _rms(x1, g_f1_ref[...]).astype(BF16)
    hid = jnp.square(jnp.maximum(_mm(h, w1_ref[...]), 0.0)).astype(BF16)
    x2 = x1 + _rms(_mm(hid, w2_ref[...]), g_f2_ref[...])
    gate = _sigmoid(_mm(_rms(x2, g_p1_ref[...]).astype(BF16), wg_ref[...]))
    pe = _mm(p_ref[...].astype(BF16), wp_ref[...]) * gate
    o_ref[...] = x2 + _rms(pe, g_p2_ref[...])


def _post(y1, y2, x2d, p2d, woa, wob, w1, w2, wp, wg, gains, tm=512):
    m, d = x2d.shape
    row = lambda n: pl.BlockSpec((tm, n), lambda i: (i, 0))
    const = lambda a: pl.BlockSpec(a.shape, lambda i: (0, 0), pipeline_mode=pl.Buffered(1))
    weights = (woa, wob, w1, w2, wp, wg) + tuple(gains)
    return pl.pallas_call(
        _post_kernel,
        grid=(m // tm,),
        in_specs=[row(WIDTH), row(WIDTH), row(d), row(p2d.shape[1])] + [const(a) for a in weights],
        out_specs=row(d),
        out_shape=jax.ShapeDtypeStruct((m, d), F32),
        compiler_params=pltpu.CompilerParams(dimension_semantics=("arbitrary",), vmem_limit_bytes=VMEM_LIMIT),
        name="post",
    )(y1, y2, x2d, p2d, *weights)


def _layer(x, p, w_in, shift_mu, w0, w2r, a0, a2, g2, k_k, k_a, r_k, ln_g, ln_b, idx_g, idx_b, rel_bias,
           w_out, w_f1, w_f2, w_ple, w_gate, n_mix_pre, n_mix_post, n_ffn_pre, n_ffn_post, n_ple_pre, n_ple_post):
    b, s, d = x.shape
    m = b * s
    row = lambda a: a.reshape(1, -1).astype(F32)

    c = [0, WIDTH, WIDTH + DECAY_RANK, 2 * WIDTH + DECAY_RANK, 3 * WIDTH + DECAY_RANK,
         3 * WIDTH + DECAY_RANK + A_RANK, RWKV_COLS]
    order = [(c[0], c[1]), (c[2], c[3]), (c[3], c[4]), (c[1], c[2]), (c[4], c[5]), (c[5], c[6])]
    reorder = lambda a: jnp.concatenate([a[..., lo:hi] for lo, hi in order], axis=-1)
    w_main = jnp.concatenate([reorder(w_in[:, :RWKV_COLS]), w_in[:, RWKV_COLS:RWKV_COLS + 3 * WIDTH]],
                             axis=1).astype(BF16)
    w_idx = w_in[:, RWKV_COLS + 3 * WIDTH:]
    w_idx = jnp.pad(w_idx, ((0, 0), (0, N_IDX - w_idx.shape[1])))
    w_idx_hi = w_idx.astype(BF16)
    w_idx_lo = (w_idx - w_idx_hi.astype(F32)).astype(BF16)

    zr, q, k, v, qi, kiw = _in_proj(x.reshape(m, d), row(n_mix_pre), w_main, w_idx_hi, w_idx_lo)

    zeros = jnp.zeros((DECAY_RANK, WIDTH), F32)
    w_lora = jnp.concatenate([jnp.concatenate([w2r, zeros], axis=1),
                              jnp.concatenate([zeros, a2], axis=1)], axis=0).astype(BF16)
    y_rwkv = _rwkv(zr.reshape(b, s, RWKV_COLS), row(reorder(shift_mu)), row(w0), w_lora, row(a0), g2.astype(BF16),
                   row(k_k), row(k_a), row(r_k), row(ln_g), row(ln_b))

    band = _bias_tiles(rel_bias.astype(F32))
    per_row = lambda a: a.reshape(b, s, -1)
    y_dsa = _dsa(qi.reshape(IDX_HEADS, b, s, 3 * IDX_DIM), per_row(q), per_row(kiw), per_row(k), per_row(v), band,
                 row(idx_g), row(idx_b))

    gains = tuple(row(a) for a in (n_mix_post, n_ffn_pre, n_ffn_post, n_ple_pre, n_ple_post))
    out = _post(y_rwkv.reshape(m, WIDTH), y_dsa.reshape(m, WIDTH), x.reshape(m, d), p.reshape(m, -1),
                w_out[:WIDTH].astype(BF16), w_out[WIDTH:].astype(BF16), w_f1.astype(BF16), w_f2.astype(BF16),
                w_ple.astype(BF16), w_gate.astype(BF16), gains)
    return out.reshape(b, s, d)


def kernel(x, p, w_in, shift_mu, rwkv_w0, rwkv_w2, rwkv_a0, rwkv_a2, rwkv_g2, rwkv_k_k, rwkv_k_a, rwkv_r_k, rwkv_ln_g, rwkv_ln_b, idx_k_ln_g, idx_k_ln_b, rel_bias, w_out, w_ffn1, w_ffn2, w_ple, w_ple_gate, norm_mix_pre, norm_mix_post, norm_ffn_pre, norm_ffn_post, norm_ple_pre, norm_ple_post):
    for i in range(p.shape[0]):
        x = _layer(x, p[i], w_in[i], shift_mu[i], rwkv_w0[i], rwkv_w2[i], rwkv_a0[i], rwkv_a2[i], rwkv_g2[i],
                   rwkv_k_k[i], rwkv_k_a[i], rwkv_r_k[i], rwkv_ln_g[i], rwkv_ln_b[i], idx_k_ln_g[i],
                   idx_k_ln_b[i], rel_bias, w_out[i], w_ffn1[i], w_ffn2[i], w_ple[i], w_ple_gate[i],
                   norm_mix_pre[i], norm_mix_post[i], norm_ffn_pre[i], norm_ffn_post[i], norm_ple_pre[i],
                   norm_ple_post[i])
    return x
```

```python
import functools
import math

import jax
import jax.numpy as jnp
from jax import lax
from jax.experimental import pallas as pl
from jax.experimental.pallas import tpu as pltpu

F32 = jnp.float32
BF16 = jnp.bfloat16
I32 = jnp.int32

CHUNK = 64
HEADS = 8
HEAD_DIM = 64
WIDTH = HEADS * HEAD_DIM
DECAY_RANK = 64
A_RANK = 64
GATE_RANK = 128
GN_EPS = 64e-5
IDX_HEADS = 8
IDX_DIM = 64
MAX_TOPK = 256
REL_BUCKETS = 32
REL_MAX_DISTANCE = 128
NORM_EPS = 1e-6
LN_EPS = 1e-6
NEG_INF = -1e30
RWKV_COLS = 1792
N_MAIN = RWKV_COLS + 3 * WIDTH
N_IDX = 640

LANES = 128
KEY_TILE = 256
HEAD_PAIRS = WIDTH // LANES
GROUP = 4
ROW_GROUP = 4
FAR_GAP = 2
NEAR_VARIANTS = KEY_TILE // CHUNK
VMEM_LIMIT = 62 * 1024 * 1024

INT_MIN = -(2**31)
INT_MAX = 2**31 - 1


def _nt(a, b):
    return lax.dot_general(a, b, (((1,), (1,)), ((), ())), preferred_element_type=F32)


def _tn(a, b):
    return lax.dot_general(a, b, (((0,), (0,)), ((), ())), preferred_element_type=F32)


def _mm(a, b):
    return jnp.dot(a, b, preferred_element_type=F32)


def _split2(x):
    hi = x.astype(BF16)
    lo = (x - hi.astype(F32)).astype(BF16)
    return hi, lo


def _alternate(*gens):
    out = [None] * len(gens)
    live = list(range(len(gens)))
    while live:
        for n in list(live):
            try:
                next(gens[n])
            except StopIteration as stop:
                out[n] = stop.value
                live.remove(n)
    return out


def _rms(x, g):
    return x * lax.rsqrt(jnp.mean(x * x, axis=-1, keepdims=True) + NORM_EPS) * g


def _sigmoid(x):
    return 1.0 / (1.0 + jnp.exp(-x))


def _in_proj_kernel(x_ref, g_ref, wm_ref, wih_ref, wil_ref, zr_ref, q_ref, k_ref, v_ref, qi_ref, kiw_ref):
    h = _rms(x_ref[...], g_ref[...])
    hh, hl = _split2(h)
    zm = _mm(hh, wm_ref[...])
    zr_ref[...] = zm[:, :RWKV_COLS]
    q_ref[...] = zm[:, RWKV_COLS:RWKV_COLS + WIDTH].astype(BF16)
    k_ref[...] = zm[:, RWKV_COLS + WIDTH:RWKV_COLS + 2 * WIDTH].astype(BF16)
    v_ref[...] = zm[:, RWKV_COLS + 2 * WIDTH:].astype(BF16)
    zi = _mm(hh, wih_ref[...]) + _mm(hh, wil_ref[...]) + _mm(hl, wih_ref[...])
    qh, ql = _split2(zi[:, :WIDTH])
    for h in range(IDX_HEADS):
        sl = slice(h * IDX_DIM, (h + 1) * IDX_DIM)
        qi_ref[h] = jnp.concatenate([qh[:, sl], qh[:, sl], ql[:, sl]], axis=1)
    kiw_ref[...] = zi[:, WIDTH:]


def _in_proj(x2, g, wm, wih, wil, tm=1024):
    m, d = x2.shape
    row = lambda n: pl.BlockSpec((tm, n), lambda i: (i, 0))
    full = lambda a: pl.BlockSpec(a.shape, lambda i: (0, 0))
    return pl.pallas_call(
        _in_proj_kernel,
        grid=(m // tm,),
        in_specs=[row(d), full(g), full(wm), full(wih), full(wil)],
        out_specs=[row(RWKV_COLS), row(WIDTH), row(WIDTH), row(WIDTH),
                   pl.BlockSpec((IDX_HEADS, tm, 3 * IDX_DIM), lambda i: (0, i, 0)), row(LANES)],
        out_shape=[
            jax.ShapeDtypeStruct((m, RWKV_COLS), F32),
            jax.ShapeDtypeStruct((m, WIDTH), BF16),
            jax.ShapeDtypeStruct((m, WIDTH), BF16),
            jax.ShapeDtypeStruct((m, WIDTH), BF16),
            jax.ShapeDtypeStruct((IDX_HEADS, m, 3 * IDX_DIM), BF16),
            jax.ShapeDtypeStruct((m, LANES), F32),
        ],
        compiler_params=pltpu.CompilerParams(dimension_semantics=("arbitrary",), vmem_limit_bytes=VMEM_LIMIT),
        name="in_proj",
    )(x2, g, wm, wih, wil)


def _t5_bucket(rel):
    half = REL_BUCKETS // 2
    max_exact = half // 2
    ret = jnp.where(rel > 0, half, 0)
    n = jnp.abs(rel)
    nf = jnp.maximum(n, 1).astype(F32)
    steps = jnp.log(nf / max_exact) / math.log(REL_MAX_DISTANCE / max_exact) * (half - max_exact)
    large = max_exact
    for level in range(1, half - max_exact):
        large = large + jnp.where(steps >= level - 1e-4, 1, 0)
    return ret + jnp.where(n < max_exact, n, large)


def _bias_kernel(tab_ref, band_ref):
    r = lax.broadcasted_iota(I32, (CHUNK, KEY_TILE), 0)
    c = lax.broadcasted_iota(I32, (CHUNK, KEY_TILE), 1)
    far_bucket = _t5_bucket(jnp.full((CHUNK, KEY_TILE), -(FAR_GAP * CHUNK + 1), I32))
    for m in range(NEAR_VARIANTS):
        near_bucket = _t5_bucket(c - m * CHUNK - r)
        for h in range(HEADS):
            near = jnp.zeros((CHUNK, KEY_TILE), F32)
            far = jnp.zeros((CHUNK, KEY_TILE), F32)
            for j in range(REL_BUCKETS):
                t = tab_ref[j, h]
                near = jnp.where(near_bucket == j, t, near)
                far = jnp.where(far_bucket == j, t, far)
            band_ref[m, h * CHUNK:(h + 1) * CHUNK, :] = near - far


def _bias_tiles(rel_bias):
    return pl.pallas_call(
        _bias_kernel,
        in_specs=[pl.BlockSpec(memory_space=pltpu.SMEM)],
        out_shape=jax.ShapeDtypeStruct((NEAR_VARIANTS, HEADS * CHUNK, KEY_TILE), F32),
        name="bias_tiles",
    )(rel_bias)


def _seg_sum(x):
    rows = x.shape[0]
    li = lax.broadcasted_iota(I32, (LANES, LANES), 0) // HEAD_DIM
    lj = lax.broadcasted_iota(I32, (LANES, LANES), 1) // HEAD_DIM
    ones = (li == lj).astype(BF16)
    xs = jnp.concatenate([x[:, g * LANES:(g + 1) * LANES] for g in range(HEAD_PAIRS)], axis=0)
    y = _mm(xs.astype(BF16), ones)
    return jnp.concatenate([y[g * rows:(g + 1) * rows] for g in range(HEAD_PAIRS)], axis=1)


def _rwkv_kernel(z_ref, mu_ref, w0_ref, wl_ref, a0_ref, g2_ref, kk_ref, ka_ref, rk_ref, lng_ref, lnb_ref,
                 y_ref, prev_s, st_s, *, nb):
    @pl.when(pl.program_id(1) == 0)
    def _():
        prev_s[...] = jnp.zeros_like(prev_s)
        st_s[...] = jnp.zeros_like(st_s)

    gw = GROUP * HEAD_DIM
    lane_h = lax.broadcasted_iota(I32, (CHUNK, gw), 1) // HEAD_DIM
    src = lax.broadcasted_iota(I32, (CHUNK, gw), 1) & (HEAD_DIM - 1)
    tok = lax.broadcasted_iota(I32, (CHUNK, gw), 0)
    strict = tok > src
    incl = tok >= src
    eye = (tok == src).astype(F32)
    diag_blk = (lax.broadcasted_iota(I32, (gw, gw), 0) // HEAD_DIM) == (lax.broadcasted_iota(I32, (gw, gw), 1) // HEAD_DIM)

    def bdiag(x):
        xb = x.astype(BF16)
        zero = jnp.zeros_like(xb)
        return jnp.concatenate([jnp.where(lane_h == h, xb, zero) for h in range(GROUP)], axis=0)

    def prepare(rb0, nr):
        rows = nr * CHUNK
        z = z_ref[rb0:rb0 + nr].reshape(rows, RWKV_COLS)
        rolled = pltpu.roll(z, 1, axis=0)
        first = lax.broadcasted_iota(I32, (8, RWKV_COLS), 0) == 0
        pieces = []
        for j in range(nr):
            base = j * CHUNK
            pieces += [jnp.where(first, prev_s[rb0 + j], rolled[base:base + 8]), rolled[base + 8:base + CHUNK]]
            prev_s[rb0 + j] = z[base + CHUNK - 1:base + CHUNK, :]
        zp = jnp.concatenate(pieces, axis=0)
        zs = z + (zp - z) * mu_ref[...]
        yield
        r = zs[:, 0:WIDTH]
        k = zs[:, WIDTH:2 * WIDTH]
        v = zs[:, 2 * WIDTH:3 * WIDTH]
        zl = zs[:, 3 * WIDTH:3 * WIDTH + LANES]
        zg = zs[:, 3 * WIDTH + LANES:]
        lane = lax.broadcasted_iota(I32, zl.shape, 1)
        zl = jnp.where(lane < DECAY_RANK, jnp.tanh(zl), zl)
        la = _mm(zl.astype(BF16), wl_ref[...])
        ld = -math.exp(-0.5) * _sigmoid(w0_ref[...] + la[:, :WIDTH])
        yield
        a = _sigmoid(a0_ref[...] + la[:, WIDTH:])
        gate = _mm(_sigmoid(zg).astype(BF16), g2_ref[...])
        yield
        kk = k * kk_ref[...]
        kk = kk * jnp.minimum(lax.rsqrt(_seg_sum(kk * kk)), 1e12)
        yield
        kmod = k * (1.0 + (a - 1.0) * ka_ref[...])
        bvec = kk * a
        yield
        ti = lax.broadcasted_iota(I32, (rows, rows), 0)
        tj = lax.broadcasted_iota(I32, (rows, rows), 1)
        tri = ((ti >= tj) & (ti // CHUNK == tj // CHUNK)).astype(BF16)
        l1, l2 = _split2(ld)
        cum = _mm(tri, l1) + _mm(tri, l2)
        cum_x = cum - ld
        yield
        row_w = lax.broadcasted_iota(I32, (rows, WIDTH), 0)
        cref = cum[CHUNK // 2 - 1:CHUNK // 2, :]
        clast = cum[CHUNK - 1:CHUNK, :]
        for j in range(1, nr):
            mid = j * CHUNK + CHUNK // 2 - 1
            cref = jnp.where(row_w >= j * CHUNK, cum[mid:mid + 1, :], cref)
            clast = jnp.where(row_w >= j * CHUNK, cum[(j + 1) * CHUNK - 1:(j + 1) * CHUNK, :], clast)
        yield
        d = dict(r=r, v=v, kmod=kmod, gate=gate, cum=cum)
        e_neg = jnp.exp(cref - cum)
        d["rt"] = r * jnp.exp(cum - cref)
        yield
        d["at"] = -kk * jnp.exp(cum_x - cref)
        d["kt"] = kmod * e_neg
        yield
        d["bt"] = bvec * e_neg
        d["r0"] = r * jnp.exp(cum)
        yield
        d["a0v"] = -kk * jnp.exp(cum_x)
        e_end = jnp.exp(clast - cum)
        yield
        d["kc"] = kmod * e_end
        d["bc"] = bvec * e_end
        return d

    def chains(rb0, nr, d):
        units = [(j, g, slice(j * CHUNK, (j + 1) * CHUNK), slice(g * gw, (g + 1) * gw))
                 for j in range(nr) for g in range(HEADS // GROUP)]
        cut = lambda x: [x[rs, gs] for _, _, rs, gs in units]
        at_w, rt_w, bt_w, kt_w, v_w = cut(d["at"]), cut(d["rt"]), cut(d["bt"]), cut(d["kt"]), cut(d["v"])
        lhs = [jnp.concatenate([a_, r_], axis=0).astype(BF16) for a_, r_ in zip(at_w, rt_w)]
        ab_rb = [_nt(l_, bdiag(b_)) for l_, b_ in zip(lhs, bt_w)]
        yield
        ak_rk = [_nt(l_, bdiag(k_)) for l_, k_ in zip(lhs, kt_w)]
        low = [jnp.where(strict, x[:CHUNK], 0.0) for x in ab_rb]
        a_ak = [jnp.where(strict, x[:CHUNK], 0.0).astype(BF16) for x in ak_rk]
        a_rb = [jnp.where(incl, x[CHUNK:], 0.0).astype(BF16) for x in ab_rb]
        a_rk = [jnp.where(incl, x[CHUNK:], 0.0).astype(BF16) for x in ak_rk]
        yield
        v_bd = [bdiag(x) for x in v_w]
        s_f = [st_s[rb0 + j, g] for j, g, _, _ in units]
        s_b = [s.astype(BF16) for s in s_f]
        x0 = [_nt(x.astype(BF16), s) for x, s in zip(cut(d["a0v"]), s_b)]
        o_st = [_nt(x.astype(BF16), s) for x, s in zip(cut(d["r0"]), s_b)]
        yk = [_mm(a_, vb) for a_, vb in zip(a_ak, v_bd)]
        yield
        t = [eye + x for x in low]
        p = low
        p_bd = [bdiag(x) for x in p]
        for _ in range(int(math.log2(CHUNK)) - 1):
            p = [_mm(x.astype(BF16), xb) for x, xb in zip(p, p_bd)]
            p_bd = [bdiag(x) for x in p]
            yield
            t = [t_ + _mm(t_.astype(BF16), xb) for t_, xb in zip(t, p_bd)]
            yield
        u = [_mm(t_.astype(BF16), bdiag(a_ + b_)) for t_, a_, b_ in zip(t, x0, yk)]
        yield
        o_w = [a_ + _mm(rb, bdiag(u_)) + _mm(rk, vb) for a_, rb, u_, rk, vb in zip(o_st, a_rb, u, a_rk, v_bd)]
        uv = [jnp.concatenate([u_, v_], axis=0).astype(BF16) for u_, v_ in zip(u, v_w)]
        bk = [jnp.concatenate([b_, k_], axis=0).astype(BF16) for b_, k_ in zip(cut(d["bc"]), cut(d["kc"]))]
        yield
        for (j, gi, _, gs), s, w, x in zip(units, s_f, uv, bk):
            pc = jnp.exp(d["cum"][(j + 1) * CHUNK - 1:(j + 1) * CHUNK, gs])
            st_s[rb0 + j, gi] = s * pc + jnp.where(diag_blk, _tn(w, x), 0.0)
        groups = HEADS // GROUP
        return jnp.concatenate([jnp.concatenate(o_w[j * groups:(j + 1) * groups], axis=1) for j in range(nr)], axis=0)

    def finish(rb0, nr, d, o):
        inv_n = 1.0 / HEAD_DIM
        mean = _seg_sum(o) * inv_n
        dev = o - mean
        yield
        var = _seg_sum(dev * dev) * inv_n
        on = dev * lax.rsqrt(var + GN_EPS) * lng_ref[...] + lnb_ref[...]
        yield
        bonus = _seg_sum(d["r"] * d["kmod"] * rk_ref[...]) * d["v"]
        y_ref[rb0:rb0 + nr] = ((on + bonus) * d["gate"]).astype(BF16).reshape(nr, CHUNK, WIDTH)

    alternate = _alternate
    spans = [(rb0, min(ROW_GROUP, nb - rb0)) for rb0 in range(0, nb, ROW_GROUP)]
    data, = alternate(prepare(*spans[0]))
    done = None
    for n, span in enumerate(spans):
        stages = [chains(*span, data)]
        if n + 1 < len(spans):
            stages.append(prepare(*spans[n + 1]))
        if done is not None:
            stages.append(finish(*done))
        res = alternate(*stages)
        done = (*span, data, res[0])
        if n + 1 < len(spans):
            data = res[1]
    alternate(finish(*done))


def _rwkv(zr, mu, w0, wl, a0, g2, k_k, k_a, r_k, ln_g, ln_b, nb_max=8):
    b, s, _ = zr.shape
    nb = max(n for n in range(1, nb_max + 1) if b % n == 0)
    full = lambda a: pl.BlockSpec(a.shape, lambda i, j: (0,) * a.ndim)
    params = (mu, w0, wl, a0, g2, k_k, k_a, r_k, ln_g, ln_b)
    return pl.pallas_call(
        functools.partial(_rwkv_kernel, nb=nb),
        grid=(b // nb, s // CHUNK),
        in_specs=[pl.BlockSpec((nb, CHUNK, RWKV_COLS), lambda i, j: (i, j, 0))] + [full(a) for a in params],
        out_specs=pl.BlockSpec((nb, CHUNK, WIDTH), lambda i, j: (i, j, 0)),
        out_shape=jax.ShapeDtypeStruct((b, s, WIDTH), BF16),
        scratch_shapes=[
            pltpu.VMEM((nb, 1, RWKV_COLS), F32),
            pltpu.VMEM((nb, HEADS // GROUP, GROUP * HEAD_DIM, GROUP * HEAD_DIM), F32),
        ],
        compiler_params=pltpu.CompilerParams(dimension_semantics=("arbitrary", "arbitrary"),
                                             vmem_limit_bytes=VMEM_LIMIT),
        name="rwkv",
    )(zr, *params)


def _key_to_float(key):
    return lax.bitcast_convert_type(key ^ ((key >> 31) & INT_MAX), F32)


def _dsa_kernel(qi_ref, q_ref, kiw_ref, k_ref, v_ref, band_ref, lng_ref, lnb_ref, o_ref,
                ki3_s, wb_s, key_s, keyt_s, thr_s, jlim_s, lg_s, mx_s, mb_s, sm_s, pv_s, *, topk, nb):
    i = pl.program_id(1)
    sp = kiw_ref.shape[1]
    far_lim = jnp.clip(CHUNK * (i - FAR_GAP), 0, sp - KEY_TILE)
    n_far = (far_lim + KEY_TILE - 1) // KEY_TILE
    n_tiles = n_far + 1
    near_variant = i - far_lim // CHUNK
    lane1 = lax.broadcasted_iota(I32, (CHUNK, LANES), 1)
    halves = [slice(0, LANES), slice(LANES, KEY_TILE)]
    n_pairs = (nb + 1) // 2
    kf = float(topk)

    def tile_pos0(c):
        return jnp.where(c < n_far, c * KEY_TILE, far_lim)

    def tile_end(c):
        return jnp.where(c < n_far, far_lim, CHUNK * (i + 1))

    def tile_row0(c):
        return pl.multiple_of(tile_pos0(c), CHUNK)

    def half_pos(c, half):
        return tile_pos0(c) + half * LANES + lane1

    @pl.when(i == 0)
    def _():
        rows = KEY_TILE if sp % KEY_TILE == 0 else CHUNK

        def body(t, carry):
            bb = t // (sp // rows)
            r0 = pl.multiple_of((t % (sp // rows)) * rows, CHUNK)
            ki = kiw_ref[bb, pl.ds(r0, rows), :][:, :IDX_DIM]
            mean = jnp.mean(ki, axis=-1, keepdims=True)
            var = jnp.mean(jnp.square(ki - mean), axis=-1, keepdims=True)
            kn = (ki - mean) * lax.rsqrt(var + LN_EPS) * lng_ref[...] + lnb_ref[...]
            hi, lo = _split2(kn)
            ki3_s[bb, pl.ds(r0, rows), :] = jnp.concatenate([hi, lo, hi], axis=1)
            return carry

        lax.fori_loop(0, nb * (sp // rows), body, 0)

    def stage_queries(bb, carry):
        wq = kiw_ref[bb, pl.ds(pl.multiple_of(CHUNK * i, CHUNK), CHUNK), :]
        wscale = IDX_HEADS ** -0.5 * IDX_DIM ** -0.5
        wcol = jnp.concatenate([wq[:, IDX_DIM + h:IDX_DIM + h + 1] for h in range(IDX_HEADS)], axis=0) * wscale
        wb_s[bb] = jnp.broadcast_to(wcol, (IDX_HEADS * CHUNK, LANES))
        return carry

    lax.fori_loop(0, nb, stage_queries, 0)

    def score_tile(c, carry):
        r0 = tile_row0(c)
        end = tile_end(c)
        xs = [_nt(qi_ref[:, bb].reshape(IDX_HEADS * CHUNK, 3 * IDX_DIM), ki3_s[bb, pl.ds(r0, KEY_TILE), :])
              for bb in range(nb)]
        keys = []
        for bb in range(nb):
            wb = wb_s[bb]
            parts = []
            for half, hs_ in enumerate(halves):
                act = jnp.maximum(xs[bb][:, hs_], 0.0) * wb
                sc = act[0:CHUNK]
                for h in range(1, IDX_HEADS):
                    sc = sc + act[h * CHUNK:(h + 1) * CHUNK]
                pos = half_pos(c, half)
                valid = (pos >= 0) & (pos < end)
                parts.append(jnp.where(valid, sc, -jnp.inf))
            keys.append(jnp.concatenate(parts, axis=1))
            key_s[bb, c] = keys[bb]
        for pr in range(n_pairs):
            both = jnp.concatenate([keys[2 * pr], keys[min(2 * pr + 1, nb - 1)]], axis=0)
            keyt_s[pr, c] = both.T
        return carry

    lax.fori_loop(0, n_tiles, score_tile, 0)

    thr_s[...] = jnp.full(thr_s.shape, -jnp.inf, F32)
    jlim_s[...] = jnp.full(jlim_s.shape, INT_MAX, I32)

    part = 32

    def count(cands, above):
        wide = [jnp.broadcast_to(cd, (part, LANES)) for cd in cands]

        def body(c, accs):
            out = []
            for pr in range(n_pairs):
                acc = accs[pr]
                for k in range(KEY_TILE // part):
                    sc = keyt_s[pr, c, k * part:(k + 1) * part, :]
                    acc = jnp.where(above(sc, wide[pr]), acc + 1.0, acc)
                out.append(acc)
            return tuple(out)

        accs = lax.fori_loop(0, n_tiles, body, tuple(jnp.zeros((part, LANES), F32) for _ in range(n_pairs)))
        return [jnp.sum(a, axis=0, keepdims=True) for a in accs]

    def count_ge(keys):
        return count([_key_to_float(k) for k in keys], lambda s, c: s >= c)

    @pl.when(CHUNK * (i + 1) > topk)
    def _():
        n_all = float(KEY_TILE) * n_tiles.astype(F32)
        n_pos = count_ge([jnp.zeros((1, LANES), I32)] * n_pairs)
        ans0 = tuple(jnp.where(n >= kf, 0, INT_MIN).astype(I32) for n in n_pos)
        cnt0 = tuple(jnp.where(n >= kf, n, n_all) for n in n_pos)

        def one_bit(t, carry):
            ans, n_ans = carry
            bit = lax.shift_left(jnp.int32(1), 30 - t)
            cands = [a | bit for a in ans]
            cnts = count_ge(cands)
            take = [n >= kf for n in cnts]
            return (tuple(jnp.where(tk, cd, a) for tk, cd, a in zip(take, cands, ans)),
                    tuple(jnp.where(tk, n, na) for tk, n, na in zip(take, cnts, n_ans)))

        ans, n_ans = lax.fori_loop(0, 31, one_bit, (ans0, cnt0))
        def to_columns(rows_, dst):
            for pr in range(n_pairs):
                col = jnp.broadcast_to(rows_[pr], (LANES, LANES)).T
                dst[2 * pr] = col[:CHUNK]
                if 2 * pr + 1 < nb:
                    dst[2 * pr + 1] = col[CHUNK:]

        kth = [_key_to_float(a) for a in ans]
        to_columns(kth, thr_s)
        most = n_ans[0]
        for pr in range(1, n_pairs):
            most = jnp.maximum(most, n_ans[pr])

        @pl.when(jnp.max(most) > kf)
        def _():
            n_gt = count(kth, lambda s, c: s > c)
            need = [kf - n for n in n_gt]
            ans_w = [jnp.broadcast_to(a, (KEY_TILE, LANES)) for a in kth]
            krow = lax.broadcasted_iota(I32, (KEY_TILE, LANES), 0)
            tri = (lax.broadcasted_iota(I32, (KEY_TILE, KEY_TILE), 0)
                   >= lax.broadcasted_iota(I32, (KEY_TILE, KEY_TILE), 1)).astype(BF16)

            def body(c, carry):
                seen, last = carry
                pos = tile_pos0(c) + krow
                seen_out, last_out = [], []
                for pr in range(n_pairs):
                    tie = keyt_s[pr, c] == ans_w[pr]
                    rank = _mm(tri, jnp.where(tie, 1.0, 0.0).astype(BF16)) + seen[pr]
                    kept = jnp.where(tie & (rank <= need[pr]), pos, -1)
                    last_out.append(jnp.maximum(last[pr], jnp.max(kept, axis=0, keepdims=True)))
                    seen_out.append(rank[KEY_TILE - 1:KEY_TILE, :])
                return tuple(seen_out), tuple(last_out)

            init = (tuple(jnp.zeros((1, LANES), F32) for _ in range(n_pairs)),
                    tuple(jnp.full((1, LANES), -1, I32) for _ in range(n_pairs)))
            _, last = lax.fori_loop(0, n_tiles, body, init)
            to_columns(last, jlim_s)

    pair_lane_head = lane1 // HEAD_DIM

    cw = lg_s.shape[0]

    def phase_c(step, carry):
        bbs = [step * cw + j for j in range(cw)]
        zero = jnp.zeros((CHUNK, LANES), BF16)
        qm = []
        for bb in bbs:
            qs = q_ref[bb] * jnp.asarray(HEAD_DIM ** -0.5, BF16)
            qm.append([jnp.concatenate([jnp.where(pair_lane_head == 0, qs[:, g * LANES:(g + 1) * LANES], zero),
                                        jnp.where(pair_lane_head == 1, qs[:, g * LANES:(g + 1) * LANES], zero)],
                                       axis=0) for g in range(HEAD_PAIRS)])
        thr = [thr_s[bb] for bb in bbs]
        jlim = [jlim_s[bb] for bb in bbs]
        mx_s[...] = jnp.full(mx_s.shape, NEG_INF, F32)

        def logits(c, bias):
            r0 = tile_row0(c)
            end = tile_end(c)
            kt = [k_ref[bb, pl.ds(r0, KEY_TILE), :] for bb in bbs]
            s = [[_nt(qm[j][g], kt[j][:, g * LANES:(g + 1) * LANES]) for g in range(HEAD_PAIRS)]
                 for j in range(cw)]
            for j, bb in enumerate(bbs):
                sel = []
                for half, hs_ in enumerate(halves):
                    key = key_s[bb, c, :, hs_]
                    pos = half_pos(c, half)
                    sel.append((pos >= 0) & (pos < end) & ((key > thr[j]) | ((key == thr[j]) & (pos <= jlim[j]))))
                for g in range(HEAD_PAIRS):
                    sg = s[j][g]
                    if bias is not None:
                        sg = sg + bias[g * 2 * CHUNK:(g + 1) * 2 * CHUNK]
                    for hh in range(2):
                        rows = slice((2 * g + hh) * CHUNK, (2 * g + hh + 1) * CHUNK)
                        sub = sg[hh * CHUNK:(hh + 1) * CHUNK]
                        m0 = jnp.where(sel[0], sub[:, halves[0]], NEG_INF)
                        m1 = jnp.where(sel[1], sub[:, halves[1]], NEG_INF)
                        lg_s[j, c, rows, halves[0]] = m0
                        lg_s[j, c, rows, halves[1]] = m1
                        mx_s[j, rows] = jnp.maximum(mx_s[j, rows], jnp.maximum(m0, m1))

        def far_logits(c, carry):
            logits(c, None)
            return carry

        lax.fori_loop(0, n_far, far_logits, 0)
        logits(n_far, band_ref[near_variant])

        for j in range(cw):
            mb_s[j] = jnp.broadcast_to(jnp.max(mx_s[j], axis=1, keepdims=True), mb_s.shape[1:])
        sm_s[...] = jnp.zeros_like(sm_s)
        pv_s[...] = jnp.zeros_like(pv_s)

        ones = jnp.ones((KEY_TILE, LANES), BF16)

        def weigh(c, carry):
            r0 = tile_row0(c)
            vt = [v_ref[bb, pl.ds(r0, KEY_TILE), :] for bb in bbs]
            for g in range(HEAD_PAIRS):
                rows = slice(g * 2 * CHUNK, (g + 1) * 2 * CHUNK)
                ps = []
                for j in range(cw):
                    m = mb_s[j, rows]
                    p0 = jnp.exp(lg_s[j, c, rows, halves[0]] - m)
                    p1 = jnp.exp(lg_s[j, c, rows, halves[1]] - m)
                    ps.append(jnp.concatenate([p0, p1], axis=1).astype(BF16))
                for j in range(cw):
                    rhs = jnp.concatenate([vt[j][:, g * LANES:(g + 1) * LANES], ones], axis=1)
                    both = _mm(ps[j], rhs)
                    pv_s[j, rows] = pv_s[j, rows] + both[:, :LANES]
                    sm_s[j, rows] = sm_s[j, rows] + both[:, LANES:]
            return carry

        lax.fori_loop(0, n_tiles, weigh, 0)

        for j, bb in enumerate(bbs):
            cols = []
            for g in range(HEAD_PAIRS):
                top, bot = slice((2 * g) * CHUNK, (2 * g + 1) * CHUNK), slice((2 * g + 1) * CHUNK, (2 * g + 2) * CHUNK)
                num = jnp.where(pair_lane_head == 0, pv_s[j, top], pv_s[j, bot])
                den = jnp.where(pair_lane_head == 0, sm_s[j, top], sm_s[j, bot])
                cols.append(num / den)
            o_ref[bb] = jnp.concatenate(cols, axis=1).astype(BF16)
        return carry

    lax.fori_loop(0, nb // cw, phase_c, 0)


def _dsa(qi_t, q, kiw_p, k_p, v_p, band, ln_g, ln_b):
    b, s, _ = q.shape
    sp = k_p.shape[1]
    topk = min(MAX_TOPK, s // 4)
    assert s % KEY_TILE == 0, "the near window needs at least one whole key tile"
    max_tiles = s // KEY_TILE
    nb = 4 if b % 4 == 0 else (2 if b % 2 == 0 else 1)
    cw = nb
    rows = HEADS * CHUNK
    full = lambda a: pl.BlockSpec(a.shape, lambda i, j: (0,) * a.ndim)
    per_group = lambda n: pl.BlockSpec((nb, sp, n), lambda i, j: (i, 0, 0), pipeline_mode=pl.Buffered(1))
    return pl.pallas_call(
        functools.partial(_dsa_kernel, topk=topk, nb=nb),
        grid=(b // nb, s // CHUNK),
        in_specs=[
            pl.BlockSpec((IDX_HEADS, nb, CHUNK, 3 * IDX_DIM), lambda i, j: (0, i, j, 0)),
            pl.BlockSpec((nb, CHUNK, WIDTH), lambda i, j: (i, j, 0)),
            per_group(LANES), per_group(WIDTH), per_group(WIDTH),
            full(band), full(ln_g), full(ln_b),
        ],
        out_specs=pl.BlockSpec((nb, CHUNK, WIDTH), lambda i, j: (i, j, 0)),
        out_shape=jax.ShapeDtypeStruct((b, s, WIDTH), BF16),
        scratch_shapes=[
            pltpu.VMEM((nb, sp, 3 * IDX_DIM), BF16),
            pltpu.VMEM((nb, rows, LANES), F32),
            pltpu.VMEM((nb, max_tiles, CHUNK, KEY_TILE), F32),
            pltpu.VMEM(((nb + 1) // 2, max_tiles, KEY_TILE, 2 * CHUNK), F32),
            pltpu.VMEM((nb, CHUNK, LANES), F32),
            pltpu.VMEM((nb, CHUNK, LANES), I32),
            pltpu.VMEM((cw, max_tiles, rows, KEY_TILE), F32),
            pltpu.VMEM((cw, rows, LANES), F32),
            pltpu.VMEM((cw, rows, LANES), F32),
            pltpu.VMEM((cw, rows, LANES), F32),
            pltpu.VMEM((cw, rows, LANES), F32),
        ],
        compiler_params=pltpu.CompilerParams(dimension_semantics=("arbitrary", "arbitrary"),
                                             vmem_limit_bytes=VMEM_LIMIT),
        name="dsa",
    )(qi_t, q, kiw_p, k_p, v_p, band, ln_g, ln_b)


def _post_kernel(y1_ref, y2_ref, x_ref, p_ref, woa_ref, wob_ref, w1_ref, w2_ref, wp_ref, wg_ref,
                 g_mix_ref, g_f1_ref, g_f2_ref, g_p1_ref, g_p2_ref, o_ref):
    mix = _mm(y1_ref[...], woa_ref[...]) + _mm(y2_ref[...], wob_ref[...])
    x1 = x_ref[...] + _rms(mix, g_mix_ref[...])
    h = _rms(x1, g_f1_ref[...]).astype(BF16)
    hid = jnp.square(jnp.maximum(_mm(h, w1_ref[...]), 0.0)).astype(BF16)
    x2 = x1 + _rms(_mm(hid, w2_ref[...]), g_f2_ref[...])
    gate = _sigmoid(_mm(_rms(x2, g_p1_ref[...]).astype(BF16), wg_ref[...]))
    pe = _mm(p_ref[...].astype(BF16), wp_ref[...]) * gate
    o_ref[...] = x2 + _rms(pe, g_p2_ref[...])


def _post(y1, y2, x2d, p2d, woa, wob, w1, w2, wp, wg, gains, tm=1024):
    m, d = x2d.shape
    row = lambda n: pl.BlockSpec((tm, n), lambda i: (i, 0))
    const = lambda a: pl.BlockSpec(a.shape, lambda i: (0, 0), pipeline_mode=pl.Buffered(1))
    weights = (woa, wob, w1, w2, wp, wg) + tuple(gains)
    return pl.pallas_call(
        _post_kernel,
        grid=(m // tm,),
        in_specs=[row(WIDTH), row(WIDTH), row(d), row(p2d.shape[1])] + [const(a) for a in weights],
        out_specs=row(d),
        out_shape=jax.ShapeDtypeStruct((m, d), F32),
        compiler_params=pltpu.CompilerParams(dimension_semantics=("arbitrary",), vmem_limit_bytes=VMEM_LIMIT),
        name="post",
    )(y1, y2, x2d, p2d, *weights)


def _layer(x, p, w_in, shift_mu, w0, w2r, a0, a2, g2, k_k, k_a, r_k, ln_g, ln_b, idx_g, idx_b, rel_bias,
           w_out, w_f1, w_f2, w_ple, w_gate, n_mix_pre, n_mix_post, n_ffn_pre, n_ffn_post, n_ple_pre, n_ple_post):
    b, s, d = x.shape
    m = b * s
    row = lambda a: a.reshape(1, -1).astype(F32)

    c = [0, WIDTH, WIDTH + DECAY_RANK, 2 * WIDTH + DECAY_RANK, 3 * WIDTH + DECAY_RANK,
         3 * WIDTH + DECAY_RANK + A_RANK, RWKV_COLS]
    order = [(c[0], c[1]), (c[2], c[3]), (c[3], c[4]), (c[1], c[2]), (c[4], c[5]), (c[5], c[6])]
    reorder = lambda a: jnp.concatenate([a[..., lo:hi] for lo, hi in order], axis=-1)
    w_main = jnp.concatenate([reorder(w_in[:, :RWKV_COLS]), w_in[:, RWKV_COLS:RWKV_COLS + 3 * WIDTH]],
                             axis=1).astype(BF16)
    w_idx = w_in[:, RWKV_COLS + 3 * WIDTH:]
    w_idx = jnp.pad(w_idx, ((0, 0), (0, N_IDX - w_idx.shape[1])))
    w_idx_hi = w_idx.astype(BF16)
    w_idx_lo = (w_idx - w_idx_hi.astype(F32)).astype(BF16)

    zr, q, k, v, qi, kiw = _in_proj(x.reshape(m, d), row(n_mix_pre), w_main, w_idx_hi, w_idx_lo)

    zeros = jnp.zeros((DECAY_RANK, WIDTH), F32)
    w_lora = jnp.concatenate([jnp.concatenate([w2r, zeros], axis=1),
                              jnp.concatenate([zeros, a2], axis=1)], axis=0).astype(BF16)
    y_rwkv = _rwkv(zr.reshape(b, s, RWKV_COLS), row(reorder(shift_mu)), row(w0), w_lora, row(a0), g2.astype(BF16),
                   row(k_k), row(k_a), row(r_k), row(ln_g), row(ln_b))

    band = _bias_tiles(rel_bias.astype(F32))
    per_row = lambda a: a.reshape(b, s, -1)
    y_dsa = _dsa(qi.reshape(IDX_HEADS, b, s, 3 * IDX_DIM), per_row(q), per_row(kiw), per_row(k), per_row(v), band,
                 row(idx_g), row(idx_b))

    gains = tuple(row(a) for a in (n_mix_post, n_ffn_pre, n_ffn_post, n_ple_pre, n_ple_post))
    out = _post(y_rwkv.reshape(m, WIDTH), y_dsa.reshape(m, WIDTH), x.reshape(m, d), p.reshape(m, -1),
                w_out[:WIDTH].astype(BF16), w_out[WIDTH:].astype(BF16), w_f1.astype(BF16), w_f2.astype(BF16),
                w_ple.astype(BF16), w_gate.astype(BF16), gains)
    return out.reshape(b, s, d)


def kernel(x, p, w_in, shift_mu, rwkv_w0, rwkv_w2, rwkv_a0, rwkv_a2, rwkv_g2, rwkv_k_k, rwkv_k_a, rwkv_r_k, rwkv_ln_g, rwkv_ln_b, idx_k_ln_g, idx_k_ln_b, rel_bias, w_out, w_ffn1, w_ffn2, w_ple, w_ple_gate, norm_mix_pre, norm_mix_post, norm_ffn_pre, norm_ffn_post, norm_ple_pre, norm_ple_post):
    for i in range(p.shape[0]):
        x = _layer(x, p[i], w_in[i], shift_mu[i], rwkv_w0[i], rwkv_w2[i], rwkv_a0[i], rwkv_a2[i], rwkv_g2[i],
                   rwkv_k_k[i], rwkv_k_a[i], rwkv_r_k[i], rwkv_ln_g[i], rwkv_ln_b[i], idx_k_ln_g[i],
                   idx_k_ln_b[i], rel_bias, w_out[i], w_ffn1[i], w_ffn2[i], w_ple[i], w_ple_gate[i],
                   norm_mix_pre[i], norm_mix_post[i], norm_ffn_pre[i], norm_ffn_post[i], norm_ple_pre[i],
                   norm_ple_post[i])
    return x
```
